```python
import math
import jax, jax.numpy as jnp
from jax import lax
import numpy as np

D_MODEL = 1024
BATCH = 8
SEQ = 4096
DEPTH = 2

GRID_W = 64
HEAD_DIM = 64
BLOCK = 128
A_HEADS = 8
A_KV = 2
B_HEADS = 8
B_KV = 2
WINDOW = 128
C_HEADS = 16
NA_ROWS = 8
NA_COLS = 16
MEM_LEN = 256
M_HEADS = 4
M_HEAD_DIM = 128
MEM_W = M_HEADS * M_HEAD_DIM
REL_BUCKETS = 32
REL_MAX_DIST = 128
ROPE_THETA = 10000.0
EPS = 1e-6
MIX_EVEN = A_HEADS * HEAD_DIM + B_HEADS * HEAD_DIM + MEM_W
MIX_ODD = C_HEADS * HEAD_DIM + MEM_W
SPLIT_EVEN = [A_HEADS * HEAD_DIM, A_KV * HEAD_DIM, A_KV * HEAD_DIM,
              B_HEADS * HEAD_DIM, B_KV * HEAD_DIM, B_KV * HEAD_DIM,
              MEM_W, MIX_EVEN]
SPLIT_ODD = [C_HEADS * HEAD_DIM, C_HEADS * HEAD_DIM, C_HEADS * HEAD_DIM,
             MEM_W, MIX_ODD]
IN_EVEN = sum(SPLIT_EVEN)
IN_ODD = sum(SPLIT_ODD)
N_EVEN = (DEPTH + 1) // 2
N_ODD = DEPTH // 2

kernel_name = "hybrid_grid_encoder_block"


def rmsnorm(x, g):
    xf = x.astype(jnp.float32)
    y = xf * lax.rsqrt(jnp.mean(xf * xf, axis=-1, keepdims=True) + EPS)
    return (y * g.astype(jnp.float32)).astype(x.dtype)


def split_cols(z, sizes):
    offs = [int(v) for v in np.cumsum(sizes)[:-1]]
    return jnp.split(z, offs, axis=-1)


def _rotate_axis(xa, ang):
    x1, x2 = jnp.split(xa, 2, axis=-1)
    c = jnp.cos(ang)[None, :, None, :]
    s = jnp.sin(ang)[None, :, None, :]
    return jnp.concatenate([x1 * c - x2 * s, x1 * s + x2 * c], axis=-1)


def rope_2d(x, row, col):
    half = x.shape[-1] // 2
    nf = half // 2
    freqs = jnp.power(ROPE_THETA, -jnp.arange(nf, dtype=jnp.float32) / nf)
    xf = x.astype(jnp.float32)
    ang_r = row.astype(jnp.float32)[:, None] * freqs
    ang_c = col.astype(jnp.float32)[:, None] * freqs
    out = jnp.concatenate([_rotate_axis(xf[..., :half], ang_r),
                           _rotate_axis(xf[..., half:], ang_c)], axis=-1)
    return out.astype(x.dtype)


def t5_bucket(rel):
    nb = REL_BUCKETS // 2
    max_exact = nb // 2
    ret = jnp.where(rel > 0, nb, 0)
    n = jnp.abs(rel)
    nf = jnp.maximum(n, 1).astype(jnp.float32)
    large = max_exact + (jnp.log(nf / max_exact) / math.log(REL_MAX_DIST / max_exact)
                         * (nb - max_exact)).astype(jnp.int32)
    large = jnp.minimum(large, nb - 1)
    return ret + jnp.where(n < max_exact, n, large)


def global_gqa(q, k, v):
    B, S, Hq, dh = q.shape
    Hkv = k.shape[2]
    G = Hq // Hkv
    nb = S // BLOCK
    scale = dh ** -0.5
    qb = q.reshape(B, nb, BLOCK, Hkv, G, dh).transpose(1, 0, 2, 3, 4, 5)

    def one(qblk):
        s = jnp.einsum('bqkgd,bskd->bkgqs', qblk, k).astype(jnp.float32) * scale
        p = jax.nn.softmax(s, axis=-1).astype(v.dtype)
        return jnp.einsum('bkgqs,bskd->bqkgd', p, v)

    o = lax.map(one, qb)
    return o.transpose(1, 0, 2, 3, 4, 5).reshape(B, S, Hq * dh)


def window_gqa(q, k, v, sink, rel_bias):
    B, S, Hq, dh = q.shape
    Hkv = k.shape[2]
    G = Hq // Hkv
    nb = S // BLOCK
    span = BLOCK + 2 * WINDOW
    scale = dh ** -0.5
    kp = jnp.pad(k, ((0, 0), (WINDOW, WINDOW), (0, 0), (0, 0)))
    vp = jnp.pad(v, ((0, 0), (WINDOW, WINDOW), (0, 0), (0, 0)))
    qpos = jnp.arange(BLOCK)
    kpos = jnp.arange(span) - WINDOW
    rel = kpos[None, :] - qpos[:, None]
    band = jnp.abs(rel) <= WINDOW
    bias = rel_bias.astype(jnp.float32)[t5_bucket(rel)]
    bias = bias.transpose(2, 0, 1).reshape(Hkv, G, BLOCK, span)
    sk = sink.astype(jnp.float32).reshape(Hkv, G)[None, :, :, None, None]
    qb = q.reshape(B, nb, BLOCK, Hkv, G, dh).transpose(1, 0, 2, 3, 4, 5)

    def one(args):
        i, qblk = args
        start = i * BLOCK
        kb = lax.dynamic_slice_in_dim(kp, start, span, axis=1)
        vb = lax.dynamic_slice_in_dim(vp, start, span, axis=1)
        apos = start + kpos
        valid = band & (apos >= 0)[None, :] & (apos < S)[None, :]
        s = jnp.einsum('bqkgd,bskd->bkgqs', qblk, kb).astype(jnp.float32) * scale + bias
        s = jnp.where(valid, s, -jnp.inf)
        m = jnp.maximum(jnp.max(s, axis=-1, keepdims=True), sk)
        p = jnp.exp(s - m)
        denom = jnp.sum(p, axis=-1, keepdims=True) + jnp.exp(sk - m)
        return jnp.einsum('bkgqs,bskd->bqkgd', (p / denom).astype(v.dtype), vb)

    o = lax.map(one, (jnp.arange(nb), qb))
    return o.transpose(1, 0, 2, 3, 4, 5).reshape(B, S, Hq * dh)


def neighborhood_attn(q, k, v, rpb):
    B, S, H, dh = q.shape
    rows = S // GRID_W
    kr = min(NA_ROWS, rows)
    scale = dh ** -0.5
    qg = q.reshape(B, rows, GRID_W, H, dh)
    kg = k.reshape(B, rows, GRID_W, H, dh)
    vg = v.reshape(B, rows, GRID_W, H, dh)
    col = jnp.arange(GRID_W)
    cs = jnp.clip(col - NA_COLS // 2, 0, GRID_W - NA_COLS)
    colmask = (col[None, :] >= cs[:, None]) & (col[None, :] < cs[:, None] + NA_COLS)
    dc = jnp.clip(col[None, :] - col[:, None] + NA_COLS - 1, 0, 2 * NA_COLS - 2)
    rpb_cols = rpb.astype(jnp.float32)[:, :, dc]

    def one(r):
        rs = jnp.clip(r - kr // 2, 0, rows - kr)
        kblk = lax.dynamic_slice_in_dim(kg, rs, kr, axis=1)
        vblk = lax.dynamic_slice_in_dim(vg, rs, kr, axis=1)
        qblk = lax.dynamic_index_in_dim(qg, r, axis=1, keepdims=False)
        dr = rs + jnp.arange(kr) - r + NA_ROWS - 1
        bias = jnp.take(rpb_cols, dr, axis=1).transpose(0, 2, 1, 3)
        s = jnp.einsum('bqhd,bikhd->bhqik', qblk, kblk).astype(jnp.float32) * scale + bias
        s = jnp.where(colmask[:, None, :], s, -jnp.inf)
        p = jax.nn.softmax(s.reshape(B, H, GRID_W, kr * GRID_W), axis=-1)
        p = p.reshape(B, H, GRID_W, kr, GRID_W).astype(v.dtype)
        return jnp.einsum('bhqik,bikhd->bqhd', p, vblk)

    o = lax.map(one, jnp.arange(rows))
    return o.transpose(1, 0, 2, 3, 4).reshape(B, S, H * dh)


def memory_attn(q, mk, mv):
    B, S, Hm, dm = q.shape
    s = jnp.einsum('bshd,bmhd->bhsm', q, mk).astype(jnp.float32) * (dm ** -0.5)
    p = jax.nn.softmax(s, axis=-1).astype(mv.dtype)
    return jnp.einsum('bhsm,bmhd->bshd', p, mv).reshape(B, S, Hm * dm)


def setup_inputs(seed: int = 0) -> dict:
    key = jax.random.key(seed)
    ks = jax.random.split(key, 16)
    f32 = jnp.float32
    nrm = lambda k, shp: jax.random.normal(k, shp, dtype=f32)
    return {
        "x": nrm(ks[0], (BATCH, SEQ, D_MODEL)),
        "mem": nrm(ks[1], (BATCH, MEM_LEN, D_MODEL)),
        "norm_gain": 1.0 + 0.05 * nrm(ks[2], (DEPTH, D_MODEL)),
        "mem_norm_gain": 1.0 + 0.05 * nrm(ks[3], (D_MODEL,)),
        "w_in_even": nrm(ks[4], (N_EVEN, D_MODEL, IN_EVEN)) * D_MODEL ** -0.5,
        "w_out_even": nrm(ks[5], (N_EVEN, MIX_EVEN, D_MODEL)) * MIX_EVEN ** -0.5,
        "q_norm_a": 1.0 + 0.05 * nrm(ks[6], (N_EVEN, HEAD_DIM)),
        "k_norm_a": 1.0 + 0.05 * nrm(ks[7], (N_EVEN, HEAD_DIM)),
        "sink_b": 0.5 * nrm(ks[8], (N_EVEN, B_HEADS)),
        "rel_bias": 0.5 * nrm(ks[9], (REL_BUCKETS, B_HEADS)),
        "w_in_odd": nrm(ks[10], (N_ODD, D_MODEL, IN_ODD)) * D_MODEL ** -0.5,
        "w_out_odd": nrm(ks[11], (N_ODD, MIX_ODD, D_MODEL)) * MIX_ODD ** -0.5,
        "rpb_c": 0.5 * nrm(ks[12], (N_ODD, C_HEADS, 2 * NA_ROWS - 1, 2 * NA_COLS - 1)),
        "w_mem_kv": nrm(ks[13], (DEPTH, D_MODEL, 2 * MEM_W)) * D_MODEL ** -0.5,
        "final_norm_gain": 1.0 + 0.05 * nrm(ks[14], (D_MODEL,)),
    }


def reference(x, mem, norm_gain, mem_norm_gain, w_in_even, w_out_even, q_norm_a, k_norm_a,
              sink_b, rel_bias, w_in_odd, w_out_odd, rpb_c, w_mem_kv, final_norm_gain):
    B, S, _ = x.shape
    t = jnp.arange(S)
    grid_row = t // GRID_W
    grid_col = t % GRID_W
    memn = rmsnorm(mem, mem_norm_gain)
    Mlen = mem.shape[1]
    for l in range(DEPTH):
        h = rmsnorm(x, norm_gain[l])
        mk, mv = jnp.split(memn @ w_mem_kv[l], 2, axis=-1)
        mk = mk.reshape(B, Mlen, M_HEADS, M_HEAD_DIM)
        mv = mv.reshape(B, Mlen, M_HEADS, M_HEAD_DIM)
        if l % 2 == 0:
            e = l // 2
            qa, ka, va, qb, kb, vb, qm, gate = split_cols(h @ w_in_even[e], SPLIT_EVEN)
            qa = rope_2d(rmsnorm(qa.reshape(B, S, A_HEADS, HEAD_DIM), q_norm_a[e]), grid_row, grid_col)
            ka = rope_2d(rmsnorm(ka.reshape(B, S, A_KV, HEAD_DIM), k_norm_a[e]), grid_row, grid_col)
            ya = global_gqa(qa, ka, va.reshape(B, S, A_KV, HEAD_DIM))
            yb = window_gqa(qb.reshape(B, S, B_HEADS, HEAD_DIM),
                            kb.reshape(B, S, B_KV, HEAD_DIM),
                            vb.reshape(B, S, B_KV, HEAD_DIM), sink_b[e], rel_bias)
            ym = memory_attn(qm.reshape(B, S, M_HEADS, M_HEAD_DIM), mk, mv)
            y = jnp.concatenate([ya, yb, ym], axis=-1) * jax.nn.silu(gate)
            x = x + y @ w_out_even[e]
        else:
            o = l // 2
            qc, kc, vc, qm, gate = split_cols(h @ w_in_odd[o], SPLIT_ODD)
            yc = neighborhood_attn(qc.reshape(B, S, C_HEADS, HEAD_DIM),
                                   kc.reshape(B, S, C_HEADS, HEAD_DIM),
                                   vc.reshape(B, S, C_HEADS, HEAD_DIM), rpb_c[o])
            ym = memory_attn(qm.reshape(B, S, M_HEADS, M_HEAD_DIM), mk, mv)
            y = jnp.concatenate([yc, ym], axis=-1) * jax.nn.silu(gate)
            x = x + y @ w_out_odd[o]
    return rmsnorm(x, final_norm_gain)
```

```python
import functools
import math

import jax
import jax.numpy as jnp
import numpy as np
from jax import lax
from jax.experimental import pallas as pl
from jax.experimental.pallas import tpu as pltpu

GRID_W = 64
HEAD_DIM = 64
A_HEADS = 8
A_KV = 2
B_HEADS = 8
B_KV = 2
WINDOW = 128
C_HEADS = 16
NA_ROWS = 8
NA_COLS = 16
M_HEADS = 4
M_HEAD_DIM = 128
MEM_W = M_HEADS * M_HEAD_DIM
REL_BUCKETS = 32
REL_MAX_DIST = 128
ROPE_THETA = 10000.0
EPS = 1e-6

LANES = 128
NEG = -1e30
VMEM_LIMIT_BYTES = 56 * 1024 * 1024

F32 = jnp.float32
BF16 = jnp.bfloat16

_NT = (((1,), (1,)), ((), ()))


def _params(*sem):
    return pltpu.CompilerParams(dimension_semantics=sem, vmem_limit_bytes=VMEM_LIMIT_BYTES)


def _rms_rows(x, gain):
    ms = jnp.mean(x * x, axis=-1, keepdims=True)
    return x * lax.rsqrt(ms + EPS) * gain


def _lane_iota(shape):
    return lax.broadcasted_iota(jnp.int32, shape, len(shape) - 1)


def _silu(x):
    return x * (1.0 / (1.0 + jnp.exp(-x)))


def _half_ones():
    r = lax.broadcasted_iota(jnp.int32, (LANES, LANES), 0) // HEAD_DIM
    c = lax.broadcasted_iota(jnp.int32, (LANES, LANES), 1) // HEAD_DIM
    return jnp.where(r == c, 1.0, 0.0).astype(BF16)


def _head_norm_rope(t, gain, cos, sin_signed, ones_bd):
    t2 = t * t
    hi = t2.astype(BF16)
    lo = (t2 - hi.astype(F32)).astype(BF16)
    ss = (jnp.dot(hi, ones_bd, preferred_element_type=F32)
          + jnp.dot(lo, ones_bd, preferred_element_type=F32))
    tn = t * lax.rsqrt(ss * (1.0 / HEAD_DIM) + EPS) * gain
    lane = _lane_iota(tn.shape)
    quarter = HEAD_DIM // 4
    partner = jnp.where((lane % (2 * quarter)) < quarter,
                        pltpu.roll(tn, LANES - quarter, 1), pltpu.roll(tn, quarter, 1))
    return tn * cos + partner * sin_signed


def _store_padded_heads(q_tiles, out_ref, n_heads, n_kv):
    group = n_heads // n_kv
    for h in range(n_heads):
        t = q_tiles[h // 2]
        src_half = h % 2
        dst_half = (h // group) % 2
        if src_half != dst_half:
            t = pltpu.roll(t, HEAD_DIM, 1)
        lane = _lane_iota(t.shape)
        keep = (lane >= HEAD_DIM) if dst_half == 1 else (lane < HEAD_DIM)
        out_ref[:, h * LANES:(h + 1) * LANES] = jnp.where(keep, t, 0.0).astype(BF16)


def _mem_kv_kernel(mem_ref, g_ref, w_ref, o_ref):
    h = _rms_rows(mem_ref[...], g_ref[...]).astype(BF16)
    o_ref[...] = jnp.dot(h, w_ref[...], preferred_element_type=F32).astype(BF16)


def _mem_kv(mem2d, gain, w_bf16, tm):
    depth, d, n = w_bf16.shape
    rows = mem2d.shape[0]
    return pl.pallas_call(
        _mem_kv_kernel,
        out_shape=jax.ShapeDtypeStruct((depth, rows, n), BF16),
        grid=(depth, rows // tm),
        in_specs=[pl.BlockSpec((tm, d), lambda l, i: (i, 0)),
                  pl.BlockSpec((1, d), lambda l, i: (0, 0)),
                  pl.BlockSpec((None, d, n), lambda l, i: (l, 0, 0))],
        out_specs=pl.BlockSpec((None, tm, n), lambda l, i: (l, i, 0)),
        compiler_params=_params("arbitrary", "arbitrary"),
        name="mem_kv_proj",
    )(mem2d, gain, w_bf16)


def _in_even_kernel(x_ref, g_ref, w_ref, gq_ref, gk_ref, cos_ref, sin_ref,
                    qa_ref, ka_ref, va_ref, qb_ref, kb_ref, vb_ref, qm_ref,
                    ga_ref, gb_ref, gm_ref):
    h = _rms_rows(x_ref[...], g_ref[...]).astype(BF16)

    def seg(lo, hi):
        return jnp.dot(h, w_ref[:, lo:hi], preferred_element_type=F32)

    ones_bd = _half_ones()
    cos = cos_ref[...]
    sin = sin_ref[...]
    scale = HEAD_DIM ** -0.5
    qa_w = A_HEADS * HEAD_DIM
    kva_w = A_KV * HEAD_DIM
    qb_w = B_HEADS * HEAD_DIM
    kvb_w = B_KV * HEAD_DIM
    o = 0
    zq = seg(o, o + qa_w)
    tiles = [_head_norm_rope(zq[:, j * LANES:(j + 1) * LANES], gq_ref[...], cos, sin, ones_bd) * scale
             for j in range(qa_w // LANES)]
    _store_padded_heads(tiles, qa_ref, A_HEADS, A_KV)
    o += qa_w
    ka_ref[...] = _head_norm_rope(seg(o, o + kva_w), gk_ref[...], cos, sin, ones_bd).astype(BF16)
    o += kva_w
    va_ref[...] = seg(o, o + kva_w).astype(BF16)
    o += kva_w
    zq = seg(o, o + qb_w) * scale
    _store_padded_heads([zq[:, j * LANES:(j + 1) * LANES] for j in range(qb_w // LANES)],
                        qb_ref, B_HEADS, B_KV)
    o += qb_w
    kb_ref[...] = seg(o, o + kvb_w).astype(BF16)
    o += kvb_w
    vb_ref[...] = seg(o, o + kvb_w).astype(BF16)
    o += kvb_w
    qm_ref[...] = (seg(o, o + MEM_W) * (M_HEAD_DIM ** -0.5)).astype(BF16)
    o += MEM_W
    ga_ref[...] = _silu(seg(o, o + qa_w)).astype(BF16)
    o += qa_w
    gb_ref[...] = _silu(seg(o, o + qb_w)).astype(BF16)
    o += qb_w
    gm_ref[...] = _silu(seg(o, o + MEM_W)).astype(BF16)


def _in_even(x2d, gain, w_bf16, gq, gk, cos, sin, seq, tm):
    n_tok, d = x2d.shape
    n_in = w_bf16.shape[1]
    per_seq = seq // tm
    widths = [A_HEADS * LANES, A_KV * HEAD_DIM, A_KV * HEAD_DIM,
              B_HEADS * LANES, B_KV * HEAD_DIM, B_KV * HEAD_DIM, MEM_W,
              A_HEADS * HEAD_DIM, B_HEADS * HEAD_DIM, MEM_W]
    row = lambda i: (i, 0)
    const = lambda i: (0, 0)
    return pl.pallas_call(
        _in_even_kernel,
        out_shape=[jax.ShapeDtypeStruct((n_tok, w), BF16) for w in widths],
        grid=(n_tok // tm,),
        in_specs=[pl.BlockSpec((tm, d), row),
                  pl.BlockSpec((1, d), const),
                  pl.BlockSpec((d, n_in), const),
                  pl.BlockSpec((1, LANES), const),
                  pl.BlockSpec((1, LANES), const),
                  pl.BlockSpec((tm, LANES), lambda i: (i % per_seq, 0)),
                  pl.BlockSpec((tm, LANES), lambda i: (i % per_seq, 0))],
        out_specs=[pl.BlockSpec((tm, w), row) for w in widths],
        compiler_params=_params("arbitrary"),
        name="in_proj_even",
    )(x2d, gain, w_bf16, gq, gk, cos, sin)


def _in_odd_kernel(x_ref, g_ref, w_ref, qc_ref, kc_ref, vc_ref, qm_ref, gc_ref, gm_ref):
    h = _rms_rows(x_ref[...], g_ref[...]).astype(BF16)

    def seg(lo, hi):
        return jnp.dot(h, w_ref[:, lo:hi], preferred_element_type=F32)

    cw = C_HEADS * HEAD_DIM
    o = 0
    qc_ref[...] = (seg(o, o + cw) * (HEAD_DIM ** -0.5)).astype(BF16)
    o += cw
    kc_ref[...] = seg(o, o + cw).astype(BF16)
    o += cw
    vc_ref[...] = seg(o, o + cw).astype(BF16)
    o += cw
    qm_ref[...] = (seg(o, o + MEM_W) * (M_HEAD_DIM ** -0.5)).astype(BF16)
    o += MEM_W
    gc_ref[...] = _silu(seg(o, o + cw)).astype(BF16)
    o += cw
    gm_ref[...] = _silu(seg(o, o + MEM_W)).astype(BF16)


def _in_odd(x2d, gain, w_bf16, tm):
    n_tok, d = x2d.shape
    n_in = w_bf16.shape[1]
    cw = C_HEADS * HEAD_DIM
    widths = [cw, cw, cw, MEM_W, cw, MEM_W]
    row = lambda i: (i, 0)
    const = lambda i: (0, 0)
    return pl.pallas_call(
        _in_odd_kernel,
        out_shape=[jax.ShapeDtypeStruct((n_tok, w), BF16) for w in widths],
        grid=(n_tok // tm,),
        in_specs=[pl.BlockSpec((tm, d), row),
                  pl.BlockSpec((1, d), const),
                  pl.BlockSpec((d, n_in), const)],
        out_specs=[pl.BlockSpec((tm, w), row) for w in widths],
        compiler_params=_params("arbitrary"),
        name="in_proj_odd",
    )(x2d, gain, w_bf16)


def _unpad_heads_store(o_rows, gate_ref, out_ref, n_heads, n_kv, tq):
    group = n_heads // n_kv
    for j in range(n_heads // 2):
        parts = []
        for h in (2 * j, 2 * j + 1):
            g = h // group
            i = h % group
            t = o_rows[g][i * tq:(i + 1) * tq, :]
            if (g % 2) != (h % 2):
                t = pltpu.roll(t, HEAD_DIM, 1)
            parts.append(t)
        lane = _lane_iota(parts[0].shape)
        tile = jnp.where(lane < HEAD_DIM, parts[0], parts[1])
        gate = gate_ref[:, j * LANES:(j + 1) * LANES].astype(F32)
        out_ref[:, j * LANES:(j + 1) * LANES] = (tile * gate).astype(BF16)


def _global_attn_kernel(q_ref, k_ref, v_ref, gate_ref, o_ref, m_scr, l_scr, acc_scr, *, tk):
    tq = q_ref.shape[0]
    seq = k_ref.shape[0]
    group = A_HEADS // A_KV
    o_rows = []
    for g in range(A_KV):
        q = jnp.concatenate([q_ref[:, (g * group + i) * LANES:(g * group + i + 1) * LANES]
                             for i in range(group)], axis=0)
        m_scr[...] = jnp.full(m_scr.shape, NEG, F32)
        l_scr[...] = jnp.zeros(l_scr.shape, F32)
        acc_scr[...] = jnp.zeros(acc_scr.shape, F32)

        def body(c, carry):
            start = pl.multiple_of(c * tk, tk)
            k = k_ref[pl.ds(start, tk), :]
            v = v_ref[pl.ds(start, tk), :]
            s = lax.dot_general(q, k, _NT, preferred_element_type=F32)
            m_old = m_scr[...]
            m_new = jnp.maximum(m_old, jnp.max(s, axis=-1, keepdims=True))
            alpha = jnp.exp(m_old - m_new)
            p = jnp.exp(s - m_new)
            l_scr[...] = alpha * l_scr[...] + jnp.sum(p, axis=-1, keepdims=True)
            acc_scr[...] = alpha * acc_scr[...] + jnp.dot(p.astype(BF16), v,
                                                          preferred_element_type=F32)
            m_scr[...] = m_new
            return carry

        lax.fori_loop(0, seq // tk, body, 0)
        o_rows.append(acc_scr[...] * (1.0 / l_scr[...]))
    _unpad_heads_store(o_rows, gate_ref, o_ref, A_HEADS, A_KV, tq)


def _global_attn(qa, ka, va, gate, tq, tk):
    b, s, _ = qa.shape
    rows = (A_HEADS // A_KV) * tq
    return pl.pallas_call(
        functools.partial(_global_attn_kernel, tk=tk),
        out_shape=jax.ShapeDtypeStruct((b, s, A_HEADS * HEAD_DIM), BF16),
        grid=(b, s // tq),
        in_specs=[pl.BlockSpec((None, tq, A_HEADS * LANES), lambda bi, i: (bi, i, 0)),
                  pl.BlockSpec((None, s, LANES), lambda bi, i: (bi, 0, 0)),
                  pl.BlockSpec((None, s, LANES), lambda bi, i: (bi, 0, 0)),
                  pl.BlockSpec((None, tq, A_HEADS * HEAD_DIM), lambda bi, i: (bi, i, 0))],
        out_specs=pl.BlockSpec((None, tq, A_HEADS * HEAD_DIM), lambda bi, i: (bi, i, 0)),
        scratch_shapes=[pltpu.VMEM((rows, 1), F32), pltpu.VMEM((rows, 1), F32),
                        pltpu.VMEM((rows, LANES), F32)],
        compiler_params=_params("arbitrary", "arbitrary"),
        name="global_attn",
    )(qa, ka, va, gate)


def _window_attn_kernel(q_ref, k_ref, v_ref, bias_ref, sink_ref, gate_ref, o_ref):
    blk = q_ref.shape[0]
    seq = k_ref.shape[0]
    nb = seq // blk
    i = pl.program_id(1)
    group = B_HEADS // B_KV
    left = pl.multiple_of(jnp.maximum(i - 1, 0) * blk, blk)
    mid = pl.multiple_of(i * blk, blk)
    right = pl.multiple_of(jnp.minimum(i + 1, nb - 1) * blk, blk)
    k = jnp.concatenate([k_ref[pl.ds(left, blk), :], k_ref[pl.ds(mid, blk), :],
                         k_ref[pl.ds(right, blk), :]], axis=0)
    v = jnp.concatenate([v_ref[pl.ds(left, blk), :], v_ref[pl.ds(mid, blk), :],
                         v_ref[pl.ds(right, blk), :]], axis=0)
    col = _lane_iota((1, 3 * blk))
    pen_left = jnp.where(i == 0, NEG, 0.0)
    pen_right = jnp.where(i == nb - 1, NEG, 0.0)
    edge = jnp.where(col < blk, pen_left, 0.0) + jnp.where(col >= 2 * blk, pen_right, 0.0)
    o_rows = []
    for g in range(B_KV):
        q = jnp.concatenate([q_ref[:, (g * group + j) * LANES:(g * group + j + 1) * LANES]
                             for j in range(group)], axis=0)
        s = lax.dot_general(q, k, _NT, preferred_element_type=F32) + bias_ref[g] + edge
        sink = sink_ref[g]
        m = jnp.maximum(jnp.max(s, axis=-1, keepdims=True), sink)
        p = jnp.exp(s - m)
        denom = jnp.sum(p, axis=-1, keepdims=True) + jnp.exp(sink - m)
        o = jnp.dot(p.astype(BF16), v, preferred_element_type=F32)
        o_rows.append(o * (1.0 / denom))
    _unpad_heads_store(o_rows, gate_ref, o_ref, B_HEADS, B_KV, blk)


def _window_attn(qb, kb, vb, bias, sink_rows, gate, blk):
    b, s, _ = qb.shape
    rows = (B_HEADS // B_KV) * blk
    return pl.pallas_call(
        _window_attn_kernel,
        out_shape=jax.ShapeDtypeStruct((b, s, B_HEADS * HEAD_DIM), BF16),
        grid=(b, s // blk),
        in_specs=[pl.BlockSpec((None, blk, B_HEADS * LANES), lambda bi, i: (bi, i, 0)),
                  pl.BlockSpec((None, s, LANES), lambda bi, i: (bi, 0, 0)),
                  pl.BlockSpec((None, s, LANES), lambda bi, i: (bi, 0, 0)),
                  pl.BlockSpec((B_KV, rows, 3 * blk), lambda bi, i: (0, 0, 0)),
                  pl.BlockSpec((B_KV, rows, 1), lambda bi, i: (0, 0, 0)),
                  pl.BlockSpec((None, blk, B_HEADS * HEAD_DIM), lambda bi, i: (bi, i, 0))],
        out_specs=pl.BlockSpec((None, blk, B_HEADS * HEAD_DIM), lambda bi, i: (bi, i, 0)),
        compiler_params=_params("arbitrary", "arbitrary"),
        name="window_attn",
    )(qb, kb, vb, bias, sink_rows, gate)


def _mem_attn_kernel(q_ref, kv_ref, gate_ref, o_ref):
    for h in range(M_HEADS):
        lo, hi = h * M_HEAD_DIM, (h + 1) * M_HEAD_DIM
        q = q_ref[:, lo:hi]
        k = kv_ref[:, lo:hi]
        v = kv_ref[:, MEM_W + lo:MEM_W + hi]
        s = lax.dot_general(q, k, _NT, preferred_element_type=F32)
        m = jnp.max(s, axis=-1, keepdims=True)
        p = jnp.exp(s - m)
        l = jnp.sum(p, axis=-1, keepdims=True)
        o = jnp.dot(p.astype(BF16), v, preferred_element_type=F32) * (1.0 / l)
        o_ref[:, lo:hi] = (o * gate_ref[:, lo:hi].astype(F32)).astype(BF16)


def _mem_attn(qm, mkv, layer, gate, tq):
    b, s, _ = qm.shape
    mlen = mkv.shape[2]
    return pl.pallas_call(
        _mem_attn_kernel,
        out_shape=jax.ShapeDtypeStruct((b, s, MEM_W), BF16),
        grid=(b, s // tq),
        in_specs=[pl.BlockSpec((None, tq, MEM_W), lambda bi, i: (bi, i, 0)),
                  pl.BlockSpec((None, None, mlen, 2 * MEM_W), lambda bi, i: (layer, bi, 0, 0)),
                  pl.BlockSpec((None, tq, MEM_W), lambda bi, i: (bi, i, 0))],
        out_specs=pl.BlockSpec((None, tq, MEM_W), lambda bi, i: (bi, i, 0)),
        compiler_params=_params("arbitrary", "arbitrary"),
        name="mem_attn",
    )(qm, mkv, gate)


def _nbr_attn_kernel(q_ref, k_ref, v_ref, bias_ref, gate_ref, o_ref, *, grid_rows):
    r = pl.program_id(1)
    rs = jnp.clip(r - NA_ROWS // 2, 0, grid_rows - NA_ROWS)
    start = pl.multiple_of(rs * GRID_W, GRID_W)
    nkeys = NA_ROWS * GRID_W
    for j in range(C_HEADS // 2):
        lo, hi = j * LANES, (j + 1) * LANES
        qt = q_ref[:, lo:hi]
        lane = _lane_iota(qt.shape)
        zero = jnp.zeros_like(qt)
        q = jnp.concatenate([jnp.where(lane < HEAD_DIM, qt, zero),
                             jnp.where(lane >= HEAD_DIM, qt, zero)], axis=0)
        k = k_ref[pl.ds(start, nkeys), lo:hi]
        v = v_ref[pl.ds(start, nkeys), lo:hi]
        s = lax.dot_general(q, k, _NT, preferred_element_type=F32) + bias_ref[j]
        m = jnp.max(s, axis=-1, keepdims=True)
        p = jnp.exp(s - m)
        l = jnp.sum(p, axis=-1, keepdims=True)
        o = jnp.dot(p.astype(BF16), v, preferred_element_type=F32) * (1.0 / l)
        lane_o = _lane_iota((GRID_W, LANES))
        tile = jnp.where(lane_o < HEAD_DIM, o[:GRID_W, :], o[GRID_W:, :])
        o_ref[:, lo:hi] = (tile * gate_ref[:, lo:hi].astype(F32)).astype(BF16)


def _nbr_attn(qc, kc, vc, bias, gate):
    b, s, w = qc.shape
    grid_rows = s // GRID_W
    nkeys = NA_ROWS * GRID_W
    half = NA_ROWS // 2

    def bias_idx(bi, r):
        return (r - jnp.clip(r - half, 0, grid_rows - NA_ROWS), 0, 0, 0)

    return pl.pallas_call(
        functools.partial(_nbr_attn_kernel, grid_rows=grid_rows),
        out_shape=jax.ShapeDtypeStruct((b, s, w), BF16),
        grid=(b, grid_rows),
        in_specs=[pl.BlockSpec((None, GRID_W, w), lambda bi, r: (bi, r, 0)),
                  pl.BlockSpec((None, s, w), lambda bi, r: (bi, 0, 0)),
                  pl.BlockSpec((None, s, w), lambda bi, r: (bi, 0, 0)),
                  pl.BlockSpec((None, C_HEADS // 2, 2 * GRID_W, nkeys), bias_idx),
                  pl.BlockSpec((None, GRID_W, w), lambda bi, r: (bi, r, 0))],
        out_specs=pl.BlockSpec((None, GRID_W, w), lambda bi, r: (bi, r, 0)),
        compiler_params=_params("arbitrary", "arbitrary"),
        name="nbr_attn",
    )(qc, kc, vc, bias, gate)


def _out_proj_kernel(*refs, n_parts, final_norm):
    y_refs = refs[:n_parts]
    x_ref, w_ref = refs[n_parts], refs[n_parts + 1]
    if final_norm:
        g_ref, o_ref = refs[n_parts + 2], refs[n_parts + 3]
    else:
        o_ref = refs[n_parts + 2]
    acc = x_ref[...]
    off = 0
    for y_ref in y_refs:
        width = y_ref.shape[1]
        acc = acc + jnp.dot(y_ref[...], w_ref[off:off + width, :], preferred_element_type=F32)
        off += width
    if final_norm:
        acc = _rms_rows(acc, g_ref[...])
    o_ref[...] = acc


def _out_proj(ys, x2d, w_bf16, tm, final_gain=None):
    n_tok, d = x2d.shape
    row = lambda i: (i, 0)
    const = lambda i: (0, 0)
    in_specs = [pl.BlockSpec((tm, y.shape[1]), row) for y in ys]
    in_specs += [pl.BlockSpec((tm, d), row), pl.BlockSpec(w_bf16.shape, const)]
    args = list(ys) + [x2d, w_bf16]
    if final_gain is not None:
        in_specs.append(pl.BlockSpec((1, d), const))
        args.append(final_gain)
    return pl.pallas_call(
        functools.partial(_out_proj_kernel, n_parts=len(ys), final_norm=final_gain is not None),
        out_shape=jax.ShapeDtypeStruct((n_tok, d), F32),
        grid=(n_tok // tm,),
        in_specs=in_specs,
        out_specs=pl.BlockSpec((tm, d), row),
        compiler_params=_params("arbitrary"),
        name="out_proj_final" if final_gain is not None else "out_proj",
    )(*args)


def _rope_tables(seq):
    quarter = HEAD_DIM // 4
    freqs = jnp.power(ROPE_THETA, -jnp.arange(quarter, dtype=F32) / quarter)
    t = jnp.arange(seq)
    ang_r = (t // GRID_W).astype(F32)[:, None] * freqs
    ang_c = (t % GRID_W).astype(F32)[:, None] * freqs
    cos_h = jnp.concatenate([jnp.cos(ang_r), jnp.cos(ang_r), jnp.cos(ang_c), jnp.cos(ang_c)], axis=-1)
    sin_h = jnp.concatenate([-jnp.sin(ang_r), jnp.sin(ang_r), -jnp.sin(ang_c), jnp.sin(ang_c)], axis=-1)
    return jnp.tile(cos_h, (1, 2)), jnp.tile(sin_h, (1, 2))


def _t5_bucket(rel):
    nb = REL_BUCKETS // 2
    max_exact = nb // 2
    ret = jnp.where(rel > 0, nb, 0)
    n = jnp.abs(rel)
    nf = jnp.maximum(n, 1).astype(F32)
    large = max_exact + (jnp.log(nf / max_exact) / math.log(REL_MAX_DIST / max_exact)
                         * (nb - max_exact)).astype(jnp.int32)
    large = jnp.minimum(large, nb - 1)
    return ret + jnp.where(n < max_exact, n, large)


def _window_bias(rel_bias, blk):
    span = blk + 2 * WINDOW
    rel = (jnp.arange(span) - WINDOW)[None, :] - jnp.arange(blk)[:, None]
    band = jnp.abs(rel) <= WINDOW
    bias = rel_bias.astype(F32)[_t5_bucket(rel)]
    bias = jnp.where(band[:, :, None], bias, NEG)
    group = B_HEADS // B_KV
    return bias.transpose(2, 0, 1).reshape(B_KV, group * blk, span)


def _nbr_bias(rpb):
    col = np.arange(GRID_W)
    cs = np.clip(col - NA_COLS // 2, 0, GRID_W - NA_COLS)
    colmask = (col[None, :] >= cs[:, None]) & (col[None, :] < cs[:, None] + NA_COLS)
    dc = np.clip(col[None, :] - col[:, None] + NA_COLS - 1, 0, 2 * NA_COLS - 2)
    delta = np.arange(NA_ROWS)[:, None]
    dr = np.arange(NA_ROWS)[None, :] - delta + NA_ROWS - 1
    t = rpb.astype(F32)[:, jnp.asarray(dr)]
    t = t[:, :, :, jnp.asarray(dc)]
    t = jnp.where(jnp.asarray(colmask)[None, None, None], t, NEG)
    t = t.transpose(1, 0, 3, 2, 4)
    return t.reshape(NA_ROWS, C_HEADS // 2, 2 * GRID_W, NA_ROWS * GRID_W)


def kernel(x, mem, norm_gain, mem_norm_gain, w_in_even, w_out_even, q_norm_a, k_norm_a, sink_b,
           rel_bias, w_in_odd, w_out_odd, rpb_c, w_mem_kv, final_norm_gain):
    b, s, d = x.shape
    mlen = mem.shape[1]
    assert s % GRID_W == 0 and s // GRID_W >= NA_ROWS and s % 512 == 0
    tm = 512
    blk = 128

    x2d = x.reshape(b * s, d)
    mem_tm = math.gcd(b * mlen, 512)
    mkv = _mem_kv(mem.reshape(b * mlen, d), mem_norm_gain.reshape(1, d), w_mem_kv.astype(BF16), mem_tm)
    mkv = mkv.reshape(w_mem_kv.shape[0], b, mlen, 2 * MEM_W)

    cos, sin = _rope_tables(s)
    gq = jnp.tile(q_norm_a[0].astype(F32), 2).reshape(1, LANES)
    gk = jnp.tile(k_norm_a[0].astype(F32), 2).reshape(1, LANES)
    qa, ka, va, qb, kb, vb, qm, ga, gb, gm = _in_even(
        x2d, norm_gain[0].reshape(1, d), w_in_even[0].astype(BF16), gq, gk, cos, sin, s, tm)
    r3 = lambda a: a.reshape(b, s, a.shape[-1])
    ya = _global_attn(r3(qa), r3(ka), r3(va), r3(ga), tq=128, tk=512)
    group_b = B_HEADS // B_KV
    sink_rows = jnp.repeat(sink_b[0].astype(F32).reshape(B_KV, group_b), blk, axis=1).reshape(B_KV, group_b * blk, 1)
    yb = _window_attn(r3(qb), r3(kb), r3(vb), _window_bias(rel_bias, blk), sink_rows, r3(gb), blk)
    ym = _mem_attn(r3(qm), mkv, 0, r3(gm), tq=512)
    x1 = _out_proj([ya.reshape(b * s, -1), yb.reshape(b * s, -1), ym.reshape(b * s, -1)],
                   x2d, w_out_even[0].astype(BF16), tm)

    qc, kc, vc, qm1, gc, gm1 = _in_odd(x1, norm_gain[1].reshape(1, d), w_in_odd[0].astype(BF16), tm)
    yc = _nbr_attn(r3(qc), r3(kc), r3(vc), _nbr_bias(rpb_c[0]), r3(gc))
    ym1 = _mem_attn(r3(qm1), mkv, 1, r3(gm1), tq=512)
    out = _out_proj([yc.reshape(b * s, -1), ym1.reshape(b * s, -1)], x1, w_out_odd[0].astype(BF16), tm,
                    final_gain=final_norm_gain.reshape(1, d))
    return out.reshape(b, s, d)
```

```python
import functools
import math

import jax
import jax.numpy as jnp
import numpy as np
from jax import lax
from jax.experimental import pallas as pl
from jax.experimental.pallas import tpu as pltpu

GRID_W = 64
HEAD_DIM = 64
A_HEADS = 8
A_KV = 2
B_HEADS = 8
B_KV = 2
WINDOW = 128
C_HEADS = 16
NA_ROWS = 8
NA_COLS = 16
M_HEADS = 4
M_HEAD_DIM = 128
MEM_W = M_HEADS * M_HEAD_DIM
REL_BUCKETS = 32
REL_MAX_DIST = 128
ROPE_THETA = 10000.0
EPS = 1e-6

LANES = 128
NEG = -1e30
LOG2E = math.log2(math.e)
VMEM_LIMIT_BYTES = 56 * 1024 * 1024

F32 = jnp.float32
BF16 = jnp.bfloat16

_NT = (((1,), (1,)), ((), ()))


def _params(*sem):
    return pltpu.CompilerParams(dimension_semantics=sem, vmem_limit_bytes=VMEM_LIMIT_BYTES)


def _rms_rows(x, gain):
    ms = jnp.mean(x * x, axis=-1, keepdims=True)
    return x * lax.rsqrt(ms + EPS) * gain


def _lane_iota(shape):
    return lax.broadcasted_iota(jnp.int32, shape, len(shape) - 1)


def _silu(x):
    return x * (1.0 / (1.0 + jnp.exp(-x)))


def _half_ones():
    r = lax.broadcasted_iota(jnp.int32, (LANES, LANES), 0) // HEAD_DIM
    c = lax.broadcasted_iota(jnp.int32, (LANES, LANES), 1) // HEAD_DIM
    return jnp.where(r == c, 1.0, 0.0).astype(BF16)


def _head_norm_rope(t, gain, cos, sin_signed, ones_bd):
    t2 = t * t
    hi = t2.astype(BF16)
    lo = (t2 - hi.astype(F32)).astype(BF16)
    ss = (jnp.dot(hi, ones_bd, preferred_element_type=F32)
          + jnp.dot(lo, ones_bd, preferred_element_type=F32))
    tn = t * lax.rsqrt(ss * (1.0 / HEAD_DIM) + EPS) * gain
    lane = _lane_iota(tn.shape)
    quarter = HEAD_DIM // 4
    partner = jnp.where((lane % (2 * quarter)) < quarter,
                        pltpu.roll(tn, LANES - quarter, 1), pltpu.roll(tn, quarter, 1))
    return tn * cos + partner * sin_signed


def _store_padded_heads(q_tiles, out_ref, n_heads, n_kv):
    group = n_heads // n_kv
    for h in range(n_heads):
        t = q_tiles[h // 2]
        src_half = h % 2
        dst_half = (h // group) % 2
        if src_half != dst_half:
            t = pltpu.roll(t, HEAD_DIM, 1)
        lane = _lane_iota(t.shape)
        keep = (lane >= HEAD_DIM) if dst_half == 1 else (lane < HEAD_DIM)
        out_ref[:, h * LANES:(h + 1) * LANES] = jnp.where(keep, t, 0.0).astype(BF16)


def _mem_kv_kernel(mem_ref, g_ref, w_ref, o_ref):
    h = _rms_rows(mem_ref[...], g_ref[...]).astype(BF16)
    o_ref[...] = jnp.dot(h, w_ref[...], preferred_element_type=F32).astype(BF16)


def _mem_kv(mem2d, gain, w_bf16, tm):
    depth, d, n = w_bf16.shape
    rows = mem2d.shape[0]
    return pl.pallas_call(
        _mem_kv_kernel,
        out_shape=jax.ShapeDtypeStruct((depth, rows, n), BF16),
        grid=(depth, rows // tm),
        in_specs=[pl.BlockSpec((tm, d), lambda l, i: (i, 0)),
                  pl.BlockSpec((1, d), lambda l, i: (0, 0)),
                  pl.BlockSpec((None, d, n), lambda l, i: (l, 0, 0))],
        out_specs=pl.BlockSpec((None, tm, n), lambda l, i: (l, i, 0)),
        compiler_params=_params("arbitrary", "arbitrary"),
        name="mem_kv_proj",
    )(mem2d, gain, w_bf16)


def _in_even_kernel(x_ref, g_ref, w_ref, gq_ref, gk_ref, cos_ref, sin_ref,
                    qa_ref, ka_ref, va_ref, qb_ref, kb_ref, vb_ref, qm_ref,
                    ga_ref, gb_ref, gm_ref):
    h = _rms_rows(x_ref[...], g_ref[...]).astype(BF16)

    def seg(lo, hi):
        return jnp.dot(h, w_ref[:, lo:hi], preferred_element_type=F32)

    ones_bd = _half_ones()
    cos = cos_ref[...]
    sin = sin_ref[...]
    scale = HEAD_DIM ** -0.5
    qa_w = A_HEADS * HEAD_DIM
    kva_w = A_KV * HEAD_DIM
    qb_w = B_HEADS * HEAD_DIM
    kvb_w = B_KV * HEAD_DIM
    o = 0
    zq = seg(o, o + qa_w)
    tiles = [_head_norm_rope(zq[:, j * LANES:(j + 1) * LANES], gq_ref[...], cos, sin, ones_bd)
             * (scale * LOG2E) for j in range(qa_w // LANES)]
    _store_padded_heads(tiles, qa_ref, A_HEADS, A_KV)
    o += qa_w
    ka_ref[...] = _head_norm_rope(seg(o, o + kva_w), gk_ref[...], cos, sin, ones_bd).astype(BF16)
    o += kva_w
    va_ref[...] = seg(o, o + kva_w).T.astype(BF16)
    o += kva_w
    zq = seg(o, o + qb_w) * scale
    _store_padded_heads([zq[:, j * LANES:(j + 1) * LANES] for j in range(qb_w // LANES)],
                        qb_ref, B_HEADS, B_KV)
    o += qb_w
    kb_ref[...] = seg(o, o + kvb_w).astype(BF16)
    o += kvb_w
    vb_ref[...] = seg(o, o + kvb_w).astype(BF16)
    o += kvb_w
    qm_ref[...] = (seg(o, o + MEM_W) * (M_HEAD_DIM ** -0.5)).astype(BF16)
    o += MEM_W
    ga_ref[...] = _silu(seg(o, o + qa_w)).astype(BF16)
    o += qa_w
    gb_ref[...] = _silu(seg(o, o + qb_w)).astype(BF16)
    o += qb_w
    gm_ref[...] = _silu(seg(o, o + MEM_W)).astype(BF16)


def _in_even(x2d, gain, w_bf16, gq, gk, cos, sin, seq, tm):
    n_tok, d = x2d.shape
    n_in = w_bf16.shape[1]
    per_seq = seq // tm
    widths = [A_HEADS * LANES, A_KV * HEAD_DIM, A_KV * HEAD_DIM,
              B_HEADS * LANES, B_KV * HEAD_DIM, B_KV * HEAD_DIM, MEM_W,
              A_HEADS * HEAD_DIM, B_HEADS * HEAD_DIM, MEM_W]
    row = lambda i: (i, 0)
    const = lambda i: (0, 0)
    out_shape = [jax.ShapeDtypeStruct((n_tok, w), BF16) for w in widths]
    out_specs = [pl.BlockSpec((tm, w), row) for w in widths]
    out_shape[2] = jax.ShapeDtypeStruct((n_tok // tm, A_KV * HEAD_DIM, tm), BF16)
    out_specs[2] = pl.BlockSpec((None, A_KV * HEAD_DIM, tm), lambda i: (i, 0, 0))
    return pl.pallas_call(
        _in_even_kernel,
        out_shape=out_shape,
        grid=(n_tok // tm,),
        in_specs=[pl.BlockSpec((tm, d), row),
                  pl.BlockSpec((1, d), const),
                  pl.BlockSpec((d, n_in), const),
                  pl.BlockSpec((1, LANES), const),
                  pl.BlockSpec((1, LANES), const),
                  pl.BlockSpec((tm, LANES), lambda i: (i % per_seq, 0)),
                  pl.BlockSpec((tm, LANES), lambda i: (i % per_seq, 0))],
        out_specs=out_specs,
        compiler_params=_params("arbitrary"),
        name="in_proj_even",
    )(x2d, gain, w_bf16, gq, gk, cos, sin)


def _in_odd_kernel(x_ref, g_ref, w_ref, qc_ref, kc_ref, vc_ref, qm_ref, gc_ref, gm_ref):
    h = _rms_rows(x_ref[...], g_ref[...]).astype(BF16)

    def seg(lo, hi):
        return jnp.dot(h, w_ref[:, lo:hi], preferred_element_type=F32)

    cw = C_HEADS * HEAD_DIM
    o = 0
    qc_ref[...] = (seg(o, o + cw) * (HEAD_DIM ** -0.5)).astype(BF16)
    o += cw
    kc_ref[...] = seg(o, o + cw).astype(BF16)
    o += cw
    vc_ref[...] = seg(o, o + cw).astype(BF16)
    o += cw
    qm_ref[...] = (seg(o, o + MEM_W) * (M_HEAD_DIM ** -0.5)).astype(BF16)
    o += MEM_W
    gc_ref[...] = _silu(seg(o, o + cw)).astype(BF16)
    o += cw
    gm_ref[...] = _silu(seg(o, o + MEM_W)).astype(BF16)


def _in_odd(x2d, gain, w_bf16, tm):
    n_tok, d = x2d.shape
    n_in = w_bf16.shape[1]
    cw = C_HEADS * HEAD_DIM
    widths = [cw, cw, cw, MEM_W, cw, MEM_W]
    row = lambda i: (i, 0)
    const = lambda i: (0, 0)
    return pl.pallas_call(
        _in_odd_kernel,
        out_shape=[jax.ShapeDtypeStruct((n_tok, w), BF16) for w in widths],
        grid=(n_tok // tm,),
        in_specs=[pl.BlockSpec((tm, d), row),
                  pl.BlockSpec((1, d), const),
                  pl.BlockSpec((d, n_in), const)],
        out_specs=[pl.BlockSpec((tm, w), row) for w in widths],
        compiler_params=_params("arbitrary"),
        name="in_proj_odd",
    )(x2d, gain, w_bf16)


def _unpad_heads_store(o_rows, gate_ref, out_ref, n_heads, n_kv, tq):
    group = n_heads // n_kv
    for j in range(n_heads // 2):
        parts = []
        for h in (2 * j, 2 * j + 1):
            g = h // group
            i = h % group
            t = o_rows[g][i * tq:(i + 1) * tq, :]
            if (g % 2) != (h % 2):
                t = pltpu.roll(t, HEAD_DIM, 1)
            parts.append(t)
        lane = _lane_iota(parts[0].shape)
        tile = jnp.where(lane < HEAD_DIM, parts[0], parts[1])
        gate = gate_ref[:, j * LANES:(j + 1) * LANES].astype(F32)
        out_ref[:, j * LANES:(j + 1) * LANES] = (tile * gate).astype(BF16)


def _global_attn_kernel(q_ref, k_ref, vt_ref, gate_ref, o_ref, m_scr, l_scr, acc_scr, s_scr, mc_scr):
    tq = q_ref.shape[0]
    n_chunks = vt_ref.shape[0]
    tk = vt_ref.shape[2]
    group = A_HEADS // A_KV
    qs = [jnp.concatenate([q_ref[:, (g * group + i) * LANES:(g * group + i + 1) * LANES]
                           for i in range(group)], axis=0) for g in range(A_KV)]
    m_scr[...] = jnp.full(m_scr.shape, NEG, F32)
    l_scr[...] = jnp.zeros(l_scr.shape, F32)
    acc_scr[...] = jnp.zeros(acc_scr.shape, F32)

    def scores(c, slot):
        start = pl.multiple_of(c * tk, tk)
        k = k_ref[pl.ds(start, tk), :]
        for g in range(A_KV):
            st = lax.dot_general(k, qs[g], _NT, preferred_element_type=F32)
            s_scr[slot, g] = st
            mc_scr[slot, g] = jnp.max(st, axis=0, keepdims=True)

    def accumulate(c, slot):
        vt = vt_ref[c]
        for g in range(A_KV):
            m_old = m_scr[g]
            m_new = jnp.maximum(m_old, mc_scr[slot, g])
            alpha = jnp.exp2(m_old - m_new)
            pt = jnp.exp2(s_scr[slot, g] - m_new)
            l_scr[g] = alpha * l_scr[g] + jnp.sum(pt, axis=0, keepdims=True)
            lo, hi = g * HEAD_DIM, (g + 1) * HEAD_DIM
            acc_scr[g, lo:hi] = alpha * acc_scr[g, lo:hi] + jnp.dot(
                vt[lo:hi], pt.astype(BF16), preferred_element_type=F32)
            m_scr[g] = m_new

    scores(0, 0)

    def body(c2, carry):
        c = 2 * c2
        scores(c + 1, 1)
        accumulate(c, 0)
        scores(jnp.minimum(c + 2, n_chunks - 1), 0)
        accumulate(c + 1, 1)
        return carry

    lax.fori_loop(0, n_chunks // 2, body, 0)
    o_rows = [(acc_scr[g] * (1.0 / l_scr[g])).T for g in range(A_KV)]
    _unpad_heads_store(o_rows, gate_ref, o_ref, A_HEADS, A_KV, tq)


def _global_attn(qa, ka, vat, gate, tq):
    b, s, _ = qa.shape
    tk = vat.shape[-1]
    n_chunks = s // tk
    rows = (A_HEADS // A_KV) * tq
    vat = vat.reshape(b, n_chunks, A_KV * HEAD_DIM, tk)
    return pl.pallas_call(
        _global_attn_kernel,
        out_shape=jax.ShapeDtypeStruct((b, s, A_HEADS * HEAD_DIM), BF16),
        grid=(b, s // tq),
        in_specs=[pl.BlockSpec((None, tq, A_HEADS * LANES), lambda bi, i: (bi, i, 0)),
                  pl.BlockSpec((None, s, LANES), lambda bi, i: (bi, 0, 0)),
                  pl.BlockSpec((None, n_chunks, A_KV * HEAD_DIM, tk), lambda bi, i: (bi, 0, 0, 0)),
                  pl.BlockSpec((None, tq, A_HEADS * HEAD_DIM), lambda bi, i: (bi, i, 0))],
        out_specs=pl.BlockSpec((None, tq, A_HEADS * HEAD_DIM), lambda bi, i: (bi, i, 0)),
        scratch_shapes=[pltpu.VMEM((A_KV, 1, rows), F32), pltpu.VMEM((A_KV, 1, rows), F32),
                        pltpu.VMEM((A_KV, LANES, rows), F32),
                        pltpu.VMEM((2, A_KV, tk, rows), F32), pltpu.VMEM((2, A_KV, 1, rows), F32)],
        compiler_params=_params("arbitrary", "arbitrary"),
        name="global_attn",
    )(qa, ka, vat, gate)


def _window_attn_kernel(q_ref, k_ref, v_ref, bias_ref, sink_ref, gate_ref, o_ref):
    blk = q_ref.shape[0]
    seq = k_ref.shape[0]
    nb = seq // blk
    i = pl.program_id(1)
    group = B_HEADS // B_KV
    left = pl.multiple_of(jnp.maximum(i - 1, 0) * blk, blk)
    mid = pl.multiple_of(i * blk, blk)
    right = pl.multiple_of(jnp.minimum(i + 1, nb - 1) * blk, blk)
    k = jnp.concatenate([k_ref[pl.ds(left, blk), :], k_ref[pl.ds(mid, blk), :],
                         k_ref[pl.ds(right, blk), :]], axis=0)
    v = jnp.concatenate([v_ref[pl.ds(left, blk), :], v_ref[pl.ds(mid, blk), :],
                         v_ref[pl.ds(right, blk), :]], axis=0)
    col = _lane_iota((1, 3 * blk))
    pen_left = jnp.where(i == 0, NEG, 0.0)
    pen_right = jnp.where(i == nb - 1, NEG, 0.0)
    edge = jnp.where(col < blk, pen_left, 0.0) + jnp.where(col >= 2 * blk, pen_right, 0.0)
    o_rows = []
    for g in range(B_KV):
        q = jnp.concatenate([q_ref[:, (g * group + j) * LANES:(g * group + j + 1) * LANES]
                             for j in range(group)], axis=0)
        s = lax.dot_general(q, k, _NT, preferred_element_type=F32) + bias_ref[g] + edge
        sink = sink_ref[g]
        m = jnp.maximum(jnp.max(s, axis=-1, keepdims=True), sink)
        p = jnp.exp(s - m)
        denom = jnp.sum(p, axis=-1, keepdims=True) + jnp.exp(sink - m)
        o = jnp.dot(p.astype(BF16), v, preferred_element_type=F32)
        o_rows.append(o * (1.0 / denom))
    _unpad_heads_store(o_rows, gate_ref, o_ref, B_HEADS, B_KV, blk)


def _window_attn(qb, kb, vb, bias, sink_rows, gate, blk):
    b, s, _ = qb.shape
    rows = (B_HEADS // B_KV) * blk
    return pl.pallas_call(
        _window_attn_kernel,
        out_shape=jax.ShapeDtypeStruct((b, s, B_HEADS * HEAD_DIM), BF16),
        grid=(b, s // blk),
        in_specs=[pl.BlockSpec((None, blk, B_HEADS * LANES), lambda bi, i: (bi, i, 0)),
                  pl.BlockSpec((None, s, LANES), lambda bi, i: (bi, 0, 0)),
                  pl.BlockSpec((None, s, LANES), lambda bi, i: (bi, 0, 0)),
                  pl.BlockSpec((B_KV, rows, 3 * blk), lambda bi, i: (0, 0, 0)),
                  pl.BlockSpec((B_KV, rows, 1), lambda bi, i: (0, 0, 0)),
                  pl.BlockSpec((None, blk, B_HEADS * HEAD_DIM), lambda bi, i: (bi, i, 0))],
        out_specs=pl.BlockSpec((None, blk, B_HEADS * HEAD_DIM), lambda bi, i: (bi, i, 0)),
        compiler_params=_params("arbitrary", "arbitrary"),
        name="window_attn",
    )(qb, kb, vb, bias, sink_rows, gate)


def _mem_attn_kernel(q_ref, kv_ref, gate_ref, o_ref):
    for h in range(M_HEADS):
        lo, hi = h * M_HEAD_DIM, (h + 1) * M_HEAD_DIM
        q = q_ref[:, lo:hi]
        k = kv_ref[:, lo:hi]
        v = kv_ref[:, MEM_W + lo:MEM_W + hi]
        s = lax.dot_general(q, k, _NT, preferred_element_type=F32)
        m = jnp.max(s, axis=-1, keepdims=True)
        p = jnp.exp(s - m)
        l = jnp.sum(p, axis=-1, keepdims=True)
        o = jnp.dot(p.astype(BF16), v, preferred_element_type=F32) * (1.0 / l)
        o_ref[:, lo:hi] = (o * gate_ref[:, lo:hi].astype(F32)).astype(BF16)


def _mem_attn(qm, mkv, layer, gate, tq):
    b, s, _ = qm.shape
    mlen = mkv.shape[2]
    return pl.pallas_call(
        _mem_attn_kernel,
        out_shape=jax.ShapeDtypeStruct((b, s, MEM_W), BF16),
        grid=(b, s // tq),
        in_specs=[pl.BlockSpec((None, tq, MEM_W), lambda bi, i: (bi, i, 0)),
                  pl.BlockSpec((None, None, mlen, 2 * MEM_W), lambda bi, i: (layer, bi, 0, 0)),
                  pl.BlockSpec((None, tq, MEM_W), lambda bi, i: (bi, i, 0))],
        out_specs=pl.BlockSpec((None, tq, MEM_W), lambda bi, i: (bi, i, 0)),
        compiler_params=_params("arbitrary", "arbitrary"),
        name="mem_attn",
    )(qm, mkv, gate)


def _nbr_attn_kernel(q_ref, k_ref, v_ref, bias_ref, gate_ref, o_ref, *, grid_rows):
    r = pl.program_id(1)
    rs = jnp.clip(r - NA_ROWS // 2, 0, grid_rows - NA_ROWS)
    start = pl.multiple_of(rs * GRID_W, GRID_W)
    nkeys = NA_ROWS * GRID_W
    for j in range(C_HEADS // 2):
        lo, hi = j * LANES, (j + 1) * LANES
        qt = q_ref[:, lo:hi]
        lane = _lane_iota(qt.shape)
        zero = jnp.zeros_like(qt)
        q = jnp.concatenate([jnp.where(lane < HEAD_DIM, qt, zero),
                             jnp.where(lane >= HEAD_DIM, qt, zero)], axis=0)
        k = k_ref[pl.ds(start, nkeys), lo:hi]
        v = v_ref[pl.ds(start, nkeys), lo:hi]
        s = lax.dot_general(q, k, _NT, preferred_element_type=F32) + bias_ref[j]
        m = jnp.max(s, axis=-1, keepdims=True)
        p = jnp.exp(s - m)
        l = jnp.sum(p, axis=-1, keepdims=True)
        o = jnp.dot(p.astype(BF16), v, preferred_element_type=F32) * (1.0 / l)
        lane_o = _lane_iota((GRID_W, LANES))
        tile = jnp.where(lane_o < HEAD_DIM, o[:GRID_W, :], o[GRID_W:, :])
        o_ref[:, lo:hi] = (tile * gate_ref[:, lo:hi].astype(F32)).astype(BF16)


def _nbr_attn(qc, kc, vc, bias, gate):
    b, s, w = qc.shape
    grid_rows = s // GRID_W
    nkeys = NA_ROWS * GRID_W
    half = NA_ROWS // 2

    def bias_idx(bi, r):
        return (r - jnp.clip(r - half, 0, grid_rows - NA_ROWS), 0, 0, 0)

    return pl.pallas_call(
        functools.partial(_nbr_attn_kernel, grid_rows=grid_rows),
        out_shape=jax.ShapeDtypeStruct((b, s, w), BF16),
        grid=(b, grid_rows),
        in_specs=[pl.BlockSpec((None, GRID_W, w), lambda bi, r: (bi, r, 0)),
                  pl.BlockSpec((None, s, w), lambda bi, r: (bi, 0, 0)),
                  pl.BlockSpec((None, s, w), lambda bi, r: (bi, 0, 0)),
                  pl.BlockSpec((None, C_HEADS // 2, 2 * GRID_W, nkeys), bias_idx),
                  pl.BlockSpec((None, GRID_W, w), lambda bi, r: (bi, r, 0))],
        out_specs=pl.BlockSpec((None, GRID_W, w), lambda bi, r: (bi, r, 0)),
        compiler_params=_params("arbitrary", "arbitrary"),
        name="nbr_attn",
    )(qc, kc, vc, bias, gate)


def _out_proj_kernel(*refs, n_parts, final_norm):
    y_refs = refs[:n_parts]
    x_ref, w_ref = refs[n_parts], refs[n_parts + 1]
    if final_norm:
        g_ref, o_ref = refs[n_parts + 2], refs[n_parts + 3]
    else:
        o_ref = refs[n_parts + 2]
    acc = x_ref[...]
    off = 0
    for y_ref in y_refs:
        width = y_ref.shape[1]
        acc = acc + jnp.dot(y_ref[...], w_ref[off:off + width, :], preferred_element_type=F32)
        off += width
    if final_norm:
        acc = _rms_rows(acc, g_ref[...])
    o_ref[...] = acc


def _out_proj(ys, x2d, w_bf16, tm, final_gain=None):
    n_tok, d = x2d.shape
    row = lambda i: (i, 0)
    const = lambda i: (0, 0)
    in_specs = [pl.BlockSpec((tm, y.shape[1]), row) for y in ys]
    in_specs += [pl.BlockSpec((tm, d), row), pl.BlockSpec(w_bf16.shape, const)]
    args = list(ys) + [x2d, w_bf16]
    if final_gain is not None:
        in_specs.append(pl.BlockSpec((1, d), const))
        args.append(final_gain)
    return pl.pallas_call(
        functools.partial(_out_proj_kernel, n_parts=len(ys), final_norm=final_gain is not None),
        out_shape=jax.ShapeDtypeStruct((n_tok, d), F32),
        grid=(n_tok // tm,),
        in_specs=in_specs,
        out_specs=pl.BlockSpec((tm, d), row),
        compiler_params=_params("arbitrary"),
        name="out_proj_final" if final_gain is not None else "out_proj",
    )(*args)


def _rope_tables(seq):
    quarter = HEAD_DIM // 4
    freqs = jnp.power(ROPE_THETA, -jnp.arange(quarter, dtype=F32) / quarter)
    t = jnp.arange(seq)
    ang_r = (t // GRID_W).astype(F32)[:, None] * freqs
    ang_c = (t % GRID_W).astype(F32)[:, None] * freqs
    cos_h = jnp.concatenate([jnp.cos(ang_r), jnp.cos(ang_r), jnp.cos(ang_c), jnp.cos(ang_c)], axis=-1)
    sin_h = jnp.concatenate([-jnp.sin(ang_r), jnp.sin(ang_r), -jnp.sin(ang_c), jnp.sin(ang_c)], axis=-1)
    return jnp.tile(cos_h, (1, 2)), jnp.tile(sin_h, (1, 2))


def _t5_bucket(rel):
    nb = REL_BUCKETS // 2
    max_exact = nb // 2
    ret = jnp.where(rel > 0, nb, 0)
    n = jnp.abs(rel)
    nf = jnp.maximum(n, 1).astype(F32)
    large = max_exact + (jnp.log(nf / max_exact) / math.log(REL_MAX_DIST / max_exact)
                         * (nb - max_exact)).astype(jnp.int32)
    large = jnp.minimum(large, nb - 1)
    return ret + jnp.where(n < max_exact, n, large)


def _window_bias(rel_bias, blk):
    span = blk + 2 * WINDOW
    rel = (jnp.arange(span) - WINDOW)[None, :] - jnp.arange(blk)[:, None]
    band = jnp.abs(rel) <= WINDOW
    bias = rel_bias.astype(F32)[_t5_bucket(rel)]
    bias = jnp.where(band[:, :, None], bias, NEG)
    group = B_HEADS // B_KV
    return bias.transpose(2, 0, 1).reshape(B_KV, group * blk, span)


def _nbr_bias(rpb):
    col = np.arange(GRID_W)
    cs = np.clip(col - NA_COLS // 2, 0, GRID_W - NA_COLS)
    colmask = (col[None, :] >= cs[:, None]) & (col[None, :] < cs[:, None] + NA_COLS)
    dc = np.clip(col[None, :] - col[:, None] + NA_COLS - 1, 0, 2 * NA_COLS - 2)
    delta = np.arange(NA_ROWS)[:, None]
    dr = np.arange(NA_ROWS)[None, :] - delta + NA_ROWS - 1
    t = rpb.astype(F32)[:, jnp.asarray(dr)]
    t = t[:, :, :, jnp.asarray(dc)]
    t = jnp.where(jnp.asarray(colmask)[None, None, None], t, NEG)
    t = t.transpose(1, 0, 3, 2, 4)
    return t.reshape(NA_ROWS, C_HEADS // 2, 2 * GRID_W, NA_ROWS * GRID_W)


def kernel(x, mem, norm_gain, mem_norm_gain, w_in_even, w_out_even, q_norm_a, k_norm_a, sink_b,
           rel_bias, w_in_odd, w_out_odd, rpb_c, w_mem_kv, final_norm_gain):
    b, s, d = x.shape
    mlen = mem.shape[1]
    assert s % GRID_W == 0 and s // GRID_W >= NA_ROWS and s % 512 == 0
    tm = 512
    blk = 128

    x2d = x.reshape(b * s, d)
    mem_tm = math.gcd(b * mlen, 512)
    mkv = _mem_kv(mem.reshape(b * mlen, d), mem_norm_gain.reshape(1, d), w_mem_kv.astype(BF16), mem_tm)
    mkv = mkv.reshape(w_mem_kv.shape[0], b, mlen, 2 * MEM_W)

    cos, sin = _rope_tables(s)
    gq = jnp.tile(q_norm_a[0].astype(F32), 2).reshape(1, LANES)
    gk = jnp.tile(k_norm_a[0].astype(F32), 2).reshape(1, LANES)
    qa, ka, va, qb, kb, vb, qm, ga, gb, gm = _in_even(
        x2d, norm_gain[0].reshape(1, d), w_in_even[0].astype(BF16), gq, gk, cos, sin, s, tm)
    r3 = lambda a: a.reshape(b, s, a.shape[-1])
    ya = _global_attn(r3(qa), r3(ka), va, r3(ga), tq=128)
    group_b = B_HEADS // B_KV
    sink_rows = jnp.repeat(sink_b[0].astype(F32).reshape(B_KV, group_b), blk, axis=1).reshape(B_KV, group_b * blk, 1)
    yb = _window_attn(r3(qb), r3(kb), r3(vb), _window_bias(rel_bias, blk), sink_rows, r3(gb), blk)
    ym = _mem_attn(r3(qm), mkv, 0, r3(gm), tq=512)
    x1 = _out_proj([ya.reshape(b * s, -1), yb.reshape(b * s, -1), ym.reshape(b * s, -1)],
                   x2d, w_out_even[0].astype(BF16), tm)

    qc, kc, vc, qm1, gc, gm1 = _in_odd(x1, norm_gain[1].reshape(1, d), w_in_odd[0].astype(BF16), tm)
    yc = _nbr_attn(r3(qc), r3(kc), r3(vc), _nbr_bias(rpb_c[0]), r3(gc))
    ym1 = _mem_attn(r3(qm1), mkv, 1, r3(gm1), tq=512)
    out = _out_proj([yc.reshape(b * s, -1), ym1.reshape(b * s, -1)], x1, w_out_odd[0].astype(BF16), tm,
                    final_gain=final_norm_gain.reshape(1, d))
    return out.reshape(b, s, d)
```

```python
import functools
import math

import jax
import jax.numpy as jnp
import numpy as np
from jax import lax
from jax.experimental import pallas as pl
from jax.experimental.pallas import tpu as pltpu

GRID_W = 64
HEAD_DIM = 64
A_HEADS = 8
A_KV = 2
B_HEADS = 8
B_KV = 2
WINDOW = 128
C_HEADS = 16
NA_ROWS = 8
NA_COLS = 16
M_HEADS = 4
M_HEAD_DIM = 128
MEM_W = M_HEADS * M_HEAD_DIM
REL_BUCKETS = 32
REL_MAX_DIST = 128
ROPE_THETA = 10000.0
EPS = 1e-6

LANES = 128
NEG = -1e30
LOG2E = math.log2(math.e)
VMEM_LIMIT_BYTES = 56 * 1024 * 1024

F32 = jnp.float32
BF16 = jnp.bfloat16

_NT = (((1,), (1,)), ((), ()))


def _params(*sem):
    return pltpu.CompilerParams(dimension_semantics=sem, vmem_limit_bytes=VMEM_LIMIT_BYTES)


def _rms_rows(x, gain):
    ms = jnp.mean(x * x, axis=-1, keepdims=True)
    return x * lax.rsqrt(ms + EPS) * gain


def _lane_iota(shape):
    return lax.broadcasted_iota(jnp.int32, shape, len(shape) - 1)


def _silu(x):
    return x * (1.0 / (1.0 + jnp.exp(-x)))


def _half_ones():
    r = lax.broadcasted_iota(jnp.int32, (LANES, LANES), 0) // HEAD_DIM
    c = lax.broadcasted_iota(jnp.int32, (LANES, LANES), 1) // HEAD_DIM
    return jnp.where(r == c, 1.0, 0.0).astype(BF16)


def _head_norm_rope(t, gain, cos, sin_signed, ones_bd):
    t2 = t * t
    hi = t2.astype(BF16)
    lo = (t2 - hi.astype(F32)).astype(BF16)
    ss = (jnp.dot(hi, ones_bd, preferred_element_type=F32)
          + jnp.dot(lo, ones_bd, preferred_element_type=F32))
    tn = t * lax.rsqrt(ss * (1.0 / HEAD_DIM) + EPS) * gain
    lane = _lane_iota(tn.shape)
    quarter = HEAD_DIM // 4
    partner = jnp.where((lane % (2 * quarter)) < quarter,
                        pltpu.roll(tn, LANES - quarter, 1), pltpu.roll(tn, quarter, 1))
    return tn * cos + partner * sin_signed


def _store_padded_heads(q_tiles, out_ref, n_heads, n_kv):
    group = n_heads // n_kv
    for h in range(n_heads):
        t = q_tiles[h // 2]
        src_half = h % 2
        dst_half = (h // group) % 2
        if src_half != dst_half:
            t = pltpu.roll(t, HEAD_DIM, 1)
        lane = _lane_iota(t.shape)
        keep = (lane >= HEAD_DIM) if dst_half == 1 else (lane < HEAD_DIM)
        out_ref[:, h * LANES:(h + 1) * LANES] = jnp.where(keep, t, 0.0).astype(BF16)


def _mem_kv_kernel(mem_ref, g_ref, w_ref, o_ref):
    h = _rms_rows(mem_ref[...], g_ref[...]).astype(BF16)
    o_ref[...] = jnp.dot(h, w_ref[...], preferred_element_type=F32).astype(BF16)


def _mem_kv(mem2d, gain, w_bf16, tm):
    depth, d, n = w_bf16.shape
    rows = mem2d.shape[0]
    return pl.pallas_call(
        _mem_kv_kernel,
        out_shape=jax.ShapeDtypeStruct((depth, rows, n), BF16),
        grid=(depth, rows // tm),
        in_specs=[pl.BlockSpec((tm, d), lambda l, i: (i, 0)),
                  pl.BlockSpec((1, d), lambda l, i: (0, 0)),
                  pl.BlockSpec((None, d, n), lambda l, i: (l, 0, 0))],
        out_specs=pl.BlockSpec((None, tm, n), lambda l, i: (l, i, 0)),
        compiler_params=_params("arbitrary", "arbitrary"),
        name="mem_kv_proj",
    )(mem2d, gain, w_bf16)


def _in_even_kernel(x_ref, g_ref, w_ref, gq_ref, gk_ref, cos_ref, sin_ref,
                    qa_ref, ka_ref, va_ref, qb_ref, kb_ref, vb_ref, qm_ref,
                    ga_ref, gb_ref, gm_ref):
    h = _rms_rows(x_ref[...], g_ref[...]).astype(BF16)

    def seg(lo, hi):
        return jnp.dot(h, w_ref[:, lo:hi], preferred_element_type=F32)

    ones_bd = _half_ones()
    cos = cos_ref[...]
    sin = sin_ref[...]
    scale = HEAD_DIM ** -0.5
    qa_w = A_HEADS * HEAD_DIM
    kva_w = A_KV * HEAD_DIM
    qb_w = B_HEADS * HEAD_DIM
    kvb_w = B_KV * HEAD_DIM
    o = 0
    zq = seg(o, o + qa_w)
    tiles = [_head_norm_rope(zq[:, j * LANES:(j + 1) * LANES], gq_ref[...], cos, sin, ones_bd)
             * (scale * LOG2E) for j in range(qa_w // LANES)]
    _store_padded_heads(tiles, qa_ref, A_HEADS, A_KV)
    o += qa_w
    ka_ref[...] = _head_norm_rope(seg(o, o + kva_w), gk_ref[...], cos, sin, ones_bd).astype(BF16)
    o += kva_w
    va_ref[...] = seg(o, o + kva_w).T.astype(BF16)
    o += kva_w
    zq = seg(o, o + qb_w) * (scale * LOG2E)
    _store_padded_heads([zq[:, j * LANES:(j + 1) * LANES] for j in range(qb_w // LANES)],
                        qb_ref, B_HEADS, B_KV)
    o += qb_w
    kb_ref[...] = seg(o, o + kvb_w).astype(BF16)
    o += kvb_w
    vb_ref[...] = seg(o, o + kvb_w).astype(BF16)
    o += kvb_w
    qm_ref[...] = (seg(o, o + MEM_W) * (M_HEAD_DIM ** -0.5)).astype(BF16)
    o += MEM_W
    ga_ref[...] = _silu(seg(o, o + qa_w)).astype(BF16)
    o += qa_w
    gb_ref[...] = _silu(seg(o, o + qb_w)).astype(BF16)
    o += qb_w
    gm_ref[...] = _silu(seg(o, o + MEM_W)).astype(BF16)


def _in_even(x2d, gain, w_bf16, gq, gk, cos, sin, seq, tm):
    n_tok, d = x2d.shape
    n_in = w_bf16.shape[1]
    per_seq = seq // tm
    widths = [A_HEADS * LANES, A_KV * HEAD_DIM, A_KV * HEAD_DIM,
              B_HEADS * LANES, B_KV * HEAD_DIM, B_KV * HEAD_DIM, MEM_W,
              A_HEADS * HEAD_DIM, B_HEADS * HEAD_DIM, MEM_W]
    row = lambda i: (i, 0)
    const = lambda i: (0, 0)
    out_shape = [jax.ShapeDtypeStruct((n_tok, w), BF16) for w in widths]
    out_specs = [pl.BlockSpec((tm, w), row) for w in widths]
    out_shape[2] = jax.ShapeDtypeStruct((n_tok // tm, A_KV * HEAD_DIM, tm), BF16)
    out_specs[2] = pl.BlockSpec((None, A_KV * HEAD_DIM, tm), lambda i: (i, 0, 0))
    return pl.pallas_call(
        _in_even_kernel,
        out_shape=out_shape,
        grid=(n_tok // tm,),
        in_specs=[pl.BlockSpec((tm, d), row),
                  pl.BlockSpec((1, d), const),
                  pl.BlockSpec((d, n_in), const),
                  pl.BlockSpec((1, LANES), const),
                  pl.BlockSpec((1, LANES), const),
                  pl.BlockSpec((tm, LANES), lambda i: (i % per_seq, 0)),
                  pl.BlockSpec((tm, LANES), lambda i: (i % per_seq, 0))],
        out_specs=out_specs,
        compiler_params=_params("arbitrary"),
        name="in_proj_even",
    )(x2d, gain, w_bf16, gq, gk, cos, sin)


def _in_odd_kernel(x_ref, g_ref, w_ref, qc_ref, kc_ref, vc_ref, qm_ref, gc_ref, gm_ref):
    h = _rms_rows(x_ref[...], g_ref[...]).astype(BF16)

    def seg(lo, hi):
        return jnp.dot(h, w_ref[:, lo:hi], preferred_element_type=F32)

    cw = C_HEADS * HEAD_DIM
    o = 0
    qc_ref[...] = (seg(o, o + cw) * (HEAD_DIM ** -0.5 * LOG2E)).astype(BF16)
    o += cw
    kc_ref[...] = seg(o, o + cw).astype(BF16)
    o += cw
    vc_ref[...] = seg(o, o + cw).astype(BF16)
    o += cw
    qm_ref[...] = (seg(o, o + MEM_W) * (M_HEAD_DIM ** -0.5)).astype(BF16)
    o += MEM_W
    gc_ref[...] = _silu(seg(o, o + cw)).astype(BF16)
    o += cw
    gm_ref[...] = _silu(seg(o, o + MEM_W)).astype(BF16)


def _in_odd(x2d, gain, w_bf16, tm):
    n_tok, d = x2d.shape
    n_in = w_bf16.shape[1]
    cw = C_HEADS * HEAD_DIM
    widths = [cw, cw, cw, MEM_W, cw, MEM_W]
    row = lambda i: (i, 0)
    const = lambda i: (0, 0)
    return pl.pallas_call(
        _in_odd_kernel,
        out_shape=[jax.ShapeDtypeStruct((n_tok, w), BF16) for w in widths],
        grid=(n_tok // tm,),
        in_specs=[pl.BlockSpec((tm, d), row),
                  pl.BlockSpec((1, d), const),
                  pl.BlockSpec((d, n_in), const)],
        out_specs=[pl.BlockSpec((tm, w), row) for w in widths],
        compiler_params=_params("arbitrary"),
        name="in_proj_odd",
    )(x2d, gain, w_bf16)


def _unpad_heads_store(o_rows, gate_ref, out_ref, n_heads, n_kv, tq, row0=0):
    group = n_heads // n_kv
    rows = slice(row0, row0 + tq)
    for j in range(n_heads // 2):
        parts = []
        for h in (2 * j, 2 * j + 1):
            g = h // group
            i = h % group
            t = o_rows[g][i * tq:(i + 1) * tq, :]
            if (g % 2) != (h % 2):
                t = pltpu.roll(t, HEAD_DIM, 1)
            parts.append(t)
        lane = _lane_iota(parts[0].shape)
        tile = jnp.where(lane < HEAD_DIM, parts[0], parts[1])
        gate = gate_ref[rows, j * LANES:(j + 1) * LANES].astype(F32)
        out_ref[rows, j * LANES:(j + 1) * LANES] = (tile * gate).astype(BF16)


def _global_attn_kernel(q_ref, k_ref, vt_ref, gate_ref, o_ref, m_scr, l_scr, acc_scr, s_scr, mc_scr):
    tq = q_ref.shape[0]
    n_chunks = vt_ref.shape[0]
    tk = vt_ref.shape[2]
    group = A_HEADS // A_KV
    qs = [jnp.concatenate([q_ref[:, (g * group + i) * LANES:(g * group + i + 1) * LANES]
                           for i in range(group)], axis=0) for g in range(A_KV)]
    m_scr[...] = jnp.full(m_scr.shape, NEG, F32)
    l_scr[...] = jnp.zeros(l_scr.shape, F32)
    acc_scr[...] = jnp.zeros(acc_scr.shape, F32)

    def scores(c, slot):
        start = pl.multiple_of(c * tk, tk)
        k = k_ref[pl.ds(start, tk), :]
        for g in range(A_KV):
            st = lax.dot_general(k, qs[g], _NT, preferred_element_type=F32)
            s_scr[slot, g] = st
            mc_scr[slot, g] = jnp.max(st, axis=0, keepdims=True)

    def accumulate(c, slot):
        vt = vt_ref[c]
        for g in range(A_KV):
            m_old = m_scr[g]
            m_new = jnp.maximum(m_old, mc_scr[slot, g])
            alpha = jnp.exp2(m_old - m_new)
            pt = jnp.exp2(s_scr[slot, g] - m_new)
            l_scr[g] = alpha * l_scr[g] + jnp.sum(pt, axis=0, keepdims=True)
            lo, hi = g * HEAD_DIM, (g + 1) * HEAD_DIM
            acc_scr[g, lo:hi] = alpha * acc_scr[g, lo:hi] + jnp.dot(
                vt[lo:hi], pt.astype(BF16), preferred_element_type=F32)
            m_scr[g] = m_new

    scores(0, 0)

    def body(c2, carry):
        c = 2 * c2
        scores(c + 1, 1)
        accumulate(c, 0)
        scores(jnp.minimum(c + 2, n_chunks - 1), 0)
        accumulate(c + 1, 1)
        return carry

    lax.fori_loop(0, n_chunks // 2, body, 0)
    o_rows = [(acc_scr[g] * (1.0 / l_scr[g])).T for g in range(A_KV)]
    _unpad_heads_store(o_rows, gate_ref, o_ref, A_HEADS, A_KV, tq)


def _global_attn(qa, ka, vat, gate, tq):
    b, s, _ = qa.shape
    tk = vat.shape[-1]
    n_chunks = s // tk
    rows = (A_HEADS // A_KV) * tq
    vat = vat.reshape(b, n_chunks, A_KV * HEAD_DIM, tk)
    return pl.pallas_call(
        _global_attn_kernel,
        out_shape=jax.ShapeDtypeStruct((b, s, A_HEADS * HEAD_DIM), BF16),
        grid=(b, s // tq),
        in_specs=[pl.BlockSpec((None, tq, A_HEADS * LANES), lambda bi, i: (bi, i, 0)),
                  pl.BlockSpec((None, s, LANES), lambda bi, i: (bi, 0, 0)),
                  pl.BlockSpec((None, n_chunks, A_KV * HEAD_DIM, tk), lambda bi, i: (bi, 0, 0, 0)),
                  pl.BlockSpec((None, tq, A_HEADS * HEAD_DIM), lambda bi, i: (bi, i, 0))],
        out_specs=pl.BlockSpec((None, tq, A_HEADS * HEAD_DIM), lambda bi, i: (bi, i, 0)),
        scratch_shapes=[pltpu.VMEM((A_KV, 1, rows), F32), pltpu.VMEM((A_KV, 1, rows), F32),
                        pltpu.VMEM((A_KV, LANES, rows), F32),
                        pltpu.VMEM((2, A_KV, tk, rows), F32), pltpu.VMEM((2, A_KV, 1, rows), F32)],
        compiler_params=_params("arbitrary", "arbitrary"),
        name="global_attn",
    )(qa, ka, vat, gate)


def _window_attn_kernel(q_ref, k_ref, v_ref, bias_ref, sink_ref, gate_ref, o_ref, *, blk):
    blocks_per_step = q_ref.shape[0] // blk
    seq = k_ref.shape[0]
    nb = seq // blk
    step = pl.program_id(1)
    group = B_HEADS // B_KV
    ones = jnp.ones((3 * blk, LANES), BF16)
    col = _lane_iota((1, 3 * blk))

    def window(ref, t):
        i = step * blocks_per_step + t
        left = pl.multiple_of(jnp.maximum(i - 1, 0) * blk, blk)
        mid = pl.multiple_of(i * blk, blk)
        right = pl.multiple_of(jnp.minimum(i + 1, nb - 1) * blk, blk)
        return jnp.concatenate([ref[pl.ds(left, blk), :], ref[pl.ds(mid, blk), :],
                                ref[pl.ds(right, blk), :]], axis=0)

    def scores(t, g):
        i = step * blocks_per_step + t
        rows = slice(t * blk, (t + 1) * blk)
        q = jnp.concatenate([q_ref[rows, (g * group + j) * LANES:(g * group + j + 1) * LANES]
                             for j in range(group)], axis=0)
        s = lax.dot_general(q, window(k_ref, t), _NT, preferred_element_type=F32) + bias_ref[g]
        if t == 0:
            s = s + jnp.where(col < blk, jnp.where(i == 0, NEG, 0.0), 0.0)
        if t == blocks_per_step - 1:
            s = s + jnp.where(col >= 2 * blk, jnp.where(i == nb - 1, NEG, 0.0), 0.0)
        m = jnp.maximum(jnp.max(s, axis=-1, keepdims=True), sink_ref[g])
        return s, m

    def probs(s, m):
        return jnp.exp2(s - m).astype(BF16), m

    def output(t, g, p, m):
        v = jnp.concatenate([window(v_ref, t), ones], axis=1)
        o = jnp.dot(p, v, preferred_element_type=F32)
        denom = o[:, LANES:] + jnp.exp2(sink_ref[g] - m)
        return o[:, :LANES] * (1.0 / denom)

    items = [(t, g) for t in range(blocks_per_step) for g in range(B_KV)]
    sm, pb, outs = {}, {}, {}
    for n in range(len(items) + 2):
        if n < len(items):
            sm[n] = scores(*items[n])
        if 1 <= n <= len(items):
            pb[n - 1] = probs(*sm.pop(n - 1))
        if n >= 2:
            t, g = items[n - 2]
            outs[g] = output(t, g, *pb.pop(n - 2))
            if g == B_KV - 1:
                _unpad_heads_store([outs[x] for x in range(B_KV)], gate_ref, o_ref,
                                   B_HEADS, B_KV, blk, row0=t * blk)


def _window_attn(qb, kb, vb, bias, sink_rows, gate, blk, blocks_per_step):
    b, s, _ = qb.shape
    rows = (B_HEADS // B_KV) * blk
    tq = blk * blocks_per_step
    return pl.pallas_call(
        functools.partial(_window_attn_kernel, blk=blk),
        out_shape=jax.ShapeDtypeStruct((b, s, B_HEADS * HEAD_DIM), BF16),
        grid=(b, s // tq),
        in_specs=[pl.BlockSpec((None, tq, B_HEADS * LANES), lambda bi, i: (bi, i, 0)),
                  pl.BlockSpec((None, s, LANES), lambda bi, i: (bi, 0, 0)),
                  pl.BlockSpec((None, s, LANES), lambda bi, i: (bi, 0, 0)),
                  pl.BlockSpec((B_KV, rows, 3 * blk), lambda bi, i: (0, 0, 0)),
                  pl.BlockSpec((B_KV, rows, 1), lambda bi, i: (0, 0, 0)),
                  pl.BlockSpec((None, tq, B_HEADS * HEAD_DIM), lambda bi, i: (bi, i, 0))],
        out_specs=pl.BlockSpec((None, tq, B_HEADS * HEAD_DIM), lambda bi, i: (bi, i, 0)),
        compiler_params=_params("arbitrary", "arbitrary"),
        name="window_attn",
    )(qb, kb, vb, bias, sink_rows, gate)


def _mem_attn_kernel(q_ref, kv_ref, gate_ref, o_ref):
    for h in range(M_HEADS):
        lo, hi = h * M_HEAD_DIM, (h + 1) * M_HEAD_DIM
        q = q_ref[:, lo:hi]
        k = kv_ref[:, lo:hi]
        v = kv_ref[:, MEM_W + lo:MEM_W + hi]
        s = lax.dot_general(q, k, _NT, preferred_element_type=F32)
        m = jnp.max(s, axis=-1, keepdims=True)
        p = jnp.exp(s - m)
        l = jnp.sum(p, axis=-1, keepdims=True)
        o = jnp.dot(p.astype(BF16), v, preferred_element_type=F32) * (1.0 / l)
        o_ref[:, lo:hi] = (o * gate_ref[:, lo:hi].astype(F32)).astype(BF16)


def _mem_attn(qm, mkv, layer, gate, tq):
    b, s, _ = qm.shape
    mlen = mkv.shape[2]
    return pl.pallas_call(
        _mem_attn_kernel,
        out_shape=jax.ShapeDtypeStruct((b, s, MEM_W), BF16),
        grid=(b, s // tq),
        in_specs=[pl.BlockSpec((None, tq, MEM_W), lambda bi, i: (bi, i, 0)),
                  pl.BlockSpec((None, None, mlen, 2 * MEM_W), lambda bi, i: (layer, bi, 0, 0)),
                  pl.BlockSpec((None, tq, MEM_W), lambda bi, i: (bi, i, 0))],
        out_specs=pl.BlockSpec((None, tq, MEM_W), lambda bi, i: (bi, i, 0)),
        compiler_params=_params("arbitrary", "arbitrary"),
        name="mem_attn",
    )(qm, mkv, gate)


def _nbr_attn_kernel(q_ref, k_ref, v_ref, bias_ref, gate_ref, o_ref, *, grid_rows, rows_per_step):
    rb = pl.program_id(2)
    nkeys = NA_ROWS * GRID_W
    n_pairs = q_ref.shape[1] // LANES
    ones = jnp.ones((nkeys, LANES), BF16)
    starts, deltas = [], []
    for t in range(rows_per_step):
        r = rb * rows_per_step + t
        rs = jnp.clip(r - NA_ROWS // 2, 0, grid_rows - NA_ROWS)
        starts.append(pl.multiple_of(rs * GRID_W, GRID_W))
        deltas.append(r - rs)
    items = [(t, j) for t in range(rows_per_step) for j in range(n_pairs)]

    def scores(t, j):
        lo, hi = j * LANES, (j + 1) * LANES
        qt = q_ref[t * GRID_W:(t + 1) * GRID_W, lo:hi]
        lane = _lane_iota(qt.shape)
        zero = jnp.zeros_like(qt)
        q = jnp.concatenate([jnp.where(lane < HEAD_DIM, qt, zero),
                             jnp.where(lane >= HEAD_DIM, qt, zero)], axis=0)
        k = k_ref[pl.ds(starts[t], nkeys), lo:hi]
        s = lax.dot_general(q, k, _NT, preferred_element_type=F32) + bias_ref[deltas[t], j]
        return s, jnp.max(s, axis=-1, keepdims=True)

    def probs(s, m):
        return jnp.exp2(s - m).astype(BF16)

    def output(t, j, p):
        lo, hi = j * LANES, (j + 1) * LANES
        v = jnp.concatenate([v_ref[pl.ds(starts[t], nkeys), lo:hi], ones], axis=1)
        o = jnp.dot(p, v, preferred_element_type=F32)
        o = o[:, :LANES] * (1.0 / o[:, LANES:])
        lane_o = _lane_iota((GRID_W, LANES))
        tile = jnp.where(lane_o < HEAD_DIM, o[:GRID_W, :], o[GRID_W:, :])
        rows = slice(t * GRID_W, (t + 1) * GRID_W)
        o_ref[rows, lo:hi] = (tile * gate_ref[rows, lo:hi].astype(F32)).astype(BF16)

    sm = {}
    pb = {}
    for n in range(len(items) + 2):
        if n < len(items):
            sm[n] = scores(*items[n])
        if 1 <= n <= len(items):
            pb[n - 1] = probs(*sm.pop(n - 1))
        if n >= 2:
            output(*items[n - 2], pb.pop(n - 2))


def _nbr_attn(qc, kc, vc, bias, gate, rows_per_step, head_splits):
    b, s, w = qc.shape
    grid_rows = s // GRID_W
    nkeys = NA_ROWS * GRID_W
    wh = w // head_splits
    tq = rows_per_step * GRID_W
    blk = lambda hh, bi, rb: (bi, rb, hh)
    return pl.pallas_call(
        functools.partial(_nbr_attn_kernel, grid_rows=grid_rows, rows_per_step=rows_per_step),
        out_shape=jax.ShapeDtypeStruct((b, s, w), BF16),
        grid=(head_splits, b, grid_rows // rows_per_step),
        in_specs=[pl.BlockSpec((None, tq, wh), blk),
                  pl.BlockSpec((None, s, wh), lambda hh, bi, rb: (bi, 0, hh)),
                  pl.BlockSpec((None, s, wh), lambda hh, bi, rb: (bi, 0, hh)),
                  pl.BlockSpec((NA_ROWS, wh // LANES, 2 * GRID_W, nkeys), lambda hh, bi, rb: (0, hh, 0, 0)),
                  pl.BlockSpec((None, tq, wh), blk)],
        out_specs=pl.BlockSpec((None, tq, wh), blk),
        compiler_params=_params("arbitrary", "arbitrary", "arbitrary"),
        name="nbr_attn",
    )(qc, kc, vc, bias, gate)


def _out_proj_kernel(*refs, n_parts, final_norm):
    y_refs = refs[:n_parts]
    x_ref, w_ref = refs[n_parts], refs[n_parts + 1]
    if final_norm:
        g_ref, o_ref = refs[n_parts + 2], refs[n_parts + 3]
    else:
        o_ref = refs[n_parts + 2]
    acc = x_ref[...]
    off = 0
    for y_ref in y_refs:
        width = y_ref.shape[1]
        acc = acc + jnp.dot(y_ref[...], w_ref[off:off + width, :], preferred_element_type=F32)
        off += width
    if final_norm:
        acc = _rms_rows(acc, g_ref[...])
    o_ref[...] = acc


def _out_proj(ys, x2d, w_bf16, tm, final_gain=None):
    n_tok, d = x2d.shape
    row = lambda i: (i, 0)
    const = lambda i: (0, 0)
    in_specs = [pl.BlockSpec((tm, y.shape[1]), row) for y in ys]
    in_specs += [pl.BlockSpec((tm, d), row), pl.BlockSpec(w_bf16.shape, const)]
    args = list(ys) + [x2d, w_bf16]
    if final_gain is not None:
        in_specs.append(pl.BlockSpec((1, d), const))
        args.append(final_gain)
    return pl.pallas_call(
        functools.partial(_out_proj_kernel, n_parts=len(ys), final_norm=final_gain is not None),
        out_shape=jax.ShapeDtypeStruct((n_tok, d), F32),
        grid=(n_tok // tm,),
        in_specs=in_specs,
        out_specs=pl.BlockSpec((tm, d), row),
        compiler_params=_params("arbitrary"),
        name="out_proj_final" if final_gain is not None else "out_proj",
    )(*args)


def _rope_tables(seq):
    quarter = HEAD_DIM // 4
    freqs = jnp.power(ROPE_THETA, -jnp.arange(quarter, dtype=F32) / quarter)
    t = jnp.arange(seq)
    ang_r = (t // GRID_W).astype(F32)[:, None] * freqs
    ang_c = (t % GRID_W).astype(F32)[:, None] * freqs
    cos_h = jnp.concatenate([jnp.cos(ang_r), jnp.cos(ang_r), jnp.cos(ang_c), jnp.cos(ang_c)], axis=-1)
    sin_h = jnp.concatenate([-jnp.sin(ang_r), jnp.sin(ang_r), -jnp.sin(ang_c), jnp.sin(ang_c)], axis=-1)
    return jnp.tile(cos_h, (1, 2)), jnp.tile(sin_h, (1, 2))


def _t5_bucket(rel):
    nb = REL_BUCKETS // 2
    max_exact = nb // 2
    ret = jnp.where(rel > 0, nb, 0)
    n = jnp.abs(rel)
    nf = jnp.maximum(n, 1).astype(F32)
    large = max_exact + (jnp.log(nf / max_exact) / math.log(REL_MAX_DIST / max_exact)
                         * (nb - max_exact)).astype(jnp.int32)
    large = jnp.minimum(large, nb - 1)
    return ret + jnp.where(n < max_exact, n, large)


def _window_bias(rel_bias, blk):
    span = blk + 2 * WINDOW
    rel = (jnp.arange(span) - WINDOW)[None, :] - jnp.arange(blk)[:, None]
    band = jnp.abs(rel) <= WINDOW
    onehot = (_t5_bucket(rel)[None] == jnp.arange(REL_BUCKETS)[:, None, None]).astype(F32)
    bias = jnp.einsum('bh,bqs->hqs', rel_bias.astype(F32) * LOG2E, onehot,
                      precision=lax.Precision.HIGHEST)
    bias = jnp.where(band[None], bias, NEG)
    group = B_HEADS // B_KV
    return bias.reshape(B_KV, group * blk, span)


def _nbr_bias(rpb):
    col = np.arange(GRID_W)
    cs = np.clip(col - NA_COLS // 2, 0, GRID_W - NA_COLS)
    colmask = (col[None, :] >= cs[:, None]) & (col[None, :] < cs[:, None] + NA_COLS)
    dc = np.clip(col[None, :] - col[:, None] + NA_COLS - 1, 0, 2 * NA_COLS - 2)
    onehot = (dc[None] == np.arange(2 * NA_COLS - 1)[:, None, None]) & colmask[None]
    t = jnp.einsum('hrc,cqk->hqrk', rpb.astype(F32) * LOG2E, jnp.asarray(onehot, F32),
                   precision=lax.Precision.HIGHEST)
    t = t + jnp.asarray(np.where(colmask, 0.0, NEG), F32)[None, :, None, :]
    per_delta = [t[:, :, NA_ROWS - 1 - dl:2 * NA_ROWS - 1 - dl, :] for dl in range(NA_ROWS)]
    return jnp.stack(per_delta).reshape(NA_ROWS, C_HEADS // 2, 2 * GRID_W, NA_ROWS * GRID_W)


def kernel(x, mem, norm_gain, mem_norm_gain, w_in_even, w_out_even, q_norm_a, k_norm_a, sink_b,
           rel_bias, w_in_odd, w_out_odd, rpb_c, w_mem_kv, final_norm_gain):
    b, s, d = x.shape
    mlen = mem.shape[1]
    assert s % GRID_W == 0 and s // GRID_W >= NA_ROWS and s % 512 == 0
    tm = 512
    blk = 128

    x2d = x.reshape(b * s, d)
    mem_tm = math.gcd(b * mlen, 512)
    mkv = _mem_kv(mem.reshape(b * mlen, d), mem_norm_gain.reshape(1, d), w_mem_kv.astype(BF16), mem_tm)
    mkv = mkv.reshape(w_mem_kv.shape[0], b, mlen, 2 * MEM_W)

    cos, sin = _rope_tables(s)
    gq = jnp.tile(q_norm_a[0].astype(F32), 2).reshape(1, LANES)
    gk = jnp.tile(k_norm_a[0].astype(F32), 2).reshape(1, LANES)
    qa, ka, va, qb, kb, vb, qm, ga, gb, gm = _in_even(
        x2d, norm_gain[0].reshape(1, d), w_in_even[0].astype(BF16), gq, gk, cos, sin, s, tm)
    r3 = lambda a: a.reshape(b, s, a.shape[-1])
    ya = _global_attn(r3(qa), r3(ka), va, r3(ga), tq=256)
    group_b = B_HEADS // B_KV
    sink_rows = jnp.repeat(sink_b[0].astype(F32).reshape(B_KV, group_b) * LOG2E, blk,
                           axis=1).reshape(B_KV, group_b * blk, 1)
    yb = _window_attn(r3(qb), r3(kb), r3(vb), _window_bias(rel_bias, blk), sink_rows, r3(gb), blk,
                      blocks_per_step=4)
    ym = _mem_attn(r3(qm), mkv, 0, r3(gm), tq=512)
    x1 = _out_proj([ya.reshape(b * s, -1), yb.reshape(b * s, -1), ym.reshape(b * s, -1)],
                   x2d, w_out_even[0].astype(BF16), tm)

    qc, kc, vc, qm1, gc, gm1 = _in_odd(x1, norm_gain[1].reshape(1, d), w_in_odd[0].astype(BF16), tm)
    yc = _nbr_attn(r3(qc), r3(kc), r3(vc), _nbr_bias(rpb_c[0]), r3(gc), rows_per_step=4, head_splits=2)
    ym1 = _mem_attn(r3(qm1), mkv, 1, r3(gm1), tq=512)
    out = _out_proj([yc.reshape(b * s, -1), ym1.reshape(b * s, -1)], x1, w_out_odd[0].astype(BF16), tm,
                    final_gain=final_norm_gain.reshape(1, d))
    return out.reshape(b, s, d)
```

```python
import functools
import math

import jax
import jax.numpy as jnp
import numpy as np
from jax import lax
from jax.experimental import pallas as pl
from jax.experimental.pallas import tpu as pltpu

GRID_W = 64
HEAD_DIM = 64
A_HEADS = 8
A_KV = 2
B_HEADS = 8
B_KV = 2
WINDOW = 128
C_HEADS = 16
NA_ROWS = 8
NA_COLS = 16
M_HEADS = 4
M_HEAD_DIM = 128
MEM_W = M_HEADS * M_HEAD_DIM
REL_BUCKETS = 32
REL_MAX_DIST = 128
ROPE_THETA = 10000.0
EPS = 1e-6

LANES = 128
NEG = -1e30
LOG2E = math.log2(math.e)
V_ONES_ROWS = 16
VMEM_LIMIT_BYTES = 56 * 1024 * 1024

F32 = jnp.float32
BF16 = jnp.bfloat16

_NT = (((1,), (1,)), ((), ()))


def _params(*sem):
    return pltpu.CompilerParams(dimension_semantics=sem, vmem_limit_bytes=VMEM_LIMIT_BYTES)


def _rms_rows(x, gain):
    ms = jnp.mean(x * x, axis=-1, keepdims=True)
    return x * lax.rsqrt(ms + EPS) * gain


def _lane_iota(shape):
    return lax.broadcasted_iota(jnp.int32, shape, len(shape) - 1)


def _silu(x):
    return x * (1.0 / (1.0 + jnp.exp(-x)))


def _half_ones():
    r = lax.broadcasted_iota(jnp.int32, (LANES, LANES), 0) // HEAD_DIM
    c = lax.broadcasted_iota(jnp.int32, (LANES, LANES), 1) // HEAD_DIM
    return jnp.where(r == c, 1.0, 0.0).astype(BF16)


def _head_norm_rope(t, gain, cos, sin_signed, ones_bd):
    t2 = t * t
    hi = t2.astype(BF16)
    lo = (t2 - hi.astype(F32)).astype(BF16)
    ss = (jnp.dot(hi, ones_bd, preferred_element_type=F32)
          + jnp.dot(lo, ones_bd, preferred_element_type=F32))
    tn = t * lax.rsqrt(ss * (1.0 / HEAD_DIM) + EPS) * gain
    lane = _lane_iota(tn.shape)
    quarter = HEAD_DIM // 4
    partner = jnp.where((lane % (2 * quarter)) < quarter,
                        pltpu.roll(tn, LANES - quarter, 1), pltpu.roll(tn, quarter, 1))
    return tn * cos + partner * sin_signed


def _store_padded_heads(q_tiles, out_ref, n_heads, n_kv):
    group = n_heads // n_kv
    for h in range(n_heads):
        t = q_tiles[h // 2]
        src_half = h % 2
        dst_half = (h // group) % 2
        if src_half != dst_half:
            t = pltpu.roll(t, HEAD_DIM, 1)
        lane = _lane_iota(t.shape)
        keep = (lane >= HEAD_DIM) if dst_half == 1 else (lane < HEAD_DIM)
        out_ref[:, h * LANES:(h + 1) * LANES] = jnp.where(keep, t, 0.0).astype(BF16)


def _mem_kv_kernel(mem_ref, g_ref, w_ref, o_ref):
    h = _rms_rows(mem_ref[...], g_ref[...]).astype(BF16)
    o_ref[...] = jnp.dot(h, w_ref[...], preferred_element_type=F32).astype(BF16)


def _mem_kv(mem2d, gain, w_bf16, tm):
    depth, d, n = w_bf16.shape
    rows = mem2d.shape[0]
    return pl.pallas_call(
        _mem_kv_kernel,
        out_shape=jax.ShapeDtypeStruct((depth, rows, n), BF16),
        grid=(depth, rows // tm),
        in_specs=[pl.BlockSpec((tm, d), lambda l, i: (i, 0)),
                  pl.BlockSpec((1, d), lambda l, i: (0, 0)),
                  pl.BlockSpec((None, d, n), lambda l, i: (l, 0, 0))],
        out_specs=pl.BlockSpec((None, tm, n), lambda l, i: (l, i, 0)),
        compiler_params=_params("arbitrary", "arbitrary"),
        name="mem_kv_proj",
    )(mem2d, gain, w_bf16)


def _in_even_kernel(x_ref, g_ref, w_ref, gq_ref, gk_ref, cos_ref, sin_ref,
                    qa_ref, ka_ref, va_ref, qb_ref, kb_ref, vb_ref, qm_ref,
                    ga_ref, gb_ref, gm_ref):
    h = _rms_rows(x_ref[...], g_ref[...]).astype(BF16)

    def seg(lo, hi):
        return jnp.dot(h, w_ref[:, lo:hi], preferred_element_type=F32)

    ones_bd = _half_ones()
    cos = cos_ref[...]
    sin = sin_ref[...]
    scale = HEAD_DIM ** -0.5
    qa_w = A_HEADS * HEAD_DIM
    kva_w = A_KV * HEAD_DIM
    qb_w = B_HEADS * HEAD_DIM
    kvb_w = B_KV * HEAD_DIM
    o = 0
    zq = seg(o, o + qa_w)
    tiles = [_head_norm_rope(zq[:, j * LANES:(j + 1) * LANES], gq_ref[...], cos, sin, ones_bd)
             * (scale * LOG2E) for j in range(qa_w // LANES)]
    _store_padded_heads(tiles, qa_ref, A_HEADS, A_KV)
    o += qa_w
    ka_ref[...] = _head_norm_rope(seg(o, o + kva_w), gk_ref[...], cos, sin, ones_bd).astype(BF16)
    o += kva_w
    vt = seg(o, o + kva_w).T.astype(BF16)
    for g in range(A_KV):
        va_ref[g, :HEAD_DIM] = vt[g * HEAD_DIM:(g + 1) * HEAD_DIM]
        va_ref[g, HEAD_DIM:] = jnp.ones((V_ONES_ROWS, vt.shape[1]), BF16)
    o += kva_w
    zq = seg(o, o + qb_w) * (scale * LOG2E)
    _store_padded_heads([zq[:, j * LANES:(j + 1) * LANES] for j in range(qb_w // LANES)],
                        qb_ref, B_HEADS, B_KV)
    o += qb_w
    kb_ref[...] = seg(o, o + kvb_w).astype(BF16)
    o += kvb_w
    vb_ref[...] = seg(o, o + kvb_w).astype(BF16)
    o += kvb_w
    qm_ref[...] = (seg(o, o + MEM_W) * (M_HEAD_DIM ** -0.5 * LOG2E)).astype(BF16)
    o += MEM_W
    ga_ref[...] = _silu(seg(o, o + qa_w)).astype(BF16)
    o += qa_w
    gb_ref[...] = _silu(seg(o, o + qb_w)).astype(BF16)
    o += qb_w
    gm_ref[...] = _silu(seg(o, o + MEM_W)).astype(BF16)


def _in_even(x2d, gain, w_bf16, gq, gk, cos, sin, seq, tm):
    n_tok, d = x2d.shape
    n_in = w_bf16.shape[1]
    per_seq = seq // tm
    widths = [A_HEADS * LANES, A_KV * HEAD_DIM, A_KV * HEAD_DIM,
              B_HEADS * LANES, B_KV * HEAD_DIM, B_KV * HEAD_DIM, MEM_W,
              A_HEADS * HEAD_DIM, B_HEADS * HEAD_DIM, MEM_W]
    row = lambda i: (i, 0)
    const = lambda i: (0, 0)
    out_shape = [jax.ShapeDtypeStruct((n_tok, w), BF16) for w in widths]
    out_specs = [pl.BlockSpec((tm, w), row) for w in widths]
    vrows = HEAD_DIM + V_ONES_ROWS
    out_shape[2] = jax.ShapeDtypeStruct((n_tok // tm, A_KV, vrows, tm), BF16)
    out_specs[2] = pl.BlockSpec((None, A_KV, vrows, tm), lambda i: (i, 0, 0, 0))
    return pl.pallas_call(
        _in_even_kernel,
        out_shape=out_shape,
        grid=(n_tok // tm,),
        in_specs=[pl.BlockSpec((tm, d), row),
                  pl.BlockSpec((1, d), const),
                  pl.BlockSpec((d, n_in), const),
                  pl.BlockSpec((1, LANES), const),
                  pl.BlockSpec((1, LANES), const),
                  pl.BlockSpec((tm, LANES), lambda i: (i % per_seq, 0)),
                  pl.BlockSpec((tm, LANES), lambda i: (i % per_seq, 0))],
        out_specs=out_specs,
        compiler_params=_params("arbitrary"),
        name="in_proj_even",
    )(x2d, gain, w_bf16, gq, gk, cos, sin)


def _in_odd_kernel(x_ref, g_ref, w_ref, qc_ref, kc_ref, vc_ref, qm_ref, gc_ref, gm_ref):
    h = _rms_rows(x_ref[...], g_ref[...]).astype(BF16)

    def seg(lo, hi):
        return jnp.dot(h, w_ref[:, lo:hi], preferred_element_type=F32)

    cw = C_HEADS * HEAD_DIM
    o = 0
    qc_ref[...] = (seg(o, o + cw) * (HEAD_DIM ** -0.5 * LOG2E)).astype(BF16)
    o += cw
    kc_ref[...] = seg(o, o + cw).astype(BF16)
    o += cw
    vc_ref[...] = seg(o, o + cw).astype(BF16)
    o += cw
    qm_ref[...] = (seg(o, o + MEM_W) * (M_HEAD_DIM ** -0.5 * LOG2E)).astype(BF16)
    o += MEM_W
    gc_ref[...] = _silu(seg(o, o + cw)).astype(BF16)
    o += cw
    gm_ref[...] = _silu(seg(o, o + MEM_W)).astype(BF16)


def _in_odd(x2d, gain, w_bf16, tm):
    n_tok, d = x2d.shape
    n_in = w_bf16.shape[1]
    cw = C_HEADS * HEAD_DIM
    widths = [cw, cw, cw, MEM_W, cw, MEM_W]
    row = lambda i: (i, 0)
    const = lambda i: (0, 0)
    return pl.pallas_call(
        _in_odd_kernel,
        out_shape=[jax.ShapeDtypeStruct((n_tok, w), BF16) for w in widths],
        grid=(n_tok // tm,),
        in_specs=[pl.BlockSpec((tm, d), row),
                  pl.BlockSpec((1, d), const),
                  pl.BlockSpec((d, n_in), const)],
        out_specs=[pl.BlockSpec((tm, w), row) for w in widths],
        compiler_params=_params("arbitrary"),
        name="in_proj_odd",
    )(x2d, gain, w_bf16)


def _unpad_heads_store(o_rows, gate_ref, out_ref, n_heads, n_kv, tq, row0=0):
    group = n_heads // n_kv
    rows = slice(row0, row0 + tq)
    for j in range(n_heads // 2):
        parts = []
        for h in (2 * j, 2 * j + 1):
            g = h // group
            i = h % group
            t = o_rows[g][i * tq:(i + 1) * tq, :]
            if (g % 2) != (h % 2):
                t = pltpu.roll(t, HEAD_DIM, 1)
            parts.append(t)
        lane = _lane_iota(parts[0].shape)
        tile = jnp.where(lane < HEAD_DIM, parts[0], parts[1])
        gate = gate_ref[rows, j * LANES:(j + 1) * LANES].astype(F32)
        out_ref[rows, j * LANES:(j + 1) * LANES] = (tile * gate).astype(BF16)


def _global_attn_kernel(q_ref, k_ref, vt_ref, gate_ref, o_ref, m_scr, acc_scr, s_scr, mc_scr):
    tq = q_ref.shape[0]
    n_chunks = vt_ref.shape[0]
    tk = vt_ref.shape[3]
    group = A_HEADS // A_KV
    qs = [jnp.concatenate([q_ref[:, (g * group + i) * LANES:(g * group + i + 1) * LANES]
                           for i in range(group)], axis=0) for g in range(A_KV)]
    m_scr[...] = jnp.full(m_scr.shape, NEG, F32)
    acc_scr[...] = jnp.zeros(acc_scr.shape, F32)

    def scores(c, slot):
        start = pl.multiple_of(c * tk, tk)
        k = k_ref[pl.ds(start, tk), :]
        for g in range(A_KV):
            st = lax.dot_general(k, qs[g], _NT, preferred_element_type=F32)
            s_scr[slot, g] = st
            mc_scr[slot, g] = jnp.max(st, axis=0, keepdims=True)

    def accumulate(c, slot):
        for g in range(A_KV):
            m_old = m_scr[g]
            m_new = jnp.maximum(m_old, mc_scr[slot, g])
            alpha = jnp.exp2(m_old - m_new)
            pt = jnp.exp2(s_scr[slot, g] - m_new).astype(BF16)
            acc_scr[g] = alpha * acc_scr[g] + jnp.dot(vt_ref[c, g], pt, preferred_element_type=F32)
            m_scr[g] = m_new

    scores(0, 0)

    def body(c2, carry):
        c = 2 * c2
        scores(c + 1, 1)
        accumulate(c, 0)
        scores(c + 2, 0)
        accumulate(c + 1, 1)
        return carry

    lax.fori_loop(0, n_chunks // 2 - 1, body, 0)
    scores(n_chunks - 1, 1)
    accumulate(n_chunks - 2, 0)
    accumulate(n_chunks - 1, 1)

    ot = [acc_scr[g, :HEAD_DIM] * (1.0 / acc_scr[g, HEAD_DIM:HEAD_DIM + 1]) for g in range(A_KV)]
    for j in range(A_HEADS // 2):
        g, i0 = (2 * j) // group, (2 * j) % group
        tile_t = jnp.concatenate([ot[g][:, i0 * tq:(i0 + 1) * tq],
                                  ot[g][:, (i0 + 1) * tq:(i0 + 2) * tq]], axis=0)
        gate = gate_ref[:, j * LANES:(j + 1) * LANES].astype(F32)
        o_ref[:, j * LANES:(j + 1) * LANES] = (tile_t.T * gate).astype(BF16)


def _global_attn(qa, ka, vat, gate, tq):
    b, s, _ = qa.shape
    _, n_kv, vrows, tk = vat.shape
    n_chunks = s // tk
    assert n_chunks % 2 == 0
    rows = (A_HEADS // A_KV) * tq
    vat = vat.reshape(b, n_chunks, n_kv, vrows, tk)
    return pl.pallas_call(
        _global_attn_kernel,
        out_shape=jax.ShapeDtypeStruct((b, s, A_HEADS * HEAD_DIM), BF16),
        grid=(b, s // tq),
        in_specs=[pl.BlockSpec((None, tq, A_HEADS * LANES), lambda bi, i: (bi, i, 0)),
                  pl.BlockSpec((None, s, LANES), lambda bi, i: (bi, 0, 0)),
                  pl.BlockSpec((None, n_chunks, n_kv, vrows, tk), lambda bi, i: (bi, 0, 0, 0, 0)),
                  pl.BlockSpec((None, tq, A_HEADS * HEAD_DIM), lambda bi, i: (bi, i, 0))],
        out_specs=pl.BlockSpec((None, tq, A_HEADS * HEAD_DIM), lambda bi, i: (bi, i, 0)),
        scratch_shapes=[pltpu.VMEM((A_KV, 1, rows), F32),
                        pltpu.VMEM((A_KV, vrows, rows), F32),
                        pltpu.VMEM((2, A_KV, tk, rows), F32), pltpu.VMEM((2, A_KV, 1, rows), F32)],
        compiler_params=_params("arbitrary", "arbitrary"),
        name="global_attn",
    )(qa, ka, vat, gate)


def _window_attn_kernel(q_ref, k_ref, v_ref, bias_ref, sink_ref, gate_ref, o_ref, *, blk):
    blocks_per_step = q_ref.shape[0] // blk
    seq = k_ref.shape[0]
    nb = seq // blk
    step = pl.program_id(1)
    group = B_HEADS // B_KV
    ones = jnp.ones((3 * blk, LANES), BF16)
    col = _lane_iota((1, 3 * blk))

    def window(ref, t):
        i = step * blocks_per_step + t
        left = pl.multiple_of(jnp.maximum(i - 1, 0) * blk, blk)
        mid = pl.multiple_of(i * blk, blk)
        right = pl.multiple_of(jnp.minimum(i + 1, nb - 1) * blk, blk)
        return jnp.concatenate([ref[pl.ds(left, blk), :], ref[pl.ds(mid, blk), :],
                                ref[pl.ds(right, blk), :]], axis=0)

    def scores(t, g):
        i = step * blocks_per_step + t
        rows = slice(t * blk, (t + 1) * blk)
        q = jnp.concatenate([q_ref[rows, (g * group + j) * LANES:(g * group + j + 1) * LANES]
                             for j in range(group)], axis=0)
        s = lax.dot_general(q, window(k_ref, t), _NT, preferred_element_type=F32) + bias_ref[g]
        if t == 0:
            s = s + jnp.where(col < blk, jnp.where(i == 0, NEG, 0.0), 0.0)
        if t == blocks_per_step - 1:
            s = s + jnp.where(col >= 2 * blk, jnp.where(i == nb - 1, NEG, 0.0), 0.0)
        m = jnp.maximum(jnp.max(s, axis=-1, keepdims=True), sink_ref[g])
        return s, m

    def probs(s, m):
        return jnp.exp2(s - m).astype(BF16), m

    def output(t, g, p, m):
        v = jnp.concatenate([window(v_ref, t), ones], axis=1)
        o = jnp.dot(p, v, preferred_element_type=F32)
        denom = o[:, LANES:] + jnp.exp2(sink_ref[g] - m)
        return o[:, :LANES] * (1.0 / denom)

    items = [(t, g) for t in range(blocks_per_step) for g in range(B_KV)]
    sm, pb, outs = {}, {}, {}
    for n in range(len(items) + 2):
        if n < len(items):
            sm[n] = scores(*items[n])
        if 1 <= n <= len(items):
            pb[n - 1] = probs(*sm.pop(n - 1))
        if n >= 2:
            t, g = items[n - 2]
            outs[g] = output(t, g, *pb.pop(n - 2))
            if g == B_KV - 1:
                _unpad_heads_store([outs[x] for x in range(B_KV)], gate_ref, o_ref,
                                   B_HEADS, B_KV, blk, row0=t * blk)


def _window_attn(qb, kb, vb, bias, sink_rows, gate, blk, blocks_per_step):
    b, s, _ = qb.shape
    rows = (B_HEADS // B_KV) * blk
    tq = blk * blocks_per_step
    return pl.pallas_call(
        functools.partial(_window_attn_kernel, blk=blk),
        out_shape=jax.ShapeDtypeStruct((b, s, B_HEADS * HEAD_DIM), BF16),
        grid=(b, s // tq),
        in_specs=[pl.BlockSpec((None, tq, B_HEADS * LANES), lambda bi, i: (bi, i, 0)),
                  pl.BlockSpec((None, s, LANES), lambda bi, i: (bi, 0, 0)),
                  pl.BlockSpec((None, s, LANES), lambda bi, i: (bi, 0, 0)),
                  pl.BlockSpec((B_KV, rows, 3 * blk), lambda bi, i: (0, 0, 0)),
                  pl.BlockSpec((B_KV, rows, 1), lambda bi, i: (0, 0, 0)),
                  pl.BlockSpec((None, tq, B_HEADS * HEAD_DIM), lambda bi, i: (bi, i, 0))],
        out_specs=pl.BlockSpec((None, tq, B_HEADS * HEAD_DIM), lambda bi, i: (bi, i, 0)),
        compiler_params=_params("arbitrary", "arbitrary"),
        name="window_attn",
    )(qb, kb, vb, bias, sink_rows, gate)


def _mem_attn_kernel(q_ref, kv_ref, gate_ref, o_ref, *, sub):
    mlen = kv_ref.shape[0]
    ones = jnp.ones((mlen, M_HEAD_DIM), BF16)

    def scores(t, h):
        lo, hi = h * M_HEAD_DIM, (h + 1) * M_HEAD_DIM
        s = lax.dot_general(q_ref[t * sub:(t + 1) * sub, lo:hi], kv_ref[:, lo:hi], _NT,
                            preferred_element_type=F32)
        return s, jnp.max(s, axis=-1, keepdims=True)

    def probs(s, m):
        return jnp.exp2(s - m).astype(BF16)

    def output(t, h, p):
        lo, hi = h * M_HEAD_DIM, (h + 1) * M_HEAD_DIM
        v = jnp.concatenate([kv_ref[:, MEM_W + lo:MEM_W + hi], ones], axis=1)
        o = jnp.dot(p, v, preferred_element_type=F32)
        o = o[:, :M_HEAD_DIM] * (1.0 / o[:, M_HEAD_DIM:])
        rows = slice(t * sub, (t + 1) * sub)
        o_ref[rows, lo:hi] = (o * gate_ref[rows, lo:hi].astype(F32)).astype(BF16)

    items = [(t, h) for t in range(q_ref.shape[0] // sub) for h in range(M_HEADS)]
    sm, pb = {}, {}
    for n in range(len(items) + 2):
        if n < len(items):
            sm[n] = scores(*items[n])
        if 1 <= n <= len(items):
            pb[n - 1] = probs(*sm.pop(n - 1))
        if n >= 2:
            output(*items[n - 2], pb.pop(n - 2))


def _mem_attn(qm, mkv, layer, gate, tq):
    b, s, _ = qm.shape
    mlen = mkv.shape[2]
    return pl.pallas_call(
        functools.partial(_mem_attn_kernel, sub=256),
        out_shape=jax.ShapeDtypeStruct((b, s, MEM_W), BF16),
        grid=(b, s // tq),
        in_specs=[pl.BlockSpec((None, tq, MEM_W), lambda bi, i: (bi, i, 0)),
                  pl.BlockSpec((None, None, mlen, 2 * MEM_W), lambda bi, i: (layer, bi, 0, 0)),
                  pl.BlockSpec((None, tq, MEM_W), lambda bi, i: (bi, i, 0))],
        out_specs=pl.BlockSpec((None, tq, MEM_W), lambda bi, i: (bi, i, 0)),
        compiler_params=_params("arbitrary", "arbitrary"),
        name="mem_attn",
    )(qm, mkv, gate)


def _nbr_attn_kernel(q_ref, k_ref, v_ref, bias_ref, gate_ref, o_ref, *, grid_rows, rows_per_step):
    rb = pl.program_id(2)
    nkeys = NA_ROWS * GRID_W
    n_pairs = q_ref.shape[1] // LANES
    ones = jnp.ones((nkeys, LANES), BF16)
    starts, deltas = [], []
    for t in range(rows_per_step):
        r = rb * rows_per_step + t
        rs = jnp.clip(r - NA_ROWS // 2, 0, grid_rows - NA_ROWS)
        starts.append(pl.multiple_of(rs * GRID_W, GRID_W))
        deltas.append(r - rs)
    items = [(t, j) for t in range(rows_per_step) for j in range(n_pairs)]

    def scores(t, j):
        lo, hi = j * LANES, (j + 1) * LANES
        qt = q_ref[t * GRID_W:(t + 1) * GRID_W, lo:hi]
        lane = _lane_iota(qt.shape)
        zero = jnp.zeros_like(qt)
        q = jnp.concatenate([jnp.where(lane < HEAD_DIM, qt, zero),
                             jnp.where(lane >= HEAD_DIM, qt, zero)], axis=0)
        k = k_ref[pl.ds(starts[t], nkeys), lo:hi]
        s = lax.dot_general(q, k, _NT, preferred_element_type=F32) + bias_ref[deltas[t], j]
        return s, jnp.max(s, axis=-1, keepdims=True)

    def probs(s, m):
        return jnp.exp2(s - m).astype(BF16)

    def output(t, j, p):
        lo, hi = j * LANES, (j + 1) * LANES
        v = jnp.concatenate([v_ref[pl.ds(starts[t], nkeys), lo:hi], ones], axis=1)
        o = jnp.dot(p, v, preferred_element_type=F32)
        o = o[:, :LANES] * (1.0 / o[:, LANES:])
        lane_o = _lane_iota((GRID_W, LANES))
        tile = jnp.where(lane_o < HEAD_DIM, o[:GRID_W, :], o[GRID_W:, :])
        rows = slice(t * GRID_W, (t + 1) * GRID_W)
        o_ref[rows, lo:hi] = (tile * gate_ref[rows, lo:hi].astype(F32)).astype(BF16)

    sm = {}
    pb = {}
    for n in range(len(items) + 2):
        if n < len(items):
            sm[n] = scores(*items[n])
        if 1 <= n <= len(items):
            pb[n - 1] = probs(*sm.pop(n - 1))
        if n >= 2:
            output(*items[n - 2], pb.pop(n - 2))


def _nbr_attn(qc, kc, vc, bias, gate, rows_per_step, head_splits):
    b, s, w = qc.shape
    grid_rows = s // GRID_W
    nkeys = NA_ROWS * GRID_W
    wh = w // head_splits
    tq = rows_per_step * GRID_W
    blk = lambda hh, bi, rb: (bi, rb, hh)
    return pl.pallas_call(
        functools.partial(_nbr_attn_kernel, grid_rows=grid_rows, rows_per_step=rows_per_step),
        out_shape=jax.ShapeDtypeStruct((b, s, w), BF16),
        grid=(head_splits, b, grid_rows // rows_per_step),
        in_specs=[pl.BlockSpec((None, tq, wh), blk),
                  pl.BlockSpec((None, s, wh), lambda hh, bi, rb: (bi, 0, hh)),
                  pl.BlockSpec((None, s, wh), lambda hh, bi, rb: (bi, 0, hh)),
                  pl.BlockSpec((NA_ROWS, wh // LANES, 2 * GRID_W, nkeys), lambda hh, bi, rb: (0, hh, 0, 0)),
                  pl.BlockSpec((None, tq, wh), blk)],
        out_specs=pl.BlockSpec((None, tq, wh), blk),
        compiler_params=_params("arbitrary", "arbitrary", "arbitrary"),
        name="nbr_attn",
    )(qc, kc, vc, bias, gate)


def _out_proj_kernel(*refs, n_parts, final_norm):
    y_refs = refs[:n_parts]
    x_ref, w_ref = refs[n_parts], refs[n_parts + 1]
    if final_norm:
        g_ref, o_ref = refs[n_parts + 2], refs[n_parts + 3]
    else:
        o_ref = refs[n_parts + 2]
    acc = x_ref[...]
    off = 0
    for y_ref in y_refs:
        width = y_ref.shape[1]
        acc = acc + jnp.dot(y_ref[...], w_ref[off:off + width, :], preferred_element_type=F32)
        off += width
    if final_norm:
        acc = _rms_rows(acc, g_ref[...])
    o_ref[...] = acc


def _out_proj(ys, x2d, w_bf16, tm, final_gain=None):
    n_tok, d = x2d.shape
    row = lambda i: (i, 0)
    const = lambda i: (0, 0)
    in_specs = [pl.BlockSpec((tm, y.shape[1]), row) for y in ys]
    in_specs += [pl.BlockSpec((tm, d), row), pl.BlockSpec(w_bf16.shape, const)]
    args = list(ys) + [x2d, w_bf16]
    if final_gain is not None:
        in_specs.append(pl.BlockSpec((1, d), const))
        args.append(final_gain)
    return pl.pallas_call(
        functools.partial(_out_proj_kernel, n_parts=len(ys), final_norm=final_gain is not None),
        out_shape=jax.ShapeDtypeStruct((n_tok, d), F32),
        grid=(n_tok // tm,),
        in_specs=in_specs,
        out_specs=pl.BlockSpec((tm, d), row),
        compiler_params=_params("arbitrary"),
        name="out_proj_final" if final_gain is not None else "out_proj",
    )(*args)


def _rope_tables(seq):
    quarter = HEAD_DIM // 4
    freqs = jnp.power(ROPE_THETA, -jnp.arange(quarter, dtype=F32) / quarter)
    t = jnp.arange(seq)
    ang_r = (t // GRID_W).astype(F32)[:, None] * freqs
    ang_c = (t % GRID_W).astype(F32)[:, None] * freqs
    cos_h = jnp.concatenate([jnp.cos(ang_r), jnp.cos(ang_r), jnp.cos(ang_c), jnp.cos(ang_c)], axis=-1)
    sin_h = jnp.concatenate([-jnp.sin(ang_r), jnp.sin(ang_r), -jnp.sin(ang_c), jnp.sin(ang_c)], axis=-1)
    return jnp.tile(cos_h, (1, 2)), jnp.tile(sin_h, (1, 2))


def _t5_bucket(rel):
    nb = REL_BUCKETS // 2
    max_exact = nb // 2
    ret = jnp.where(rel > 0, nb, 0)
    n = jnp.abs(rel)
    nf = jnp.maximum(n, 1).astype(F32)
    large = max_exact + (jnp.log(nf / max_exact) / math.log(REL_MAX_DIST / max_exact)
                         * (nb - max_exact)).astype(jnp.int32)
    large = jnp.minimum(large, nb - 1)
    return ret + jnp.where(n < max_exact, n, large)


def _window_bias(rel_bias, blk):
    span = blk + 2 * WINDOW
    rel = (jnp.arange(span) - WINDOW)[None, :] - jnp.arange(blk)[:, None]
    band = jnp.abs(rel) <= WINDOW
    onehot = (_t5_bucket(rel)[None] == jnp.arange(REL_BUCKETS)[:, None, None]).astype(F32)
    bias = jnp.einsum('bh,bqs->hqs', rel_bias.astype(F32) * LOG2E, onehot,
                      precision=lax.Precision.HIGHEST)
    bias = jnp.where(band[None], bias, NEG)
    group = B_HEADS // B_KV
    return bias.reshape(B_KV, group * blk, span)


def _nbr_bias(rpb):
    col = np.arange(GRID_W)
    cs = np.clip(col - NA_COLS // 2, 0, GRID_W - NA_COLS)
    colmask = (col[None, :] >= cs[:, None]) & (col[None, :] < cs[:, None] + NA_COLS)
    dc = np.clip(col[None, :] - col[:, None] + NA_COLS - 1, 0, 2 * NA_COLS - 2)
    onehot = (dc[None] == np.arange(2 * NA_COLS - 1)[:, None, None]) & colmask[None]
    t = jnp.einsum('hrc,cqk->hqrk', rpb.astype(F32) * LOG2E, jnp.asarray(onehot, F32),
                   precision=lax.Precision.HIGHEST)
    t = t + jnp.asarray(np.where(colmask, 0.0, NEG), F32)[None, :, None, :]
    per_delta = [t[:, :, NA_ROWS - 1 - dl:2 * NA_ROWS - 1 - dl, :] for dl in range(NA_ROWS)]
    return jnp.stack(per_delta).reshape(NA_ROWS, C_HEADS // 2, 2 * GRID_W, NA_ROWS * GRID_W)


def kernel(x, mem, norm_gain, mem_norm_gain, w_in_even, w_out_even, q_norm_a, k_norm_a, sink_b,
           rel_bias, w_in_odd, w_out_odd, rpb_c, w_mem_kv, final_norm_gain):
    b, s, d = x.shape
    mlen = mem.shape[1]
    assert s % GRID_W == 0 and s // GRID_W >= NA_ROWS and s % 512 == 0
    tm = 512
    blk = 128

    x2d = x.reshape(b * s, d)
    mem_tm = math.gcd(b * mlen, 512)
    mkv = _mem_kv(mem.reshape(b * mlen, d), mem_norm_gain.reshape(1, d), w_mem_kv.astype(BF16), mem_tm)
    mkv = mkv.reshape(w_mem_kv.shape[0], b, mlen, 2 * MEM_W)

    cos, sin = _rope_tables(s)
    gq = jnp.tile(q_norm_a[0].astype(F32), 2).reshape(1, LANES)
    gk = jnp.tile(k_norm_a[0].astype(F32), 2).reshape(1, LANES)
    qa, ka, va, qb, kb, vb, qm, ga, gb, gm = _in_even(
        x2d, norm_gain[0].reshape(1, d), w_in_even[0].astype(BF16), gq, gk, cos, sin, s, tm)
    r3 = lambda a: a.reshape(b, s, a.shape[-1])
    ya = _global_attn(r3(qa), r3(ka), va, r3(ga), tq=256)
    group_b = B_HEADS // B_KV
    sink_rows = jnp.repeat(sink_b[0].astype(F32).reshape(B_KV, group_b) * LOG2E, blk,
                           axis=1).reshape(B_KV, group_b * blk, 1)
    yb = _window_attn(r3(qb), r3(kb), r3(vb), _window_bias(rel_bias, blk), sink_rows, r3(gb), blk,
                      blocks_per_step=4)
    ym = _mem_attn(r3(qm), mkv, 0, r3(gm), tq=math.gcd(s, 1024))
    x1 = _out_proj([ya.reshape(b * s, -1), yb.reshape(b * s, -1), ym.reshape(b * s, -1)],
                   x2d, w_out_even[0].astype(BF16), tm)

    qc, kc, vc, qm1, gc, gm1 = _in_odd(x1, norm_gain[1].reshape(1, d), w_in_odd[0].astype(BF16), tm)
    yc = _nbr_attn(r3(qc), r3(kc), r3(vc), _nbr_bias(rpb_c[0]), r3(gc), rows_per_step=4, head_splits=2)
    ym1 = _mem_attn(r3(qm1), mkv, 1, r3(gm1), tq=math.gcd(s, 1024))
    out = _out_proj([yc.reshape(b * s, -1), ym1.reshape(b * s, -1)], x1, w_out_odd[0].astype(BF16), tm,
                    final_gain=final_norm_gain.reshape(1, d))
    return out.reshape(b, s, d)
```

```python
import functools
import math

import jax
import jax.numpy as jnp
import numpy as np
from jax import lax
from jax.experimental import pallas as pl
from jax.experimental.pallas import tpu as pltpu

GRID_W = 64
HEAD_DIM = 64
A_HEADS = 8
A_KV = 2
B_HEADS = 8
B_KV = 2
WINDOW = 128
C_HEADS = 16
NA_ROWS = 8
NA_COLS = 16
M_HEADS = 4
M_HEAD_DIM = 128
MEM_W = M_HEADS * M_HEAD_DIM
REL_BUCKETS = 32
REL_MAX_DIST = 128
ROPE_THETA = 10000.0
EPS = 1e-6

LANES = 128
NEG = -1e30
LOG2E = math.log2(math.e)
V_ONES_ROWS = 16
VMEM_LIMIT_BYTES = 56 * 1024 * 1024

F32 = jnp.float32
BF16 = jnp.bfloat16

_NT = (((1,), (1,)), ((), ()))


def _params(*sem):
    return pltpu.CompilerParams(dimension_semantics=sem, vmem_limit_bytes=VMEM_LIMIT_BYTES)


def _rms_rows(x, gain):
    ms = jnp.mean(x * x, axis=-1, keepdims=True)
    return x * lax.rsqrt(ms + EPS) * gain


def _lane_iota(shape):
    return lax.broadcasted_iota(jnp.int32, shape, len(shape) - 1)


def _silu(x):
    return x * (1.0 / (1.0 + jnp.exp(-x)))


def _half_ones():
    r = lax.broadcasted_iota(jnp.int32, (LANES, LANES), 0) // HEAD_DIM
    c = lax.broadcasted_iota(jnp.int32, (LANES, LANES), 1) // HEAD_DIM
    return jnp.where(r == c, 1.0, 0.0).astype(BF16)


def _head_norm_rope(t, gain, cos, sin_signed, ones_bd):
    ss = jnp.dot((t * t).astype(BF16), ones_bd, preferred_element_type=F32)
    tn = t * lax.rsqrt(ss * (1.0 / HEAD_DIM) + EPS) * gain
    lane = _lane_iota(tn.shape)
    quarter = HEAD_DIM // 4
    partner = jnp.where((lane % (2 * quarter)) < quarter,
                        pltpu.roll(tn, LANES - quarter, 1), pltpu.roll(tn, quarter, 1))
    return tn * cos + partner * sin_signed


def _store_padded_heads(q_tiles, out_ref, n_heads, n_kv):
    group = n_heads // n_kv
    for h in range(n_heads):
        t = q_tiles[h // 2]
        src_half = h % 2
        dst_half = (h // group) % 2
        if src_half != dst_half:
            t = pltpu.roll(t, HEAD_DIM, 1)
        lane = _lane_iota(t.shape)
        keep = (lane >= HEAD_DIM) if dst_half == 1 else (lane < HEAD_DIM)
        out_ref[:, h * LANES:(h + 1) * LANES] = jnp.where(keep, t, 0.0).astype(BF16)


def _mem_kv_kernel(mem_ref, g_ref, w_ref, o_ref):
    h = _rms_rows(mem_ref[...], g_ref[...]).astype(BF16)
    o_ref[...] = jnp.dot(h, w_ref[...], preferred_element_type=F32).astype(BF16)


def _mem_kv(mem2d, gain, w_bf16, tm):
    depth, d, n = w_bf16.shape
    rows = mem2d.shape[0]
    return pl.pallas_call(
        _mem_kv_kernel,
        out_shape=jax.ShapeDtypeStruct((depth, rows, n), BF16),
        grid=(depth, rows // tm),
        in_specs=[pl.BlockSpec((tm, d), lambda l, i: (i, 0)),
                  pl.BlockSpec((1, d), lambda l, i: (0, 0)),
                  pl.BlockSpec((None, d, n), lambda l, i: (l, 0, 0))],
        out_specs=pl.BlockSpec((None, tm, n), lambda l, i: (l, i, 0)),
        compiler_params=_params("arbitrary", "arbitrary"),
        name="mem_kv_proj",
    )(mem2d, gain, w_bf16)


def _in_even_kernel(x_ref, g_ref, w_ref, gq_ref, gk_ref, cos_ref, sin_ref,
                    qa_ref, ka_ref, va_ref, qb_ref, kb_ref, vb_ref, qm_ref,
                    ga_ref, gb_ref, gm_ref):
    h = _rms_rows(x_ref[...], g_ref[...]).astype(BF16)

    def seg(lo, hi):
        return jnp.dot(h, w_ref[:, lo:hi], preferred_element_type=F32)

    ones_bd = _half_ones()
    cos = cos_ref[...]
    sin = sin_ref[...]
    scale = HEAD_DIM ** -0.5
    qa_w = A_HEADS * HEAD_DIM
    kva_w = A_KV * HEAD_DIM
    qb_w = B_HEADS * HEAD_DIM
    kvb_w = B_KV * HEAD_DIM
    o = 0
    zq = seg(o, o + qa_w)
    tiles = [_head_norm_rope(zq[:, j * LANES:(j + 1) * LANES], gq_ref[...], cos, sin, ones_bd)
             * (scale * LOG2E) for j in range(qa_w // LANES)]
    _store_padded_heads(tiles, qa_ref, A_HEADS, A_KV)
    o += qa_w
    ka_ref[...] = _head_norm_rope(seg(o, o + kva_w), gk_ref[...], cos, sin, ones_bd).astype(BF16)
    o += kva_w
    vt = seg(o, o + kva_w).T.astype(BF16)
    for g in range(A_KV):
        va_ref[g, :HEAD_DIM] = vt[g * HEAD_DIM:(g + 1) * HEAD_DIM]
        va_ref[g, HEAD_DIM:] = jnp.ones((V_ONES_ROWS, vt.shape[1]), BF16)
    o += kva_w
    zq = seg(o, o + qb_w) * (scale * LOG2E)
    _store_padded_heads([zq[:, j * LANES:(j + 1) * LANES] for j in range(qb_w // LANES)],
                        qb_ref, B_HEADS, B_KV)
    o += qb_w
    kb_ref[...] = seg(o, o + kvb_w).astype(BF16)
    o += kvb_w
    vt = seg(o, o + kvb_w).T.astype(BF16)
    for j in range(vb_ref.shape[0]):
        for g in range(B_KV):
            vb_ref[j, g, :HEAD_DIM] = vt[g * HEAD_DIM:(g + 1) * HEAD_DIM, j * WINDOW:(j + 1) * WINDOW]
            vb_ref[j, g, HEAD_DIM:] = jnp.ones((V_ONES_ROWS, WINDOW), BF16)
    o += kvb_w
    qm_ref[...] = (seg(o, o + MEM_W) * (M_HEAD_DIM ** -0.5 * LOG2E)).astype(BF16)
    o += MEM_W
    ga_ref[...] = _silu(seg(o, o + qa_w)).astype(BF16)
    o += qa_w
    gb_ref[...] = _silu(seg(o, o + qb_w)).astype(BF16)
    o += qb_w
    gm_ref[...] = _silu(seg(o, o + MEM_W)).astype(BF16)


def _in_even(x2d, gain, w_bf16, gq, gk, cos, sin, seq, tm):
    n_tok, d = x2d.shape
    n_in = w_bf16.shape[1]
    per_seq = seq // tm
    widths = [A_HEADS * LANES, A_KV * HEAD_DIM, A_KV * HEAD_DIM,
              B_HEADS * LANES, B_KV * HEAD_DIM, B_KV * HEAD_DIM, MEM_W,
              A_HEADS * HEAD_DIM, B_HEADS * HEAD_DIM, MEM_W]
    row = lambda i: (i, 0)
    const = lambda i: (0, 0)
    out_shape = [jax.ShapeDtypeStruct((n_tok, w), BF16) for w in widths]
    out_specs = [pl.BlockSpec((tm, w), row) for w in widths]
    vrows = HEAD_DIM + V_ONES_ROWS
    out_shape[2] = jax.ShapeDtypeStruct((n_tok // tm, A_KV, vrows, tm), BF16)
    out_specs[2] = pl.BlockSpec((None, A_KV, vrows, tm), lambda i: (i, 0, 0, 0))
    out_shape[5] = jax.ShapeDtypeStruct((n_tok // WINDOW, B_KV, vrows, WINDOW), BF16)
    out_specs[5] = pl.BlockSpec((tm // WINDOW, B_KV, vrows, WINDOW), lambda i: (i, 0, 0, 0))
    return pl.pallas_call(
        _in_even_kernel,
        out_shape=out_shape,
        grid=(n_tok // tm,),
        in_specs=[pl.BlockSpec((tm, d), row),
                  pl.BlockSpec((1, d), const),
                  pl.BlockSpec((d, n_in), const),
                  pl.BlockSpec((1, LANES), const),
                  pl.BlockSpec((1, LANES), const),
                  pl.BlockSpec((tm, LANES), lambda i: (i % per_seq, 0)),
                  pl.BlockSpec((tm, LANES), lambda i: (i % per_seq, 0))],
        out_specs=out_specs,
        compiler_params=_params("arbitrary"),
        name="in_proj_even",
    )(x2d, gain, w_bf16, gq, gk, cos, sin)


def _in_odd_kernel(x_ref, g_ref, w_ref, qc_ref, kc_ref, vc_ref, qm_ref, gc_ref, gm_ref):
    h = _rms_rows(x_ref[...], g_ref[...]).astype(BF16)

    def seg(lo, hi):
        return jnp.dot(h, w_ref[:, lo:hi], preferred_element_type=F32)

    cw = C_HEADS * HEAD_DIM
    o = 0
    qc_ref[...] = (seg(o, o + cw) * (HEAD_DIM ** -0.5 * LOG2E)).astype(BF16)
    o += cw
    kc_ref[...] = seg(o, o + cw).astype(BF16)
    o += cw
    vc_ref[...] = seg(o, o + cw).astype(BF16)
    o += cw
    qm_ref[...] = (seg(o, o + MEM_W) * (M_HEAD_DIM ** -0.5 * LOG2E)).astype(BF16)
    o += MEM_W
    gc_ref[...] = _silu(seg(o, o + cw)).astype(BF16)
    o += cw
    gm_ref[...] = _silu(seg(o, o + MEM_W)).astype(BF16)


def _in_odd(x2d, gain, w_bf16, tm):
    n_tok, d = x2d.shape
    n_in = w_bf16.shape[1]
    cw = C_HEADS * HEAD_DIM
    widths = [cw, cw, cw, MEM_W, cw, MEM_W]
    row = lambda i: (i, 0)
    const = lambda i: (0, 0)
    return pl.pallas_call(
        _in_odd_kernel,
        out_shape=[jax.ShapeDtypeStruct((n_tok, w), BF16) for w in widths],
        grid=(n_tok // tm,),
        in_specs=[pl.BlockSpec((tm, d), row),
                  pl.BlockSpec((1, d), const),
                  pl.BlockSpec((d, n_in), const)],
        out_specs=[pl.BlockSpec((tm, w), row) for w in widths],
        compiler_params=_params("arbitrary"),
        name="in_proj_odd",
    )(x2d, gain, w_bf16)


def _unpad_heads_store(o_rows, gate_ref, out_ref, n_heads, n_kv, tq, row0=0):
    group = n_heads // n_kv
    rows = slice(row0, row0 + tq)
    for j in range(n_heads // 2):
        parts = []
        for h in (2 * j, 2 * j + 1):
            g = h // group
            i = h % group
            t = o_rows[g][i * tq:(i + 1) * tq, :]
            if (g % 2) != (h % 2):
                t = pltpu.roll(t, HEAD_DIM, 1)
            parts.append(t)
        lane = _lane_iota(parts[0].shape)
        tile = jnp.where(lane < HEAD_DIM, parts[0], parts[1])
        gate = gate_ref[rows, j * LANES:(j + 1) * LANES].astype(F32)
        out_ref[rows, j * LANES:(j + 1) * LANES] = (tile * gate).astype(BF16)


def _global_attn_kernel(q_ref, k_ref, vt_ref, gate_ref, o_ref, m_scr, acc_scr, s_scr, mc_scr):
    tq = q_ref.shape[0]
    n_chunks = vt_ref.shape[0]
    tk = vt_ref.shape[3]
    group = A_HEADS // A_KV
    qs = [jnp.concatenate([q_ref[:, (g * group + i) * LANES:(g * group + i + 1) * LANES]
                           for i in range(group)], axis=0) for g in range(A_KV)]
    m_scr[...] = jnp.full(m_scr.shape, NEG, F32)
    acc_scr[...] = jnp.zeros(acc_scr.shape, F32)

    def scores(c, slot):
        start = pl.multiple_of(c * tk, tk)
        k = k_ref[pl.ds(start, tk), :]
        for g in range(A_KV):
            st = lax.dot_general(k, qs[g], _NT, preferred_element_type=F32)
            s_scr[slot, g] = st
            mc_scr[slot, g] = jnp.max(st, axis=0, keepdims=True)

    def accumulate(c, slot):
        for g in range(A_KV):
            m_old = m_scr[g]
            m_new = jnp.maximum(m_old, mc_scr[slot, g])
            alpha = jnp.exp2(m_old - m_new)
            pt = jnp.exp2(s_scr[slot, g] - m_new).astype(BF16)
            acc_scr[g] = alpha * acc_scr[g] + jnp.dot(vt_ref[c, g], pt, preferred_element_type=F32)
            m_scr[g] = m_new

    scores(0, 0)

    def body(c2, carry):
        c = 2 * c2
        scores(c + 1, 1)
        accumulate(c, 0)
        scores(c + 2, 0)
        accumulate(c + 1, 1)
        return carry

    lax.fori_loop(0, n_chunks // 2 - 1, body, 0)
    scores(n_chunks - 1, 1)
    accumulate(n_chunks - 2, 0)
    accumulate(n_chunks - 1, 1)

    ot = [acc_scr[g, :HEAD_DIM] * (1.0 / acc_scr[g, HEAD_DIM:HEAD_DIM + 1]) for g in range(A_KV)]
    for j in range(A_HEADS // 2):
        g, i0 = (2 * j) // group, (2 * j) % group
        tile_t = jnp.concatenate([ot[g][:, i0 * tq:(i0 + 1) * tq],
                                  ot[g][:, (i0 + 1) * tq:(i0 + 2) * tq]], axis=0)
        gate = gate_ref[:, j * LANES:(j + 1) * LANES].astype(F32)
        o_ref[:, j * LANES:(j + 1) * LANES] = (tile_t.T * gate).astype(BF16)


def _global_attn(qa, ka, vat, gate, tq):
    b, s, _ = qa.shape
    _, n_kv, vrows, tk = vat.shape
    n_chunks = s // tk
    assert n_chunks % 2 == 0
    rows = (A_HEADS // A_KV) * tq
    vat = vat.reshape(b, n_chunks, n_kv, vrows, tk)
    return pl.pallas_call(
        _global_attn_kernel,
        out_shape=jax.ShapeDtypeStruct((b, s, A_HEADS * HEAD_DIM), BF16),
        grid=(b, s // tq),
        in_specs=[pl.BlockSpec((None, tq, A_HEADS * LANES), lambda bi, i: (bi, i, 0)),
                  pl.BlockSpec((None, s, LANES), lambda bi, i: (bi, 0, 0)),
                  pl.BlockSpec((None, n_chunks, n_kv, vrows, tk), lambda bi, i: (bi, 0, 0, 0, 0)),
                  pl.BlockSpec((None, tq, A_HEADS * HEAD_DIM), lambda bi, i: (bi, i, 0))],
        out_specs=pl.BlockSpec((None, tq, A_HEADS * HEAD_DIM), lambda bi, i: (bi, i, 0)),
        scratch_shapes=[pltpu.VMEM((A_KV, 1, rows), F32),
                        pltpu.VMEM((A_KV, vrows, rows), F32),
                        pltpu.VMEM((2, A_KV, tk, rows), F32), pltpu.VMEM((2, A_KV, 1, rows), F32)],
        compiler_params=_params("arbitrary", "arbitrary"),
        name="global_attn",
    )(qa, ka, vat, gate)


def _window_attn_kernel(q_ref, k_ref, vt_ref, bias_ref, sink_ref, gate_ref, o_ref, *, blk):
    blocks_per_step = q_ref.shape[0] // blk
    nb = vt_ref.shape[0]
    step = pl.program_id(1)
    group = B_HEADS // B_KV

    def neighbours(t):
        i = step * blocks_per_step + t
        return jnp.maximum(i - 1, 0), i, jnp.minimum(i + 1, nb - 1)

    def scores(t, g):
        i = step * blocks_per_step + t
        rows = slice(t * blk, (t + 1) * blk)
        q = jnp.concatenate([q_ref[rows, (g * group + j) * LANES:(g * group + j + 1) * LANES]
                             for j in range(group)], axis=0)
        k = jnp.concatenate([k_ref[pl.ds(pl.multiple_of(n * blk, blk), blk), :]
                             for n in neighbours(t)], axis=0)
        st = lax.dot_general(k, q, _NT, preferred_element_type=F32) + bias_ref[g]
        if t == 0:
            st = jnp.concatenate([st[:blk] + jnp.where(i == 0, NEG, 0.0), st[blk:]], axis=0)
        if t == blocks_per_step - 1:
            st = jnp.concatenate([st[:2 * blk], st[2 * blk:] + jnp.where(i == nb - 1, NEG, 0.0)], axis=0)
        m = jnp.maximum(jnp.max(st, axis=0, keepdims=True), sink_ref[g])
        return st, m

    def probs(st, m):
        return jnp.exp2(st - m).astype(BF16), m

    def output(t, g, pt, m):
        vt = jnp.concatenate([vt_ref[n, g] for n in neighbours(t)], axis=1)
        ot = jnp.dot(vt, pt, preferred_element_type=F32)
        denom = ot[HEAD_DIM:HEAD_DIM + 1] + jnp.exp2(sink_ref[g] - m)
        return ot[:HEAD_DIM] * (1.0 / denom)

    def store(t, ot):
        rows = slice(t * blk, (t + 1) * blk)
        for j in range(B_HEADS // 2):
            g, i0 = (2 * j) // group, (2 * j) % group
            tile_t = jnp.concatenate([ot[g][:, i0 * blk:(i0 + 1) * blk],
                                      ot[g][:, (i0 + 1) * blk:(i0 + 2) * blk]], axis=0)
            gate = gate_ref[rows, j * LANES:(j + 1) * LANES].astype(F32)
            o_ref[rows, j * LANES:(j + 1) * LANES] = (tile_t.T * gate).astype(BF16)

    items = [(t, g) for t in range(blocks_per_step) for g in range(B_KV)]
    sm, pb, outs = {}, {}, {}
    for n in range(len(items) + 2):
        if n < len(items):
            sm[n] = scores(*items[n])
        if 1 <= n <= len(items):
            pb[n - 1] = probs(*sm.pop(n - 1))
        if n >= 2:
            t, g = items[n - 2]
            outs[g] = output(t, g, *pb.pop(n - 2))
            if g == B_KV - 1:
                store(t, outs)


def _window_attn(qb, kb, vbt, bias_t, sink_cols, gate, blk, blocks_per_step):
    b, s, _ = qb.shape
    nb = s // blk
    rows = (B_HEADS // B_KV) * blk
    tq = blk * blocks_per_step
    vrows = vbt.shape[2]
    vbt = vbt.reshape(b, nb, B_KV, vrows, blk)
    return pl.pallas_call(
        functools.partial(_window_attn_kernel, blk=blk),
        out_shape=jax.ShapeDtypeStruct((b, s, B_HEADS * HEAD_DIM), BF16),
        grid=(b, s // tq),
        in_specs=[pl.BlockSpec((None, tq, B_HEADS * LANES), lambda bi, i: (bi, i, 0)),
                  pl.BlockSpec((None, s, LANES), lambda bi, i: (bi, 0, 0)),
                  pl.BlockSpec((None, nb, B_KV, vrows, blk), lambda bi, i: (bi, 0, 0, 0, 0)),
                  pl.BlockSpec((B_KV, 3 * blk, rows), lambda bi, i: (0, 0, 0)),
                  pl.BlockSpec((B_KV, 1, rows), lambda bi, i: (0, 0, 0)),
                  pl.BlockSpec((None, tq, B_HEADS * HEAD_DIM), lambda bi, i: (bi, i, 0))],
        out_specs=pl.BlockSpec((None, tq, B_HEADS * HEAD_DIM), lambda bi, i: (bi, i, 0)),
        compiler_params=_params("arbitrary", "arbitrary"),
        name="window_attn",
    )(qb, kb, vbt, bias_t, sink_cols, gate)


def _mem_attn_kernel(q_ref, kv_ref, gate_ref, o_ref, *, sub):
    mlen = kv_ref.shape[0]
    ones = jnp.ones((mlen, M_HEAD_DIM), BF16)

    def scores(t, h):
        lo, hi = h * M_HEAD_DIM, (h + 1) * M_HEAD_DIM
        s = lax.dot_general(q_ref[t * sub:(t + 1) * sub, lo:hi], kv_ref[:, lo:hi], _NT,
                            preferred_element_type=F32)
        return s, jnp.max(s, axis=-1, keepdims=True)

    def probs(s, m):
        return jnp.exp2(s - m).astype(BF16)

    def output(t, h, p):
        lo, hi = h * M_HEAD_DIM, (h + 1) * M_HEAD_DIM
        v = jnp.concatenate([kv_ref[:, MEM_W + lo:MEM_W + hi], ones], axis=1)
        o = jnp.dot(p, v, preferred_element_type=F32)
        o = o[:, :M_HEAD_DIM] * (1.0 / o[:, M_HEAD_DIM:])
        rows = slice(t * sub, (t + 1) * sub)
        o_ref[rows, lo:hi] = (o * gate_ref[rows, lo:hi].astype(F32)).astype(BF16)

    items = [(t, h) for t in range(q_ref.shape[0] // sub) for h in range(M_HEADS)]
    sm, pb = {}, {}
    for n in range(len(items) + 2):
        if n < len(items):
            sm[n] = scores(*items[n])
        if 1 <= n <= len(items):
            pb[n - 1] = probs(*sm.pop(n - 1))
        if n >= 2:
            output(*items[n - 2], pb.pop(n - 2))


def _mem_attn(qm, mkv, layer, gate, tq):
    b, s, _ = qm.shape
    mlen = mkv.shape[2]
    return pl.pallas_call(
        functools.partial(_mem_attn_kernel, sub=256),
        out_shape=jax.ShapeDtypeStruct((b, s, MEM_W), BF16),
        grid=(b, s // tq),
        in_specs=[pl.BlockSpec((None, tq, MEM_W), lambda bi, i: (bi, i, 0)),
                  pl.BlockSpec((None, None, mlen, 2 * MEM_W), lambda bi, i: (layer, bi, 0, 0)),
                  pl.BlockSpec((None, tq, MEM_W), lambda bi, i: (bi, i, 0))],
        out_specs=pl.BlockSpec((None, tq, MEM_W), lambda bi, i: (bi, i, 0)),
        compiler_params=_params("arbitrary", "arbitrary"),
        name="mem_attn",
    )(qm, mkv, gate)


def _nbr_attn_kernel(q_ref, k_ref, v_ref, bias_ref, gate_ref, o_ref, *, grid_rows, rows_per_step):
    rb = pl.program_id(2)
    nkeys = NA_ROWS * GRID_W
    n_pairs = q_ref.shape[1] // LANES
    ones = jnp.ones((nkeys, LANES), BF16)
    starts, deltas = [], []
    for t in range(rows_per_step):
        r = rb * rows_per_step + t
        rs = jnp.clip(r - NA_ROWS // 2, 0, grid_rows - NA_ROWS)
        starts.append(pl.multiple_of(rs * GRID_W, GRID_W))
        deltas.append(r - rs)
    items = [(t, j) for t in range(rows_per_step) for j in range(n_pairs)]

    def scores(t, j):
        lo, hi = j * LANES, (j + 1) * LANES
        qt = q_ref[t * GRID_W:(t + 1) * GRID_W, lo:hi]
        lane = _lane_iota(qt.shape)
        zero = jnp.zeros_like(qt)
        q = jnp.concatenate([jnp.where(lane < HEAD_DIM, qt, zero),
                             jnp.where(lane >= HEAD_DIM, qt, zero)], axis=0)
        k = k_ref[pl.ds(starts[t], nkeys), lo:hi]
        s = lax.dot_general(q, k, _NT, preferred_element_type=F32) + bias_ref[deltas[t], j]
        return s, jnp.max(s, axis=-1, keepdims=True)

    def probs(s, m):
        return jnp.exp2(s - m).astype(BF16)

    def output(t, j, p):
        lo, hi = j * LANES, (j + 1) * LANES
        v = jnp.concatenate([v_ref[pl.ds(starts[t], nkeys), lo:hi], ones], axis=1)
        o = jnp.dot(p, v, preferred_element_type=F32)
        o = o[:, :LANES] * (1.0 / o[:, LANES:])
        lane_o = _lane_iota((GRID_W, LANES))
        tile = jnp.where(lane_o < HEAD_DIM, o[:GRID_W, :], o[GRID_W:, :])
        rows = slice(t * GRID_W, (t + 1) * GRID_W)
        o_ref[rows, lo:hi] = (tile * gate_ref[rows, lo:hi].astype(F32)).astype(BF16)

    sm = {}
    pb = {}
    for n in range(len(items) + 2):
        if n < len(items):
            sm[n] = scores(*items[n])
        if 1 <= n <= len(items):
            pb[n - 1] = probs(*sm.pop(n - 1))
        if n >= 2:
            output(*items[n - 2], pb.pop(n - 2))


def _nbr_attn(qc, kc, vc, bias, gate, rows_per_step, head_splits):
    b, s, w = qc.shape
    grid_rows = s // GRID_W
    nkeys = NA_ROWS * GRID_W
    wh = w // head_splits
    tq = rows_per_step * GRID_W
    blk = lambda hh, bi, rb: (bi, rb, hh)
    return pl.pallas_call(
        functools.partial(_nbr_attn_kernel, grid_rows=grid_rows, rows_per_step=rows_per_step),
        out_shape=jax.ShapeDtypeStruct((b, s, w), BF16),
        grid=(head_splits, b, grid_rows // rows_per_step),
        in_specs=[pl.BlockSpec((None, tq, wh), blk),
                  pl.BlockSpec((None, s, wh), lambda hh, bi, rb: (bi, 0, hh)),
                  pl.BlockSpec((None, s, wh), lambda hh, bi, rb: (bi, 0, hh)),
                  pl.BlockSpec((NA_ROWS, wh // LANES, 2 * GRID_W, nkeys), lambda hh, bi, rb: (0, hh, 0, 0)),
                  pl.BlockSpec((None, tq, wh), blk)],
        out_specs=pl.BlockSpec((None, tq, wh), blk),
        compiler_params=_params("arbitrary", "arbitrary", "arbitrary"),
        name="nbr_attn",
    )(qc, kc, vc, bias, gate)


def _out_proj_kernel(*refs, n_parts, final_norm):
    y_refs = refs[:n_parts]
    x_ref, w_ref = refs[n_parts], refs[n_parts + 1]
    if final_norm:
        g_ref, o_ref = refs[n_parts + 2], refs[n_parts + 3]
    else:
        o_ref = refs[n_parts + 2]
    acc = x_ref[...]
    off = 0
    for y_ref in y_refs:
        width = y_ref.shape[1]
        acc = acc + jnp.dot(y_ref[...], w_ref[off:off + width, :], preferred_element_type=F32)
        off += width
    if final_norm:
        acc = _rms_rows(acc, g_ref[...])
    o_ref[...] = acc


def _out_proj(ys, x2d, w_bf16, tm, final_gain=None):
    n_tok, d = x2d.shape
    row = lambda i: (i, 0)
    const = lambda i: (0, 0)
    in_specs = [pl.BlockSpec((tm, y.shape[1]), row) for y in ys]
    in_specs += [pl.BlockSpec((tm, d), row), pl.BlockSpec(w_bf16.shape, const)]
    args = list(ys) + [x2d, w_bf16]
    if final_gain is not None:
        in_specs.append(pl.BlockSpec((1, d), const))
        args.append(final_gain)
    return pl.pallas_call(
        functools.partial(_out_proj_kernel, n_parts=len(ys), final_norm=final_gain is not None),
        out_shape=jax.ShapeDtypeStruct((n_tok, d), F32),
        grid=(n_tok // tm,),
        in_specs=in_specs,
        out_specs=pl.BlockSpec((tm, d), row),
        compiler_params=_params("arbitrary"),
        name="out_proj_final" if final_gain is not None else "out_proj",
    )(*args)


def _rope_tables(seq):
    quarter = HEAD_DIM // 4
    freqs = jnp.power(ROPE_THETA, -jnp.arange(quarter, dtype=F32) / quarter)
    t = jnp.arange(seq)
    ang_r = (t // GRID_W).astype(F32)[:, None] * freqs
    ang_c = (t % GRID_W).astype(F32)[:, None] * freqs
    cos_h = jnp.concatenate([jnp.cos(ang_r), jnp.cos(ang_r), jnp.cos(ang_c), jnp.cos(ang_c)], axis=-1)
    sin_h = jnp.concatenate([-jnp.sin(ang_r), jnp.sin(ang_r), -jnp.sin(ang_c), jnp.sin(ang_c)], axis=-1)
    return jnp.tile(cos_h, (1, 2)), jnp.tile(sin_h, (1, 2))


def _t5_bucket(rel):
    nb = REL_BUCKETS // 2
    max_exact = nb // 2
    ret = jnp.where(rel > 0, nb, 0)
    n = jnp.abs(rel)
    nf = jnp.maximum(n, 1).astype(F32)
    large = max_exact + (jnp.log(nf / max_exact) / math.log(REL_MAX_DIST / max_exact)
                         * (nb - max_exact)).astype(jnp.int32)
    large = jnp.minimum(large, nb - 1)
    return ret + jnp.where(n < max_exact, n, large)


def _window_bias(rel_bias, blk):
    span = blk + 2 * WINDOW
    rel = (jnp.arange(span) - WINDOW)[None, :] - jnp.arange(blk)[:, None]
    band = jnp.abs(rel) <= WINDOW
    onehot = (_t5_bucket(rel)[None] == jnp.arange(REL_BUCKETS)[:, None, None]).astype(F32)
    bias = jnp.einsum('bh,bqs->shq', rel_bias.astype(F32) * LOG2E, onehot,
                      precision=lax.Precision.HIGHEST)
    bias = jnp.where(band.T[:, None, :], bias, NEG)
    group = B_HEADS // B_KV
    return bias.reshape(span, B_KV, group * blk).transpose(1, 0, 2)


def _nbr_bias(rpb):
    col = np.arange(GRID_W)
    cs = np.clip(col - NA_COLS // 2, 0, GRID_W - NA_COLS)
    colmask = (col[None, :] >= cs[:, None]) & (col[None, :] < cs[:, None] + NA_COLS)
    dc = np.clip(col[None, :] - col[:, None] + NA_COLS - 1, 0, 2 * NA_COLS - 2)
    onehot = (dc[None] == np.arange(2 * NA_COLS - 1)[:, None, None]) & colmask[None]
    t = jnp.einsum('hrc,cqk->hqrk', rpb.astype(F32) * LOG2E, jnp.asarray(onehot, F32),
                   precision=lax.Precision.HIGHEST)
    t = t + jnp.asarray(np.where(colmask, 0.0, NEG), F32)[None, :, None, :]
    per_delta = [t[:, :, NA_ROWS - 1 - dl:2 * NA_ROWS - 1 - dl, :] for dl in range(NA_ROWS)]
    return jnp.stack(per_delta).reshape(NA_ROWS, C_HEADS // 2, 2 * GRID_W, NA_ROWS * GRID_W)


def kernel(x, mem, norm_gain, mem_norm_gain, w_in_even, w_out_even, q_norm_a, k_norm_a, sink_b,
           rel_bias, w_in_odd, w_out_odd, rpb_c, w_mem_kv, final_norm_gain):
    b, s, d = x.shape
    mlen = mem.shape[1]
    assert s % GRID_W == 0 and s // GRID_W >= NA_ROWS and s % 512 == 0
    tm = 512
    blk = 128

    x2d = x.reshape(b * s, d)
    mem_tm = math.gcd(b * mlen, 512)
    mkv = _mem_kv(mem.reshape(b * mlen, d), mem_norm_gain.reshape(1, d), w_mem_kv.astype(BF16), mem_tm)
    mkv = mkv.reshape(w_mem_kv.shape[0], b, mlen, 2 * MEM_W)

    cos, sin = _rope_tables(s)
    gq = jnp.tile(q_norm_a[0].astype(F32), 2).reshape(1, LANES)
    gk = jnp.tile(k_norm_a[0].astype(F32), 2).reshape(1, LANES)
    qa, ka, va, qb, kb, vb, qm, ga, gb, gm = _in_even(
        x2d, norm_gain[0].reshape(1, d), w_in_even[0].astype(BF16), gq, gk, cos, sin, s, tm)
    r3 = lambda a: a.reshape(b, s, a.shape[-1])
    ya = _global_attn(r3(qa), r3(ka), va, r3(ga), tq=256)
    group_b = B_HEADS // B_KV
    sink_cols = jnp.repeat(sink_b[0].astype(F32).reshape(B_KV, group_b) * LOG2E, blk,
                           axis=1).reshape(B_KV, 1, group_b * blk)
    yb = _window_attn(r3(qb), r3(kb), vb, _window_bias(rel_bias, blk), sink_cols, r3(gb), blk,
                      blocks_per_step=4)
    ym = _mem_attn(r3(qm), mkv, 0, r3(gm), tq=math.gcd(s, 1024))
    x1 = _out_proj([ya.reshape(b * s, -1), yb.reshape(b * s, -1), ym.reshape(b * s, -1)],
                   x2d, w_out_even[0].astype(BF16), tm)

    qc, kc, vc, qm1, gc, gm1 = _in_odd(x1, norm_gain[1].reshape(1, d), w_in_odd[0].astype(BF16), tm)
    yc = _nbr_attn(r3(qc), r3(kc), r3(vc), _nbr_bias(rpb_c[0]), r3(gc), rows_per_step=4, head_splits=2)
    ym1 = _mem_attn(r3(qm1), mkv, 1, r3(gm1), tq=math.gcd(s, 1024))
    out = _out_proj([yc.reshape(b * s, -1), ym1.reshape(b * s, -1)], x1, w_out_odd[0].astype(BF16), tm,
                    final_gain=final_norm_gain.reshape(1, d))
    return out.reshape(b, s, d)
```

```python
import functools
import math

import jax
import jax.numpy as jnp
import numpy as np
from jax import lax
from jax.experimental import pallas as pl
from jax.experimental.pallas import tpu as pltpu

GRID_W = 64
HEAD_DIM = 64
A_HEADS = 8
A_KV = 2
B_HEADS = 8
B_KV = 2
WINDOW = 128
C_HEADS = 16
NA_ROWS = 8
NA_COLS = 16
M_HEADS = 4
M_HEAD_DIM = 128
MEM_W = M_HEADS * M_HEAD_DIM
REL_BUCKETS = 32
REL_MAX_DIST = 128
ROPE_THETA = 10000.0
EPS = 1e-6

LANES = 128
NEG = -1e30
LOG2E = math.log2(math.e)
V_ONES_ROWS = 16
VMEM_LIMIT_BYTES = 56 * 1024 * 1024

F32 = jnp.float32
BF16 = jnp.bfloat16

_NT = (((1,), (1,)), ((), ()))


def _params(*sem):
    return pltpu.CompilerParams(dimension_semantics=sem, vmem_limit_bytes=VMEM_LIMIT_BYTES)


def _rms_rows(x, gain):
    ms = jnp.mean(x * x, axis=-1, keepdims=True)
    return x * lax.rsqrt(ms + EPS) * gain


def _lane_iota(shape):
    return lax.broadcasted_iota(jnp.int32, shape, len(shape) - 1)


def _silu(x):
    return x * (1.0 / (1.0 + jnp.exp(-x)))


def _half_ones():
    r = lax.broadcasted_iota(jnp.int32, (LANES, LANES), 0) // HEAD_DIM
    c = lax.broadcasted_iota(jnp.int32, (LANES, LANES), 1) // HEAD_DIM
    return jnp.where(r == c, 1.0, 0.0).astype(BF16)


def _head_norm_rope(t, gain, cos, sin_signed, ones_bd):
    ss = jnp.dot((t * t).astype(BF16), ones_bd, preferred_element_type=F32)
    tn = t * lax.rsqrt(ss * (1.0 / HEAD_DIM) + EPS) * gain
    lane = _lane_iota(tn.shape)
    quarter = HEAD_DIM // 4
    partner = jnp.where((lane % (2 * quarter)) < quarter,
                        pltpu.roll(tn, LANES - quarter, 1), pltpu.roll(tn, quarter, 1))
    return tn * cos + partner * sin_signed


def _store_padded_heads(q_tiles, out_ref, n_heads, n_kv):
    group = n_heads // n_kv
    for h in range(n_heads):
        t = q_tiles[h // 2]
        src_half = h % 2
        dst_half = (h // group) % 2
        if src_half != dst_half:
            t = pltpu.roll(t, HEAD_DIM, 1)
        lane = _lane_iota(t.shape)
        keep = (lane >= HEAD_DIM) if dst_half == 1 else (lane < HEAD_DIM)
        out_ref[:, h * LANES:(h + 1) * LANES] = jnp.where(keep, t, 0.0).astype(BF16)


def _mem_kv_kernel(mem_ref, g_ref, w_ref, o_ref):
    h = _rms_rows(mem_ref[...], g_ref[...]).astype(BF16)
    o_ref[...] = jnp.dot(h, w_ref[...], preferred_element_type=F32).astype(BF16)


def _mem_kv(mem2d, gain, w_bf16, tm):
    depth, d, n = w_bf16.shape
    rows = mem2d.shape[0]
    return pl.pallas_call(
        _mem_kv_kernel,
        out_shape=jax.ShapeDtypeStruct((depth, rows, n), BF16),
        grid=(depth, rows // tm),
        in_specs=[pl.BlockSpec((tm, d), lambda l, i: (i, 0)),
                  pl.BlockSpec((1, d), lambda l, i: (0, 0)),
                  pl.BlockSpec((None, d, n), lambda l, i: (l, 0, 0))],
        out_specs=pl.BlockSpec((None, tm, n), lambda l, i: (l, i, 0)),
        compiler_params=_params("arbitrary", "arbitrary"),
        name="mem_kv_proj",
    )(mem2d, gain, w_bf16)


def _in_even_kernel(x_ref, g_ref, w_ref, gq_ref, gk_ref, cos_ref, sin_ref, mkv_ref,
                    qa_ref, ka_ref, va_ref, qb_ref, kb_ref, vb_ref, ga_ref, gb_ref, ym_ref,
                    qm_scr, gm_scr):
    h = _rms_rows(x_ref[...], g_ref[...]).astype(BF16)

    def seg(lo, hi):
        return jnp.dot(h, w_ref[:, lo:hi], preferred_element_type=F32)

    ones_bd = _half_ones()
    scale = HEAD_DIM ** -0.5
    qa_w = A_HEADS * HEAD_DIM
    kva_w = A_KV * HEAD_DIM
    qb_w = B_HEADS * HEAD_DIM
    kvb_w = B_KV * HEAD_DIM
    offs = np.cumsum([0, qa_w, kva_w, kva_w, qb_w, kvb_w, kvb_w, MEM_W, qa_w, qb_w, MEM_W])
    o_qa, o_ka, o_va, o_qb, o_kb, o_vb, o_qm, o_ga, o_gb, o_gm = (int(v) for v in offs[:-1])

    def do_qa():
        zq = seg(o_qa, o_qa + qa_w)
        tiles = [_head_norm_rope(zq[:, j * LANES:(j + 1) * LANES], gq_ref[...], cos_ref[...],
                                 sin_ref[...], ones_bd) * (scale * LOG2E) for j in range(qa_w // LANES)]
        _store_padded_heads(tiles, qa_ref, A_HEADS, A_KV)

    def do_ka():
        ka_ref[...] = _head_norm_rope(seg(o_ka, o_ka + kva_w), gk_ref[...], cos_ref[...],
                                      sin_ref[...], ones_bd).astype(BF16)

    def do_va():
        vt = seg(o_va, o_va + kva_w).T.astype(BF16)
        for g in range(A_KV):
            va_ref[g, :HEAD_DIM] = vt[g * HEAD_DIM:(g + 1) * HEAD_DIM]
            va_ref[g, HEAD_DIM:] = jnp.ones((V_ONES_ROWS, vt.shape[1]), BF16)

    def do_qb():
        zq = seg(o_qb, o_qb + qb_w) * (scale * LOG2E)
        _store_padded_heads([zq[:, j * LANES:(j + 1) * LANES] for j in range(qb_w // LANES)],
                            qb_ref, B_HEADS, B_KV)

    def do_kb():
        kb_ref[...] = seg(o_kb, o_kb + kvb_w).astype(BF16)

    def do_vb():
        vt = seg(o_vb, o_vb + kvb_w).T.astype(BF16)
        for j in range(vb_ref.shape[0]):
            for g in range(B_KV):
                vb_ref[j, g, :HEAD_DIM] = vt[g * HEAD_DIM:(g + 1) * HEAD_DIM, j * WINDOW:(j + 1) * WINDOW]
                vb_ref[j, g, HEAD_DIM:] = jnp.ones((V_ONES_ROWS, WINDOW), BF16)

    def do_ga():
        ga_ref[...] = _silu(seg(o_ga, o_ga + qa_w)).astype(BF16)

    def do_gb():
        gb_ref[...] = _silu(seg(o_gb, o_gb + qb_w)).astype(BF16)

    qm_scr[...] = (seg(o_qm, o_qm + MEM_W) * (M_HEAD_DIM ** -0.5 * LOG2E)).astype(BF16)
    gm_scr[...] = _silu(seg(o_gm, o_gm + MEM_W)).astype(BF16)
    _mem_attn_items(qm_scr, mkv_ref, gm_scr, ym_ref,
                    between=[do_ga, do_gb, do_qa, do_qb, do_ka, do_kb, do_va, do_vb])


def _in_even(x2d, gain, w_bf16, gq, gk, cos, sin, mkv, seq, tm):
    n_tok, d = x2d.shape
    n_in = w_bf16.shape[1]
    per_seq = seq // tm
    mlen = mkv.shape[2]
    widths = [A_HEADS * LANES, A_KV * HEAD_DIM, A_KV * HEAD_DIM,
              B_HEADS * LANES, B_KV * HEAD_DIM, B_KV * HEAD_DIM,
              A_HEADS * HEAD_DIM, B_HEADS * HEAD_DIM, MEM_W]
    row = lambda i: (i, 0)
    const = lambda i: (0, 0)
    out_shape = [jax.ShapeDtypeStruct((n_tok, w), BF16) for w in widths]
    out_specs = [pl.BlockSpec((tm, w), row) for w in widths]
    vrows = HEAD_DIM + V_ONES_ROWS
    out_shape[2] = jax.ShapeDtypeStruct((n_tok // tm, A_KV, vrows, tm), BF16)
    out_specs[2] = pl.BlockSpec((None, A_KV, vrows, tm), lambda i: (i, 0, 0, 0))
    out_shape[5] = jax.ShapeDtypeStruct((n_tok // WINDOW, B_KV, vrows, WINDOW), BF16)
    out_specs[5] = pl.BlockSpec((tm // WINDOW, B_KV, vrows, WINDOW), lambda i: (i, 0, 0, 0))
    return pl.pallas_call(
        _in_even_kernel,
        out_shape=out_shape,
        grid=(n_tok // tm,),
        in_specs=[pl.BlockSpec((tm, d), row),
                  pl.BlockSpec((1, d), const),
                  pl.BlockSpec((d, n_in), const),
                  pl.BlockSpec((1, LANES), const),
                  pl.BlockSpec((1, LANES), const),
                  pl.BlockSpec((tm, LANES), lambda i: (i % per_seq, 0)),
                  pl.BlockSpec((tm, LANES), lambda i: (i % per_seq, 0)),
                  pl.BlockSpec((None, None, mlen, 2 * MEM_W), lambda i: (0, i // per_seq, 0, 0))],
        out_specs=out_specs,
        scratch_shapes=[pltpu.VMEM((tm, MEM_W), BF16), pltpu.VMEM((tm, MEM_W), BF16)],
        compiler_params=_params("arbitrary"),
        name="in_proj_even",
    )(x2d, gain, w_bf16, gq, gk, cos, sin, mkv)


def _mid_kernel(ya_ref, yb_ref, ym_ref, x_ref, wo_ref, g_ref, wi_ref, mkv_ref,
                x1_ref, qc_ref, kc_ref, vc_ref, gc_ref, ym1_ref, qm_scr, gm_scr):
    acc = x_ref[...]
    off = 0
    for y_ref in (ya_ref, yb_ref, ym_ref):
        width = y_ref.shape[1]
        acc = acc + jnp.dot(y_ref[...], wo_ref[off:off + width, :], preferred_element_type=F32)
        off += width
    x1_ref[...] = acc
    h = _rms_rows(acc, g_ref[...]).astype(BF16)

    def seg(lo, hi):
        return jnp.dot(h, wi_ref[:, lo:hi], preferred_element_type=F32)

    cw = C_HEADS * HEAD_DIM
    half = cw // 2
    o_qc, o_kc, o_vc, o_qm, o_gc, o_gm = 0, cw, 2 * cw, 3 * cw, 3 * cw + MEM_W, 4 * cw + MEM_W

    def thunk(ref, base, part, fn):
        def run():
            lo = part * half
            ref[:, lo:lo + half] = fn(seg(base + lo, base + lo + half)).astype(BF16)
        return run

    plain = lambda z: z
    to_q = lambda z: z * (HEAD_DIM ** -0.5 * LOG2E)
    segments = [thunk(ref, base, part, fn)
                for ref, base, fn in ((gc_ref, o_gc, _silu), (qc_ref, o_qc, to_q),
                                      (kc_ref, o_kc, plain), (vc_ref, o_vc, plain))
                for part in range(2)]
    qm_scr[...] = (seg(o_qm, o_qm + MEM_W) * (M_HEAD_DIM ** -0.5 * LOG2E)).astype(BF16)
    gm_scr[...] = _silu(seg(o_gm, o_gm + MEM_W)).astype(BF16)
    _mem_attn_items(qm_scr, mkv_ref, gm_scr, ym1_ref, between=segments)


def _mid(ya, yb, ym, x2d, wo_bf16, gain, wi_bf16, mkv, seq, tm):
    n_tok, d = x2d.shape
    per_seq = seq // tm
    mlen = mkv.shape[2]
    cw = C_HEADS * HEAD_DIM
    row = lambda i: (i, 0)
    const = lambda i: (0, 0)
    once = pl.Buffered(1)
    out_widths = [cw, cw, cw, cw, MEM_W]
    return pl.pallas_call(
        _mid_kernel,
        out_shape=[jax.ShapeDtypeStruct((n_tok, d), F32)]
                  + [jax.ShapeDtypeStruct((n_tok, w), BF16) for w in out_widths],
        grid=(n_tok // tm,),
        in_specs=[pl.BlockSpec((tm, ya.shape[1]), row),
                  pl.BlockSpec((tm, yb.shape[1]), row),
                  pl.BlockSpec((tm, ym.shape[1]), row),
                  pl.BlockSpec((tm, d), row),
                  pl.BlockSpec(wo_bf16.shape, const, pipeline_mode=once),
                  pl.BlockSpec((1, d), const),
                  pl.BlockSpec(wi_bf16.shape, const, pipeline_mode=once),
                  pl.BlockSpec((None, None, mlen, 2 * MEM_W), lambda i: (1, i // per_seq, 0, 0))],
        out_specs=[pl.BlockSpec((tm, d), row)] + [pl.BlockSpec((tm, w), row) for w in out_widths],
        scratch_shapes=[pltpu.VMEM((tm, MEM_W), BF16), pltpu.VMEM((tm, MEM_W), BF16)],
        compiler_params=_params("arbitrary"),
        name="out_proj_in_proj_odd",
    )(ya, yb, ym, x2d, wo_bf16, gain, wi_bf16, mkv)


def _global_attn_kernel(q_ref, k_ref, vt_ref, gate_ref, o_ref, m_scr, acc_scr, s_scr, mc_scr):
    tq = q_ref.shape[0]
    n_chunks = vt_ref.shape[0]
    tk = vt_ref.shape[3]
    group = A_HEADS // A_KV
    qs = [jnp.concatenate([q_ref[:, (g * group + i) * LANES:(g * group + i + 1) * LANES]
                           for i in range(group)], axis=0) for g in range(A_KV)]
    m_scr[...] = jnp.full(m_scr.shape, NEG, F32)
    acc_scr[...] = jnp.zeros(acc_scr.shape, F32)

    def scores(c, slot):
        start = pl.multiple_of(c * tk, tk)
        k = k_ref[pl.ds(start, tk), :]
        for g in range(A_KV):
            st = lax.dot_general(k, qs[g], _NT, preferred_element_type=F32)
            s_scr[slot, g] = st
            mc_scr[slot, g] = jnp.max(st, axis=0, keepdims=True)

    def accumulate(c, slot):
        for g in range(A_KV):
            m_old = m_scr[g]
            m_new = jnp.maximum(m_old, mc_scr[slot, g])
            alpha = jnp.exp2(m_old - m_new)
            pt = jnp.exp2(s_scr[slot, g] - m_new).astype(BF16)
            acc_scr[g] = alpha * acc_scr[g] + jnp.dot(vt_ref[c, g], pt, preferred_element_type=F32)
            m_scr[g] = m_new

    scores(0, 0)

    def body(c2, carry):
        c = 2 * c2
        scores(c + 1, 1)
        accumulate(c, 0)
        scores(c + 2, 0)
        accumulate(c + 1, 1)
        return carry

    lax.fori_loop(0, n_chunks // 2 - 1, body, 0)
    scores(n_chunks - 1, 1)
    accumulate(n_chunks - 2, 0)
    accumulate(n_chunks - 1, 1)

    ot = [acc_scr[g, :HEAD_DIM] * (1.0 / acc_scr[g, HEAD_DIM:HEAD_DIM + 1]) for g in range(A_KV)]
    for j in range(A_HEADS // 2):
        g, i0 = (2 * j) // group, (2 * j) % group
        tile_t = jnp.concatenate([ot[g][:, i0 * tq:(i0 + 1) * tq],
                                  ot[g][:, (i0 + 1) * tq:(i0 + 2) * tq]], axis=0)
        gate = gate_ref[:, j * LANES:(j + 1) * LANES].astype(F32)
        o_ref[:, j * LANES:(j + 1) * LANES] = (tile_t.T * gate).astype(BF16)


def _global_attn(qa, ka, vat, gate, tq):
    b, s, _ = qa.shape
    _, n_kv, vrows, tk = vat.shape
    n_chunks = s // tk
    assert n_chunks % 2 == 0
    rows = (A_HEADS // A_KV) * tq
    vat = vat.reshape(b, n_chunks, n_kv, vrows, tk)
    return pl.pallas_call(
        _global_attn_kernel,
        out_shape=jax.ShapeDtypeStruct((b, s, A_HEADS * HEAD_DIM), BF16),
        grid=(b, s // tq),
        in_specs=[pl.BlockSpec((None, tq, A_HEADS * LANES), lambda bi, i: (bi, i, 0)),
                  pl.BlockSpec((None, s, LANES), lambda bi, i: (bi, 0, 0)),
                  pl.BlockSpec((None, n_chunks, n_kv, vrows, tk), lambda bi, i: (bi, 0, 0, 0, 0)),
                  pl.BlockSpec((None, tq, A_HEADS * HEAD_DIM), lambda bi, i: (bi, i, 0))],
        out_specs=pl.BlockSpec((None, tq, A_HEADS * HEAD_DIM), lambda bi, i: (bi, i, 0)),
        scratch_shapes=[pltpu.VMEM((A_KV, 1, rows), F32),
                        pltpu.VMEM((A_KV, vrows, rows), F32),
                        pltpu.VMEM((2, A_KV, tk, rows), F32), pltpu.VMEM((2, A_KV, 1, rows), F32)],
        compiler_params=_params("arbitrary", "arbitrary"),
        name="global_attn",
    )(qa, ka, vat, gate)


def _window_attn_kernel(q_ref, k_ref, vt_ref, bias_ref, sink_ref, gate_ref, o_ref, *, blk):
    blocks_per_step = q_ref.shape[0] // blk
    nb = vt_ref.shape[0]
    step = pl.program_id(1)
    group = B_HEADS // B_KV

    def neighbours(t):
        i = step * blocks_per_step + t
        return jnp.maximum(i - 1, 0), i, jnp.minimum(i + 1, nb - 1)

    def scores(t, g):
        i = step * blocks_per_step + t
        rows = slice(t * blk, (t + 1) * blk)
        q = jnp.concatenate([q_ref[rows, (g * group + j) * LANES:(g * group + j + 1) * LANES]
                             for j in range(group)], axis=0)
        k = jnp.concatenate([k_ref[pl.ds(pl.multiple_of(n * blk, blk), blk), :]
                             for n in neighbours(t)], axis=0)
        st = lax.dot_general(k, q, _NT, preferred_element_type=F32) + bias_ref[g]
        if t == 0:
            st = jnp.concatenate([st[:blk] + jnp.where(i == 0, NEG, 0.0), st[blk:]], axis=0)
        if t == blocks_per_step - 1:
            st = jnp.concatenate([st[:2 * blk], st[2 * blk:] + jnp.where(i == nb - 1, NEG, 0.0)], axis=0)
        m = jnp.maximum(jnp.max(st, axis=0, keepdims=True), sink_ref[g])
        return st, m

    def probs(st, m):
        return jnp.exp2(st - m).astype(BF16), m

    def output(t, g, pt, m):
        vt = jnp.concatenate([vt_ref[n, g] for n in neighbours(t)], axis=1)
        ot = jnp.dot(vt, pt, preferred_element_type=F32)
        denom = ot[HEAD_DIM:HEAD_DIM + 1] + jnp.exp2(sink_ref[g] - m)
        return ot[:HEAD_DIM] * (1.0 / denom)

    def store(t, ot):
        rows = slice(t * blk, (t + 1) * blk)
        for j in range(B_HEADS // 2):
            g, i0 = (2 * j) // group, (2 * j) % group
            tile_t = jnp.concatenate([ot[g][:, i0 * blk:(i0 + 1) * blk],
                                      ot[g][:, (i0 + 1) * blk:(i0 + 2) * blk]], axis=0)
            gate = gate_ref[rows, j * LANES:(j + 1) * LANES].astype(F32)
            o_ref[rows, j * LANES:(j + 1) * LANES] = (tile_t.T * gate).astype(BF16)

    items = [(t, g) for t in range(blocks_per_step) for g in range(B_KV)]
    sm, pb, outs = {}, {}, {}
    for n in range(len(items) + 2):
        if n < len(items):
            sm[n] = scores(*items[n])
        if 1 <= n <= len(items):
            pb[n - 1] = probs(*sm.pop(n - 1))
        if n >= 2:
            t, g = items[n - 2]
            outs[g] = output(t, g, *pb.pop(n - 2))
            if g == B_KV - 1:
                store(t, outs)


def _window_attn(qb, kb, vbt, bias_t, sink_cols, gate, blk, blocks_per_step):
    b, s, _ = qb.shape
    nb = s // blk
    rows = (B_HEADS // B_KV) * blk
    tq = blk * blocks_per_step
    vrows = vbt.shape[2]
    vbt = vbt.reshape(b, nb, B_KV, vrows, blk)
    return pl.pallas_call(
        functools.partial(_window_attn_kernel, blk=blk),
        out_shape=jax.ShapeDtypeStruct((b, s, B_HEADS * HEAD_DIM), BF16),
        grid=(b, s // tq),
        in_specs=[pl.BlockSpec((None, tq, B_HEADS * LANES), lambda bi, i: (bi, i, 0)),
                  pl.BlockSpec((None, s, LANES), lambda bi, i: (bi, 0, 0)),
                  pl.BlockSpec((None, nb, B_KV, vrows, blk), lambda bi, i: (bi, 0, 0, 0, 0)),
                  pl.BlockSpec((B_KV, 3 * blk, rows), lambda bi, i: (0, 0, 0)),
                  pl.BlockSpec((B_KV, 1, rows), lambda bi, i: (0, 0, 0)),
                  pl.BlockSpec((None, tq, B_HEADS * HEAD_DIM), lambda bi, i: (bi, i, 0))],
        out_specs=pl.BlockSpec((None, tq, B_HEADS * HEAD_DIM), lambda bi, i: (bi, i, 0)),
        compiler_params=_params("arbitrary", "arbitrary"),
        name="window_attn",
    )(qb, kb, vbt, bias_t, sink_cols, gate)


def _mem_attn_items(q_ref, kv_ref, gate_ref, o_ref, between=(), sub=256):
    between = list(between)
    mlen = kv_ref.shape[0]
    ones = jnp.ones((mlen, M_HEAD_DIM), BF16)

    def scores(t, h):
        lo, hi = h * M_HEAD_DIM, (h + 1) * M_HEAD_DIM
        s = lax.dot_general(q_ref[t * sub:(t + 1) * sub, lo:hi], kv_ref[:, lo:hi], _NT,
                            preferred_element_type=F32)
        return s, jnp.max(s, axis=-1, keepdims=True)

    def probs(s, m):
        return jnp.exp2(s - m).astype(BF16)

    def output(t, h, p):
        lo, hi = h * M_HEAD_DIM, (h + 1) * M_HEAD_DIM
        v = jnp.concatenate([kv_ref[:, MEM_W + lo:MEM_W + hi], ones], axis=1)
        o = jnp.dot(p, v, preferred_element_type=F32)
        o = o[:, :M_HEAD_DIM] * (1.0 / o[:, M_HEAD_DIM:])
        rows = slice(t * sub, (t + 1) * sub)
        o_ref[rows, lo:hi] = (o * gate_ref[rows, lo:hi].astype(F32)).astype(BF16)

    items = [(t, h) for t in range(q_ref.shape[0] // sub) for h in range(M_HEADS)]
    sm, pb = {}, {}
    for n in range(len(items) + 2):
        if n < len(items):
            sm[n] = scores(*items[n])
        if 1 <= n <= len(items):
            pb[n - 1] = probs(*sm.pop(n - 1))
        if n >= 2:
            output(*items[n - 2], pb.pop(n - 2))
        if between:
            between.pop(0)()
    for run in between:
        run()


def _nbr_attn_kernel(q_ref, k_ref, v_ref, bias_ref, gate_ref, o_ref, *, grid_rows, rows_per_step):
    rb = pl.program_id(2)
    nkeys = NA_ROWS * GRID_W
    n_pairs = q_ref.shape[1] // LANES
    ones = jnp.ones((nkeys, LANES), BF16)
    starts, deltas = [], []
    for t in range(rows_per_step):
        r = rb * rows_per_step + t
        rs = jnp.clip(r - NA_ROWS // 2, 0, grid_rows - NA_ROWS)
        starts.append(pl.multiple_of(rs * GRID_W, GRID_W))
        deltas.append(r - rs)
    items = [(t, j) for t in range(rows_per_step) for j in range(n_pairs)]

    def scores(t, j):
        lo, hi = j * LANES, (j + 1) * LANES
        qt = q_ref[t * GRID_W:(t + 1) * GRID_W, lo:hi]
        lane = _lane_iota(qt.shape)
        zero = jnp.zeros_like(qt)
        q = jnp.concatenate([jnp.where(lane < HEAD_DIM, qt, zero),
                             jnp.where(lane >= HEAD_DIM, qt, zero)], axis=0)
        k = k_ref[pl.ds(starts[t], nkeys), lo:hi]
        s = lax.dot_general(q, k, _NT, preferred_element_type=F32) + bias_ref[deltas[t], j]
        return s, jnp.max(s, axis=-1, keepdims=True)

    def probs(s, m):
        return jnp.exp2(s - m).astype(BF16)

    def output(t, j, p):
        lo, hi = j * LANES, (j + 1) * LANES
        v = jnp.concatenate([v_ref[pl.ds(starts[t], nkeys), lo:hi], ones], axis=1)
        o = jnp.dot(p, v, preferred_element_type=F32)
        o = o[:, :LANES] * (1.0 / o[:, LANES:])
        lane_o = _lane_iota((GRID_W, LANES))
        tile = jnp.where(lane_o < HEAD_DIM, o[:GRID_W, :], o[GRID_W:, :])
        rows = slice(t * GRID_W, (t + 1) * GRID_W)
        o_ref[rows, lo:hi] = (tile * gate_ref[rows, lo:hi].astype(F32)).astype(BF16)

    sm = {}
    pb = {}
    for n in range(len(items) + 2):
        if n < len(items):
            sm[n] = scores(*items[n])
        if 1 <= n <= len(items):
            pb[n - 1] = probs(*sm.pop(n - 1))
        if n >= 2:
            output(*items[n - 2], pb.pop(n - 2))


def _nbr_attn(qc, kc, vc, bias, gate, rows_per_step, head_splits):
    b, s, w = qc.shape
    grid_rows = s // GRID_W
    nkeys = NA_ROWS * GRID_W
    wh = w // head_splits
    tq = rows_per_step * GRID_W
    blk = lambda hh, bi, rb: (bi, rb, hh)
    return pl.pallas_call(
        functools.partial(_nbr_attn_kernel, grid_rows=grid_rows, rows_per_step=rows_per_step),
        out_shape=jax.ShapeDtypeStruct((b, s, w), BF16),
        grid=(head_splits, b, grid_rows // rows_per_step),
        in_specs=[pl.BlockSpec((None, tq, wh), blk),
                  pl.BlockSpec((None, s, wh), lambda hh, bi, rb: (bi, 0, hh)),
                  pl.BlockSpec((None, s, wh), lambda hh, bi, rb: (bi, 0, hh)),
                  pl.BlockSpec((NA_ROWS, wh // LANES, 2 * GRID_W, nkeys), lambda hh, bi, rb: (0, hh, 0, 0)),
                  pl.BlockSpec((None, tq, wh), blk)],
        out_specs=pl.BlockSpec((None, tq, wh), blk),
        compiler_params=_params("arbitrary", "arbitrary", "arbitrary"),
        name="nbr_attn",
    )(qc, kc, vc, bias, gate)


def _out_final_kernel(yc_ref, ym_ref, x_ref, w_ref, g_ref, o_ref):
    acc = x_ref[...]
    off = 0
    for y_ref in (yc_ref, ym_ref):
        width = y_ref.shape[1]
        acc = acc + jnp.dot(y_ref[...], w_ref[off:off + width, :], preferred_element_type=F32)
        off += width
    o_ref[...] = _rms_rows(acc, g_ref[...])


def _out_final(yc, ym, x2d, w_bf16, final_gain, tm):
    n_tok, d = x2d.shape
    row = lambda i: (i, 0)
    const = lambda i: (0, 0)
    return pl.pallas_call(
        _out_final_kernel,
        out_shape=jax.ShapeDtypeStruct((n_tok, d), F32),
        grid=(n_tok // tm,),
        in_specs=[pl.BlockSpec((tm, yc.shape[1]), row),
                  pl.BlockSpec((tm, ym.shape[1]), row),
                  pl.BlockSpec((tm, d), row),
                  pl.BlockSpec(w_bf16.shape, const),
                  pl.BlockSpec((1, d), const)],
        out_specs=pl.BlockSpec((tm, d), row),
        compiler_params=_params("arbitrary"),
        name="out_proj_final",
    )(yc, ym, x2d, w_bf16, final_gain)


def _rope_tables(seq):
    quarter = HEAD_DIM // 4
    freqs = jnp.power(ROPE_THETA, -jnp.arange(quarter, dtype=F32) / quarter)
    t = jnp.arange(seq)
    ang_r = (t // GRID_W).astype(F32)[:, None] * freqs
    ang_c = (t % GRID_W).astype(F32)[:, None] * freqs
    cos_h = jnp.concatenate([jnp.cos(ang_r), jnp.cos(ang_r), jnp.cos(ang_c), jnp.cos(ang_c)], axis=-1)
    sin_h = jnp.concatenate([-jnp.sin(ang_r), jnp.sin(ang_r), -jnp.sin(ang_c), jnp.sin(ang_c)], axis=-1)
    return jnp.tile(cos_h, (1, 2)), jnp.tile(sin_h, (1, 2))


def _t5_bucket(rel):
    nb = REL_BUCKETS // 2
    max_exact = nb // 2
    ret = jnp.where(rel > 0, nb, 0)
    n = jnp.abs(rel)
    nf = jnp.maximum(n, 1).astype(F32)
    large = max_exact + (jnp.log(nf / max_exact) / math.log(REL_MAX_DIST / max_exact)
                         * (nb - max_exact)).astype(jnp.int32)
    large = jnp.minimum(large, nb - 1)
    return ret + jnp.where(n < max_exact, n, large)


def _window_bias(rel_bias, blk):
    span = blk + 2 * WINDOW
    rel = (jnp.arange(span) - WINDOW)[None, :] - jnp.arange(blk)[:, None]
    band = jnp.abs(rel) <= WINDOW
    onehot = (_t5_bucket(rel)[None] == jnp.arange(REL_BUCKETS)[:, None, None]).astype(F32)
    bias = jnp.einsum('bh,bqs->shq', rel_bias.astype(F32) * LOG2E, onehot,
                      precision=lax.Precision.HIGHEST)
    bias = jnp.where(band.T[:, None, :], bias, NEG)
    group = B_HEADS // B_KV
    return bias.reshape(span, B_KV, group * blk).transpose(1, 0, 2)


def _nbr_bias(rpb):
    col = np.arange(GRID_W)
    cs = np.clip(col - NA_COLS // 2, 0, GRID_W - NA_COLS)
    colmask = (col[None, :] >= cs[:, None]) & (col[None, :] < cs[:, None] + NA_COLS)
    dc = np.clip(col[None, :] - col[:, None] + NA_COLS - 1, 0, 2 * NA_COLS - 2)
    onehot = (dc[None] == np.arange(2 * NA_COLS - 1)[:, None, None]) & colmask[None]
    t = jnp.einsum('hrc,cqk->hqrk', rpb.astype(F32) * LOG2E, jnp.asarray(onehot, F32),
                   precision=lax.Precision.HIGHEST)
    t = t + jnp.asarray(np.where(colmask, 0.0, NEG), F32)[None, :, None, :]
    per_delta = [t[:, :, NA_ROWS - 1 - dl:2 * NA_ROWS - 1 - dl, :] for dl in range(NA_ROWS)]
    return jnp.stack(per_delta).reshape(NA_ROWS, C_HEADS // 2, 2 * GRID_W, NA_ROWS * GRID_W)


def kernel(x, mem, norm_gain, mem_norm_gain, w_in_even, w_out_even, q_norm_a, k_norm_a, sink_b,
           rel_bias, w_in_odd, w_out_odd, rpb_c, w_mem_kv, final_norm_gain):
    b, s, d = x.shape
    mlen = mem.shape[1]
    assert s % GRID_W == 0 and s // GRID_W >= NA_ROWS and s % 512 == 0
    tm = 512
    blk = 128

    x2d = x.reshape(b * s, d)
    mem_tm = math.gcd(b * mlen, 512)
    mkv = _mem_kv(mem.reshape(b * mlen, d), mem_norm_gain.reshape(1, d), w_mem_kv.astype(BF16), mem_tm)
    mkv = mkv.reshape(w_mem_kv.shape[0], b, mlen, 2 * MEM_W)

    cos, sin = _rope_tables(s)
    gq = jnp.tile(q_norm_a[0].astype(F32), 2).reshape(1, LANES)
    gk = jnp.tile(k_norm_a[0].astype(F32), 2).reshape(1, LANES)
    qa, ka, va, qb, kb, vb, ga, gb, ym = _in_even(
        x2d, norm_gain[0].reshape(1, d), w_in_even[0].astype(BF16), gq, gk, cos, sin, mkv, s, tm)
    r3 = lambda a: a.reshape(b, s, a.shape[-1])
    ya = _global_attn(r3(qa), r3(ka), va, r3(ga), tq=256)
    group_b = B_HEADS // B_KV
    sink_cols = jnp.repeat(sink_b[0].astype(F32).reshape(B_KV, group_b) * LOG2E, blk,
                           axis=1).reshape(B_KV, 1, group_b * blk)
    yb = _window_attn(r3(qb), r3(kb), vb, _window_bias(rel_bias, blk), sink_cols, r3(gb), blk,
                      blocks_per_step=4)

    x1, qc, kc, vc, gc, ym1 = _mid(ya.reshape(b * s, -1), yb.reshape(b * s, -1), ym, x2d,
                                   w_out_even[0].astype(BF16), norm_gain[1].reshape(1, d),
                                   w_in_odd[0].astype(BF16), mkv, s, tm)
    yc = _nbr_attn(r3(qc), r3(kc), r3(vc), _nbr_bias(rpb_c[0]), r3(gc), rows_per_step=4, head_splits=2)
    out = _out_final(yc.reshape(b * s, -1), ym1, x1, w_out_odd[0].astype(BF16),
                     final_norm_gain.reshape(1, d), tm)
    return out.reshape(b, s, d)
```

```python
import functools
import math

import jax
import jax.numpy as jnp
import numpy as np
from jax import lax
from jax.experimental import pallas as pl
from jax.experimental.pallas import tpu as pltpu

GRID_W = 64
HEAD_DIM = 64
A_HEADS = 8
A_KV = 2
B_HEADS = 8
B_KV = 2
WINDOW = 128
C_HEADS = 16
NA_ROWS = 8
NA_COLS = 16
M_HEADS = 4
M_HEAD_DIM = 128
MEM_W = M_HEADS * M_HEAD_DIM
REL_BUCKETS = 32
REL_MAX_DIST = 128
ROPE_THETA = 10000.0
EPS = 1e-6

LANES = 128
NEG = -1e30
LOG2E = math.log2(math.e)
V_ONES_ROWS = 16
VMEM_LIMIT_BYTES = 56 * 1024 * 1024

F32 = jnp.float32
BF16 = jnp.bfloat16

_NT = (((1,), (1,)), ((), ()))


def _params(*sem):
    return pltpu.CompilerParams(dimension_semantics=sem, vmem_limit_bytes=VMEM_LIMIT_BYTES)


def _rms_rows(x, gain):
    ms = jnp.mean(x * x, axis=-1, keepdims=True)
    return x * lax.rsqrt(ms + EPS) * gain


def _lane_iota(shape):
    return lax.broadcasted_iota(jnp.int32, shape, len(shape) - 1)


def _silu(x):
    return x * (1.0 / (1.0 + jnp.exp(-x)))


def _half_ones():
    r = lax.broadcasted_iota(jnp.int32, (LANES, LANES), 0) // HEAD_DIM
    c = lax.broadcasted_iota(jnp.int32, (LANES, LANES), 1) // HEAD_DIM
    return jnp.where(r == c, 1.0, 0.0).astype(BF16)


def _head_norm_rope(t, gain, cos, sin_signed, ones_bd):
    ss = jnp.dot((t * t).astype(BF16), ones_bd, preferred_element_type=F32)
    tn = t * lax.rsqrt(ss * (1.0 / HEAD_DIM) + EPS) * gain
    lane = _lane_iota(tn.shape)
    quarter = HEAD_DIM // 4
    partner = jnp.where((lane % (2 * quarter)) < quarter,
                        pltpu.roll(tn, LANES - quarter, 1), pltpu.roll(tn, quarter, 1))
    return tn * cos + partner * sin_signed


def _store_padded_heads(q_tiles, out_ref, n_heads, n_kv):
    group = n_heads // n_kv
    for h in range(n_heads):
        t = q_tiles[h // 2]
        src_half = h % 2
        dst_half = (h // group) % 2
        if src_half != dst_half:
            t = pltpu.roll(t, HEAD_DIM, 1)
        lane = _lane_iota(t.shape)
        keep = (lane >= HEAD_DIM) if dst_half == 1 else (lane < HEAD_DIM)
        out_ref[:, h * LANES:(h + 1) * LANES] = jnp.where(keep, t, 0.0).astype(BF16)


def _mem_kv_kernel(mem_ref, g_ref, w_ref, o_ref):
    h = _rms_rows(mem_ref[...], g_ref[...]).astype(BF16)
    o_ref[...] = jnp.dot(h, w_ref[...], preferred_element_type=F32).astype(BF16)


def _mem_kv(mem2d, gain, w_bf16, tm):
    depth, d, n = w_bf16.shape
    rows = mem2d.shape[0]
    return pl.pallas_call(
        _mem_kv_kernel,
        out_shape=jax.ShapeDtypeStruct((depth, rows, n), BF16),
        grid=(depth, rows // tm),
        in_specs=[pl.BlockSpec((tm, d), lambda l, i: (i, 0)),
                  pl.BlockSpec((1, d), lambda l, i: (0, 0)),
                  pl.BlockSpec((None, d, n), lambda l, i: (l, 0, 0))],
        out_specs=pl.BlockSpec((None, tm, n), lambda l, i: (l, i, 0)),
        compiler_params=_params("arbitrary", "arbitrary"),
        name="mem_kv_proj",
    )(mem2d, gain, w_bf16)


def _in_even_kernel(x_ref, g_ref, w_ref, gq_ref, gk_ref, cos_ref, sin_ref, mkv_ref,
                    qa_ref, ka_ref, va_ref, qb_ref, kb_ref, vb_ref, ga_ref, gb_ref, ym_ref,
                    qm_scr, gm_scr):
    h = _rms_rows(x_ref[...], g_ref[...]).astype(BF16)

    def seg(lo, hi):
        return jnp.dot(h, w_ref[:, lo:hi], preferred_element_type=F32)

    ones_bd = _half_ones()
    scale = HEAD_DIM ** -0.5
    qa_w = A_HEADS * HEAD_DIM
    kva_w = A_KV * HEAD_DIM
    qb_w = B_HEADS * HEAD_DIM
    kvb_w = B_KV * HEAD_DIM
    offs = np.cumsum([0, qa_w, kva_w, kva_w, qb_w, kvb_w, kvb_w, MEM_W, qa_w, qb_w, MEM_W])
    o_qa, o_ka, o_va, o_qb, o_kb, o_vb, o_qm, o_ga, o_gb, o_gm = (int(v) for v in offs[:-1])

    def do_qa():
        zq = seg(o_qa, o_qa + qa_w)
        tiles = [_head_norm_rope(zq[:, j * LANES:(j + 1) * LANES], gq_ref[...], cos_ref[...],
                                 sin_ref[...], ones_bd) * (scale * LOG2E) for j in range(qa_w // LANES)]
        _store_padded_heads(tiles, qa_ref, A_HEADS, A_KV)

    def do_ka():
        ka_ref[...] = _head_norm_rope(seg(o_ka, o_ka + kva_w), gk_ref[...], cos_ref[...],
                                      sin_ref[...], ones_bd).astype(BF16)

    def do_va():
        vt = seg(o_va, o_va + kva_w).T.astype(BF16)
        for g in range(A_KV):
            va_ref[g, :HEAD_DIM] = vt[g * HEAD_DIM:(g + 1) * HEAD_DIM]
            va_ref[g, HEAD_DIM:] = jnp.ones((V_ONES_ROWS, vt.shape[1]), BF16)

    def do_qb():
        zq = seg(o_qb, o_qb + qb_w) * (scale * LOG2E)
        _store_padded_heads([zq[:, j * LANES:(j + 1) * LANES] for j in range(qb_w // LANES)],
                            qb_ref, B_HEADS, B_KV)

    def do_kb():
        kb_ref[...] = seg(o_kb, o_kb + kvb_w).astype(BF16)

    def do_vb():
        vt = seg(o_vb, o_vb + kvb_w).T.astype(BF16)
        for j in range(vb_ref.shape[0]):
            for g in range(B_KV):
                vb_ref[j, g, :HEAD_DIM] = vt[g * HEAD_DIM:(g + 1) * HEAD_DIM, j * WINDOW:(j + 1) * WINDOW]
                vb_ref[j, g, HEAD_DIM:] = jnp.ones((V_ONES_ROWS, WINDOW), BF16)

    def do_ga():
        ga_ref[...] = _silu(seg(o_ga, o_ga + qa_w)).astype(BF16)

    def do_gb():
        gb_ref[...] = _silu(seg(o_gb, o_gb + qb_w)).astype(BF16)

    qm_scr[...] = (seg(o_qm, o_qm + MEM_W) * (M_HEAD_DIM ** -0.5 * LOG2E)).astype(BF16)
    gm_scr[...] = _silu(seg(o_gm, o_gm + MEM_W)).astype(BF16)
    _mem_attn_items(qm_scr, mkv_ref, gm_scr, ym_ref,
                    between=[do_ga, do_gb, do_qa, do_qb, do_ka, do_kb, do_va, do_vb])


def _in_even(x2d, gain, w_bf16, gq, gk, cos, sin, mkv, seq, tm):
    n_tok, d = x2d.shape
    n_in = w_bf16.shape[1]
    per_seq = seq // tm
    mlen = mkv.shape[2]
    widths = [A_HEADS * LANES, A_KV * HEAD_DIM, A_KV * HEAD_DIM,
              B_HEADS * LANES, B_KV * HEAD_DIM, B_KV * HEAD_DIM,
              A_HEADS * HEAD_DIM, B_HEADS * HEAD_DIM, MEM_W]
    row = lambda i: (i, 0)
    const = lambda i: (0, 0)
    out_shape = [jax.ShapeDtypeStruct((n_tok, w), BF16) for w in widths]
    out_specs = [pl.BlockSpec((tm, w), row) for w in widths]
    vrows = HEAD_DIM + V_ONES_ROWS
    out_shape[2] = jax.ShapeDtypeStruct((n_tok // tm, A_KV, vrows, tm), BF16)
    out_specs[2] = pl.BlockSpec((None, A_KV, vrows, tm), lambda i: (i, 0, 0, 0))
    out_shape[5] = jax.ShapeDtypeStruct((n_tok // WINDOW, B_KV, vrows, WINDOW), BF16)
    out_specs[5] = pl.BlockSpec((tm // WINDOW, B_KV, vrows, WINDOW), lambda i: (i, 0, 0, 0))
    return pl.pallas_call(
        _in_even_kernel,
        out_shape=out_shape,
        grid=(n_tok // tm,),
        in_specs=[pl.BlockSpec((tm, d), row),
                  pl.BlockSpec((1, d), const),
                  pl.BlockSpec((d, n_in), const),
                  pl.BlockSpec((1, LANES), const),
                  pl.BlockSpec((1, LANES), const),
                  pl.BlockSpec((tm, LANES), lambda i: (i % per_seq, 0)),
                  pl.BlockSpec((tm, LANES), lambda i: (i % per_seq, 0)),
                  pl.BlockSpec((None, None, mlen, 2 * MEM_W), lambda i: (0, i // per_seq, 0, 0))],
        out_specs=out_specs,
        scratch_shapes=[pltpu.VMEM((tm, MEM_W), BF16), pltpu.VMEM((tm, MEM_W), BF16)],
        compiler_params=_params("arbitrary"),
        name="in_proj_even",
    )(x2d, gain, w_bf16, gq, gk, cos, sin, mkv)


def _mid_kernel(ya_ref, yb_ref, ym_ref, x_ref, wo_ref, g_ref, wi_ref, mkv_ref,
                x1_ref, qc_ref, kc_ref, vc_ref, gc_ref, ym1_ref, qm_scr, gm_scr):
    acc = x_ref[...]
    off = 0
    for y_ref in (ya_ref, yb_ref, ym_ref):
        width = y_ref.shape[1]
        acc = acc + jnp.dot(y_ref[...], wo_ref[off:off + width, :], preferred_element_type=F32)
        off += width
    x1_ref[...] = acc
    h = _rms_rows(acc, g_ref[...]).astype(BF16)

    def seg(lo, hi):
        return jnp.dot(h, wi_ref[:, lo:hi], preferred_element_type=F32)

    cw = C_HEADS * HEAD_DIM
    half = cw // 2
    o_qc, o_kc, o_vc, o_qm, o_gc, o_gm = 0, cw, 2 * cw, 3 * cw, 3 * cw + MEM_W, 4 * cw + MEM_W

    def thunk(ref, base, part, fn):
        def run():
            lo = part * half
            ref[:, lo:lo + half] = fn(seg(base + lo, base + lo + half)).astype(BF16)
        return run

    plain = lambda z: z
    to_q = lambda z: z * (HEAD_DIM ** -0.5 * LOG2E)
    segments = [thunk(ref, base, part, fn)
                for ref, base, fn in ((gc_ref, o_gc, _silu), (qc_ref, o_qc, to_q),
                                      (kc_ref, o_kc, plain), (vc_ref, o_vc, plain))
                for part in range(2)]
    qm_scr[...] = (seg(o_qm, o_qm + MEM_W) * (M_HEAD_DIM ** -0.5 * LOG2E)).astype(BF16)
    gm_scr[...] = _silu(seg(o_gm, o_gm + MEM_W)).astype(BF16)
    _mem_attn_items(qm_scr, mkv_ref, gm_scr, ym1_ref, between=segments)


def _mid(ya, yb, ym, x2d, wo_bf16, gain, wi_bf16, mkv, seq, tm):
    n_tok, d = x2d.shape
    per_seq = seq // tm
    mlen = mkv.shape[2]
    cw = C_HEADS * HEAD_DIM
    row = lambda i: (i, 0)
    const = lambda i: (0, 0)
    once = pl.Buffered(1)
    out_widths = [cw, cw, cw, cw, MEM_W]
    return pl.pallas_call(
        _mid_kernel,
        out_shape=[jax.ShapeDtypeStruct((n_tok, d), F32)]
                  + [jax.ShapeDtypeStruct((n_tok, w), BF16) for w in out_widths],
        grid=(n_tok // tm,),
        in_specs=[pl.BlockSpec((tm, ya.shape[1]), row),
                  pl.BlockSpec((tm, yb.shape[1]), row),
                  pl.BlockSpec((tm, ym.shape[1]), row),
                  pl.BlockSpec((tm, d), row),
                  pl.BlockSpec(wo_bf16.shape, const, pipeline_mode=once),
                  pl.BlockSpec((1, d), const),
                  pl.BlockSpec(wi_bf16.shape, const, pipeline_mode=once),
                  pl.BlockSpec((None, None, mlen, 2 * MEM_W), lambda i: (1, i // per_seq, 0, 0))],
        out_specs=[pl.BlockSpec((tm, d), row)] + [pl.BlockSpec((tm, w), row) for w in out_widths],
        scratch_shapes=[pltpu.VMEM((tm, MEM_W), BF16), pltpu.VMEM((tm, MEM_W), BF16)],
        compiler_params=_params("arbitrary"),
        name="out_proj_in_proj_odd",
    )(ya, yb, ym, x2d, wo_bf16, gain, wi_bf16, mkv)


def _global_attn_kernel(q_ref, k_ref, vt_ref, gate_ref, o_ref, m_scr, acc_scr, s_scr, mc_scr):
    tq = q_ref.shape[0]
    n_chunks = vt_ref.shape[0]
    tk = vt_ref.shape[3]
    group = A_HEADS // A_KV
    qs = [jnp.concatenate([q_ref[:, (g * group + i) * LANES:(g * group + i + 1) * LANES]
                           for i in range(group)], axis=0) for g in range(A_KV)]
    m_scr[...] = jnp.full(m_scr.shape, NEG, F32)
    acc_scr[...] = jnp.zeros(acc_scr.shape, F32)

    parts = 2
    tp = tk // parts
    pieces = [(p, g) for p in range(parts) for g in range(A_KV)]

    def scores_piece(c, slot, p, g):
        start = pl.multiple_of(c * tk + p * tp, tp)
        st = lax.dot_general(k_ref[pl.ds(start, tp), :], qs[g], _NT,
                             preferred_element_type=F32)
        s_scr[slot, g, p * tp:(p + 1) * tp] = st
        mx = jnp.max(st, axis=0, keepdims=True)
        mc_scr[slot, g] = mx if p == 0 else jnp.maximum(mc_scr[slot, g], mx)

    def accumulate_piece(c, slot, p, g):
        m_old = m_scr[g]
        m_new = jnp.maximum(m_old, mc_scr[slot, g])
        pt = jnp.exp2(s_scr[slot, g, p * tp:(p + 1) * tp] - m_new).astype(BF16)
        pv = jnp.dot(vt_ref[c, g][:, p * tp:(p + 1) * tp], pt, preferred_element_type=F32)
        if p == 0:
            acc_scr[g] = jnp.exp2(m_old - m_new) * acc_scr[g] + pv
        else:
            acc_scr[g] = acc_scr[g] + pv
        if p == parts - 1:
            m_scr[g] = m_new

    def scores(c, slot):
        for p, g in pieces:
            scores_piece(c, slot, p, g)

    def accumulate(c, slot):
        for p, g in pieces:
            accumulate_piece(c, slot, p, g)

    def scores_and_accumulate(c_s, slot_s, c_a, slot_a):
        for p, g in pieces:
            scores_piece(c_s, slot_s, p, g)
            accumulate_piece(c_a, slot_a, p, g)

    scores(0, 0)

    def body(c2, carry):
        c = 2 * c2
        scores_and_accumulate(c + 1, 1, c, 0)
        scores_and_accumulate(c + 2, 0, c + 1, 1)
        return carry

    lax.fori_loop(0, n_chunks // 2 - 1, body, 0)
    scores_and_accumulate(n_chunks - 1, 1, n_chunks - 2, 0)
    accumulate(n_chunks - 1, 1)

    ot = [acc_scr[g, :HEAD_DIM] * (1.0 / acc_scr[g, HEAD_DIM:HEAD_DIM + 1]) for g in range(A_KV)]
    for j in range(A_HEADS // 2):
        g, i0 = (2 * j) // group, (2 * j) % group
        tile_t = jnp.concatenate([ot[g][:, i0 * tq:(i0 + 1) * tq],
                                  ot[g][:, (i0 + 1) * tq:(i0 + 2) * tq]], axis=0)
        gate = gate_ref[:, j * LANES:(j + 1) * LANES].astype(F32)
        o_ref[:, j * LANES:(j + 1) * LANES] = (tile_t.T * gate).astype(BF16)


def _global_attn(qa, ka, vat, gate, tq):
    b, s, _ = qa.shape
    _, n_kv, vrows, tk = vat.shape
    n_chunks = s // tk
    assert n_chunks % 2 == 0
    rows = (A_HEADS // A_KV) * tq
    vat = vat.reshape(b, n_chunks, n_kv, vrows, tk)
    return pl.pallas_call(
        _global_attn_kernel,
        out_shape=jax.ShapeDtypeStruct((b, s, A_HEADS * HEAD_DIM), BF16),
        grid=(b, s // tq),
        in_specs=[pl.BlockSpec((None, tq, A_HEADS * LANES), lambda bi, i: (bi, i, 0)),
                  pl.BlockSpec((None, s, LANES), lambda bi, i: (bi, 0, 0)),
                  pl.BlockSpec((None, n_chunks, n_kv, vrows, tk), lambda bi, i: (bi, 0, 0, 0, 0)),
                  pl.BlockSpec((None, tq, A_HEADS * HEAD_DIM), lambda bi, i: (bi, i, 0))],
        out_specs=pl.BlockSpec((None, tq, A_HEADS * HEAD_DIM), lambda bi, i: (bi, i, 0)),
        scratch_shapes=[pltpu.VMEM((A_KV, 1, rows), F32),
                        pltpu.VMEM((A_KV, vrows, rows), F32),
                        pltpu.VMEM((2, A_KV, tk, rows), F32), pltpu.VMEM((2, A_KV, 1, rows), F32)],
        compiler_params=_params("arbitrary", "arbitrary"),
        name="global_attn",
    )(qa, ka, vat, gate)


def _window_attn_kernel(q_ref, k_ref, vt_ref, bias_ref, sink_ref, gate_ref, o_ref, *, blk):
    blocks_per_step = q_ref.shape[0] // blk
    nb = vt_ref.shape[0]
    step = pl.program_id(1)
    group = B_HEADS // B_KV

    def neighbours(t):
        i = step * blocks_per_step + t
        return jnp.maximum(i - 1, 0), i, jnp.minimum(i + 1, nb - 1)

    def scores(t, g):
        i = step * blocks_per_step + t
        rows = slice(t * blk, (t + 1) * blk)
        q = jnp.concatenate([q_ref[rows, (g * group + j) * LANES:(g * group + j + 1) * LANES]
                             for j in range(group)], axis=0)
        k = jnp.concatenate([k_ref[pl.ds(pl.multiple_of(n * blk, blk), blk), :]
                             for n in neighbours(t)], axis=0)
        st = lax.dot_general(k, q, _NT, preferred_element_type=F32) + bias_ref[g]
        if t == 0:
            st = jnp.concatenate([st[:blk] + jnp.where(i == 0, NEG, 0.0), st[blk:]], axis=0)
        if t == blocks_per_step - 1:
            st = jnp.concatenate([st[:2 * blk], st[2 * blk:] + jnp.where(i == nb - 1, NEG, 0.0)], axis=0)
        m = jnp.maximum(jnp.max(st, axis=0, keepdims=True), sink_ref[g])
        return st, m

    def probs(st, m):
        return jnp.exp2(st - m).astype(BF16), m

    def output(t, g, pt, m):
        vt = jnp.concatenate([vt_ref[n, g] for n in neighbours(t)], axis=1)
        ot = jnp.dot(vt, pt, preferred_element_type=F32)
        denom = ot[HEAD_DIM:HEAD_DIM + 1] + jnp.exp2(sink_ref[g] - m)
        return ot[:HEAD_DIM] * (1.0 / denom)

    def store(t, ot):
        rows = slice(t * blk, (t + 1) * blk)
        for j in range(B_HEADS // 2):
            g, i0 = (2 * j) // group, (2 * j) % group
            tile_t = jnp.concatenate([ot[g][:, i0 * blk:(i0 + 1) * blk],
                                      ot[g][:, (i0 + 1) * blk:(i0 + 2) * blk]], axis=0)
            gate = gate_ref[rows, j * LANES:(j + 1) * LANES].astype(F32)
            o_ref[rows, j * LANES:(j + 1) * LANES] = (tile_t.T * gate).astype(BF16)

    items = [(t, g) for t in range(blocks_per_step) for g in range(B_KV)]
    sm, pb, outs = {}, {}, {}
    for n in range(len(items) + 2):
        if n < len(items):
            sm[n] = scores(*items[n])
        if 1 <= n <= len(items):
            pb[n - 1] = probs(*sm.pop(n - 1))
        if n >= 2:
            t, g = items[n - 2]
            outs[g] = output(t, g, *pb.pop(n - 2))
            if g == B_KV - 1:
                store(t, outs)


def _window_attn(qb, kb, vbt, bias_t, sink_cols, gate, blk, blocks_per_step):
    b, s, _ = qb.shape
    nb = s // blk
    rows = (B_HEADS // B_KV) * blk
    tq = blk * blocks_per_step
    vrows = vbt.shape[2]
    vbt = vbt.reshape(b, nb, B_KV, vrows, blk)
    return pl.pallas_call(
        functools.partial(_window_attn_kernel, blk=blk),
        out_shape=jax.ShapeDtypeStruct((b, s, B_HEADS * HEAD_DIM), BF16),
        grid=(b, s // tq),
        in_specs=[pl.BlockSpec((None, tq, B_HEADS * LANES), lambda bi, i: (bi, i, 0)),
                  pl.BlockSpec((None, s, LANES), lambda bi, i: (bi, 0, 0)),
                  pl.BlockSpec((None, nb, B_KV, vrows, blk), lambda bi, i: (bi, 0, 0, 0, 0)),
                  pl.BlockSpec((B_KV, 3 * blk, rows), lambda bi, i: (0, 0, 0)),
                  pl.BlockSpec((B_KV, 1, rows), lambda bi, i: (0, 0, 0)),
                  pl.BlockSpec((None, tq, B_HEADS * HEAD_DIM), lambda bi, i: (bi, i, 0))],
        out_specs=pl.BlockSpec((None, tq, B_HEADS * HEAD_DIM), lambda bi, i: (bi, i, 0)),
        compiler_params=_params("arbitrary", "arbitrary"),
        name="window_attn",
    )(qb, kb, vbt, bias_t, sink_cols, gate)


def _mem_attn_items(q_ref, kv_ref, gate_ref, o_ref, between=(), sub=256):
    between = list(between)
    mlen = kv_ref.shape[0]
    ones = jnp.ones((mlen, M_HEAD_DIM), BF16)

    def scores(t, h):
        lo, hi = h * M_HEAD_DIM, (h + 1) * M_HEAD_DIM
        s = lax.dot_general(q_ref[t * sub:(t + 1) * sub, lo:hi], kv_ref[:, lo:hi], _NT,
                            preferred_element_type=F32)
        return s, jnp.max(s, axis=-1, keepdims=True)

    def probs(s, m):
        return jnp.exp2(s - m).astype(BF16)

    def output(t, h, p):
        lo, hi = h * M_HEAD_DIM, (h + 1) * M_HEAD_DIM
        v = jnp.concatenate([kv_ref[:, MEM_W + lo:MEM_W + hi], ones], axis=1)
        o = jnp.dot(p, v, preferred_element_type=F32)
        o = o[:, :M_HEAD_DIM] * (1.0 / o[:, M_HEAD_DIM:])
        rows = slice(t * sub, (t + 1) * sub)
        o_ref[rows, lo:hi] = (o * gate_ref[rows, lo:hi].astype(F32)).astype(BF16)

    items = [(t, h) for t in range(q_ref.shape[0] // sub) for h in range(M_HEADS)]
    sm, pb = {}, {}
    for n in range(len(items) + 2):
        if n < len(items):
            sm[n] = scores(*items[n])
        if 1 <= n <= len(items):
            pb[n - 1] = probs(*sm.pop(n - 1))
        if n >= 2:
            output(*items[n - 2], pb.pop(n - 2))
        if between:
            between.pop(0)()
    for run in between:
        run()


def _nbr_attn_kernel(q_ref, k_ref, v_ref, bias_ref, gate_ref, o_ref, *, grid_rows, rows_per_step):
    rb = pl.program_id(2)
    nkeys = NA_ROWS * GRID_W
    n_pairs = q_ref.shape[1] // LANES
    ones = jnp.ones((nkeys, LANES), BF16)
    starts, deltas = [], []
    for t in range(rows_per_step):
        r = rb * rows_per_step + t
        rs = jnp.clip(r - NA_ROWS // 2, 0, grid_rows - NA_ROWS)
        starts.append(pl.multiple_of(rs * GRID_W, GRID_W))
        deltas.append(r - rs)
    items = [(t, j) for t in range(rows_per_step) for j in range(n_pairs)]

    def scores(t, j):
        lo, hi = j * LANES, (j + 1) * LANES
        qt = q_ref[t * GRID_W:(t + 1) * GRID_W, lo:hi]
        lane = _lane_iota(qt.shape)
        zero = jnp.zeros_like(qt)
        q = jnp.concatenate([jnp.where(lane < HEAD_DIM, qt, zero),
                             jnp.where(lane >= HEAD_DIM, qt, zero)], axis=0)
        k = k_ref[pl.ds(starts[t], nkeys), lo:hi]
        s = lax.dot_general(q, k, _NT, preferred_element_type=F32) + bias_ref[deltas[t], j]
        return s, jnp.max(s, axis=-1, keepdims=True)

    def probs(s, m):
        return jnp.exp2(s - m).astype(BF16)

    def output(t, j, p):
        lo, hi = j * LANES, (j + 1) * LANES
        v = jnp.concatenate([v_ref[pl.ds(starts[t], nkeys), lo:hi], ones], axis=1)
        o = jnp.dot(p, v, preferred_element_type=F32)
        o = o[:, :LANES] * (1.0 / o[:, LANES:])
        lane_o = _lane_iota((GRID_W, LANES))
        tile = jnp.where(lane_o < HEAD_DIM, o[:GRID_W, :], o[GRID_W:, :])
        rows = slice(t * GRID_W, (t + 1) * GRID_W)
        o_ref[rows, lo:hi] = (tile * gate_ref[rows, lo:hi].astype(F32)).astype(BF16)

    sm = {}
    pb = {}
    for n in range(len(items) + 2):
        if n < len(items):
            sm[n] = scores(*items[n])
        if 1 <= n <= len(items):
            pb[n - 1] = probs(*sm.pop(n - 1))
        if n >= 2:
            output(*items[n - 2], pb.pop(n - 2))


def _nbr_attn(qc, kc, vc, bias, gate, rows_per_step, head_splits):
    b, s, w = qc.shape
    grid_rows = s // GRID_W
    nkeys = NA_ROWS * GRID_W
    wh = w // head_splits
    tq = rows_per_step * GRID_W
    blk = lambda hh, bi, rb: (bi, rb, hh)
    return pl.pallas_call(
        functools.partial(_nbr_attn_kernel, grid_rows=grid_rows, rows_per_step=rows_per_step),
        out_shape=jax.ShapeDtypeStruct((b, s, w), BF16),
        grid=(head_splits, b, grid_rows // rows_per_step),
        in_specs=[pl.BlockSpec((None, tq, wh), blk),
                  pl.BlockSpec((None, s, wh), lambda hh, bi, rb: (bi, 0, hh)),
                  pl.BlockSpec((None, s, wh), lambda hh, bi, rb: (bi, 0, hh)),
                  pl.BlockSpec((NA_ROWS, wh // LANES, 2 * GRID_W, nkeys), lambda hh, bi, rb: (0, hh, 0, 0)),
                  pl.BlockSpec((None, tq, wh), blk)],
        out_specs=pl.BlockSpec((None, tq, wh), blk),
        compiler_params=_params("arbitrary", "arbitrary", "arbitrary"),
        name="nbr_attn",
    )(qc, kc, vc, bias, gate)


def _out_final_kernel(yc_ref, ym_ref, x_ref, w_ref, g_ref, o_ref):
    acc = x_ref[...]
    off = 0
    for y_ref in (yc_ref, ym_ref):
        width = y_ref.shape[1]
        acc = acc + jnp.dot(y_ref[...], w_ref[off:off + width, :], preferred_element_type=F32)
        off += width
    o_ref[...] = _rms_rows(acc, g_ref[...])


def _out_final(yc, ym, x2d, w_bf16, final_gain, tm):
    n_tok, d = x2d.shape
    row = lambda i: (i, 0)
    const = lambda i: (0, 0)
    return pl.pallas_call(
        _out_final_kernel,
        out_shape=jax.ShapeDtypeStruct((n_tok, d), F32),
        grid=(n_tok // tm,),
        in_specs=[pl.BlockSpec((tm, yc.shape[1]), row),
                  pl.BlockSpec((tm, ym.shape[1]), row),
                  pl.BlockSpec((tm, d), row),
                  pl.BlockSpec(w_bf16.shape, const),
                  pl.BlockSpec((1, d), const)],
        out_specs=pl.BlockSpec((tm, d), row),
        compiler_params=_params("arbitrary"),
        name="out_proj_final",
    )(yc, ym, x2d, w_bf16, final_gain)


def _rope_tables(seq):
    quarter = HEAD_DIM // 4
    freqs = jnp.power(ROPE_THETA, -jnp.arange(quarter, dtype=F32) / quarter)
    t = jnp.arange(seq)
    ang_r = (t // GRID_W).astype(F32)[:, None] * freqs
    ang_c = (t % GRID_W).astype(F32)[:, None] * freqs
    cos_h = jnp.concatenate([jnp.cos(ang_r), jnp.cos(ang_r), jnp.cos(ang_c), jnp.cos(ang_c)], axis=-1)
    sin_h = jnp.concatenate([-jnp.sin(ang_r), jnp.sin(ang_r), -jnp.sin(ang_c), jnp.sin(ang_c)], axis=-1)
    return jnp.tile(cos_h, (1, 2)), jnp.tile(sin_h, (1, 2))


def _t5_bucket(rel):
    nb = REL_BUCKETS // 2
    max_exact = nb // 2
    ret = jnp.where(rel > 0, nb, 0)
    n = jnp.abs(rel)
    nf = jnp.maximum(n, 1).astype(F32)
    large = max_exact + (jnp.log(nf / max_exact) / math.log(REL_MAX_DIST / max_exact)
                         * (nb - max_exact)).astype(jnp.int32)
    large = jnp.minimum(large, nb - 1)
    return ret + jnp.where(n < max_exact, n, large)


def _window_bias(rel_bias, blk):
    span = blk + 2 * WINDOW
    rel = (jnp.arange(span) - WINDOW)[None, :] - jnp.arange(blk)[:, None]
    band = jnp.abs(rel) <= WINDOW
    onehot = (_t5_bucket(rel)[None] == jnp.arange(REL_BUCKETS)[:, None, None]).astype(F32)
    bias = jnp.einsum('bh,bqs->shq', rel_bias.astype(F32) * LOG2E, onehot,
                      precision=lax.Precision.HIGHEST)
    bias = jnp.where(band.T[:, None, :], bias, NEG)
    group = B_HEADS // B_KV
    return bias.reshape(span, B_KV, group * blk).transpose(1, 0, 2)


def _nbr_bias(rpb):
    col = np.arange(GRID_W)
    cs = np.clip(col - NA_COLS // 2, 0, GRID_W - NA_COLS)
    colmask = (col[None, :] >= cs[:, None]) & (col[None, :] < cs[:, None] + NA_COLS)
    dc = np.clip(col[None, :] - col[:, None] + NA_COLS - 1, 0, 2 * NA_COLS - 2)
    onehot = (dc[None] == np.arange(2 * NA_COLS - 1)[:, None, None]) & colmask[None]
    t = jnp.einsum('hrc,cqk->hqrk', rpb.astype(F32) * LOG2E, jnp.asarray(onehot, F32),
                   precision=lax.Precision.HIGHEST)
    t = t + jnp.asarray(np.where(colmask, 0.0, NEG), F32)[None, :, None, :]
    per_delta = [t[:, :, NA_ROWS - 1 - dl:2 * NA_ROWS - 1 - dl, :] for dl in range(NA_ROWS)]
    return jnp.stack(per_delta).reshape(NA_ROWS, C_HEADS // 2, 2 * GRID_W, NA_ROWS * GRID_W)


def kernel(x, mem, norm_gain, mem_norm_gain, w_in_even, w_out_even, q_norm_a, k_norm_a, sink_b,
           rel_bias, w_in_odd, w_out_odd, rpb_c, w_mem_kv, final_norm_gain):
    b, s, d = x.shape
    mlen = mem.shape[1]
    assert s % GRID_W == 0 and s // GRID_W >= NA_ROWS and s % 512 == 0
    tm = 512
    blk = 128

    x2d = x.reshape(b * s, d)
    mem_tm = math.gcd(b * mlen, 512)
    mkv = _mem_kv(mem.reshape(b * mlen, d), mem_norm_gain.reshape(1, d), w_mem_kv.astype(BF16), mem_tm)
    mkv = mkv.reshape(w_mem_kv.shape[0], b, mlen, 2 * MEM_W)

    cos, sin = _rope_tables(s)
    gq = jnp.tile(q_norm_a[0].astype(F32), 2).reshape(1, LANES)
    gk = jnp.tile(k_norm_a[0].astype(F32), 2).reshape(1, LANES)
    qa, ka, va, qb, kb, vb, ga, gb, ym = _in_even(
        x2d, norm_gain[0].reshape(1, d), w_in_even[0].astype(BF16), gq, gk, cos, sin, mkv, s, tm)
    r3 = lambda a: a.reshape(b, s, a.shape[-1])
    ya = _global_attn(r3(qa), r3(ka), va, r3(ga), tq=256)
    group_b = B_HEADS // B_KV
    sink_cols = jnp.repeat(sink_b[0].astype(F32).reshape(B_KV, group_b) * LOG2E, blk,
                           axis=1).reshape(B_KV, 1, group_b * blk)
    yb = _window_attn(r3(qb), r3(kb), vb, _window_bias(rel_bias, blk), sink_cols, r3(gb), blk,
                      blocks_per_step=4)

    x1, qc, kc, vc, gc, ym1 = _mid(ya.reshape(b * s, -1), yb.reshape(b * s, -1), ym, x2d,
                                   w_out_even[0].astype(BF16), norm_gain[1].reshape(1, d),
                                   w_in_odd[0].astype(BF16), mkv, s, tm)
    yc = _nbr_attn(r3(qc), r3(kc), r3(vc), _nbr_bias(rpb_c[0]), r3(gc), rows_per_step=8, head_splits=2)
    out = _out_final(yc.reshape(b * s, -1), ym1, x1, w_out_odd[0].astype(BF16),
                     final_norm_gain.reshape(1, d), tm)
    return out.reshape(b, s, d)
```

```python
import functools
import math

import jax
import jax.numpy as jnp
import numpy as np
from jax import lax
from jax.experimental import pallas as pl
from jax.experimental.pallas import tpu as pltpu

GRID_W = 64
HEAD_DIM = 64
A_HEADS = 8
A_KV = 2
B_HEADS = 8
B_KV = 2
WINDOW = 128
C_HEADS = 16
NA_ROWS = 8
NA_COLS = 16
M_HEADS = 4
M_HEAD_DIM = 128
MEM_W = M_HEADS * M_HEAD_DIM
REL_BUCKETS = 32
REL_MAX_DIST = 128
ROPE_THETA = 10000.0
EPS = 1e-6

LANES = 128
NEG = -1e30
LOG2E = math.log2(math.e)
V_ONES_ROWS = 16
VMEM_LIMIT_BYTES = 56 * 1024 * 1024

F32 = jnp.float32
BF16 = jnp.bfloat16

_NT = (((1,), (1,)), ((), ()))


def _params(*sem):
    return pltpu.CompilerParams(dimension_semantics=sem, vmem_limit_bytes=VMEM_LIMIT_BYTES)


def _rms_rows(x, gain):
    ms = jnp.mean(x * x, axis=-1, keepdims=True)
    return x * lax.rsqrt(ms + EPS) * gain


def _lane_iota(shape):
    return lax.broadcasted_iota(jnp.int32, shape, len(shape) - 1)


def _silu(x):
    return x * (1.0 / (1.0 + jnp.exp(-x)))


def _half_ones():
    r = lax.broadcasted_iota(jnp.int32, (LANES, LANES), 0) // HEAD_DIM
    c = lax.broadcasted_iota(jnp.int32, (LANES, LANES), 1) // HEAD_DIM
    return jnp.where(r == c, 1.0, 0.0).astype(BF16)


def _head_norm_rope(t, gain, cos, sin_signed, ones_bd):
    ss = jnp.dot((t * t).astype(BF16), ones_bd, preferred_element_type=F32)
    tn = t * lax.rsqrt(ss * (1.0 / HEAD_DIM) + EPS) * gain
    lane = _lane_iota(tn.shape)
    quarter = HEAD_DIM // 4
    partner = jnp.where((lane % (2 * quarter)) < quarter,
                        pltpu.roll(tn, LANES - quarter, 1), pltpu.roll(tn, quarter, 1))
    return tn * cos + partner * sin_signed


def _store_padded_heads(q_tiles, out_ref, n_heads, n_kv, transposed=False):
    group = n_heads // n_kv
    for h in range(n_heads):
        t = q_tiles[h // 2]
        src_half = h % 2
        dst_half = (h // group) % 2
        if src_half != dst_half:
            t = pltpu.roll(t, HEAD_DIM, 1)
        lane = _lane_iota(t.shape)
        keep = (lane >= HEAD_DIM) if dst_half == 1 else (lane < HEAD_DIM)
        padded = jnp.where(keep, t, 0.0)
        if transposed:
            out_ref[h] = padded.T.astype(BF16)
        else:
            out_ref[:, h * LANES:(h + 1) * LANES] = padded.astype(BF16)


def _mem_kv_kernel(mem_ref, g_ref, w_ref, o_ref):
    h = _rms_rows(mem_ref[...], g_ref[...]).astype(BF16)
    o_ref[...] = jnp.dot(h, w_ref[...], preferred_element_type=F32).astype(BF16)


def _mem_kv(mem2d, gain, w_bf16, tm):
    depth, d, n = w_bf16.shape
    rows = mem2d.shape[0]
    return pl.pallas_call(
        _mem_kv_kernel,
        out_shape=jax.ShapeDtypeStruct((depth, rows, n), BF16),
        grid=(depth, rows // tm),
        in_specs=[pl.BlockSpec((tm, d), lambda l, i: (i, 0)),
                  pl.BlockSpec((1, d), lambda l, i: (0, 0)),
                  pl.BlockSpec((None, d, n), lambda l, i: (l, 0, 0))],
        out_specs=pl.BlockSpec((None, tm, n), lambda l, i: (l, i, 0)),
        compiler_params=_params("arbitrary", "arbitrary"),
        name="mem_kv_proj",
    )(mem2d, gain, w_bf16)


def _in_even_kernel(x_ref, g_ref, w_ref, gq_ref, gk_ref, cos_ref, sin_ref, mkv_ref,
                    qa_ref, ka_ref, va_ref, qb_ref, kb_ref, vb_ref, ga_ref, gb_ref, ym_ref,
                    qm_scr, gm_scr):
    h = _rms_rows(x_ref[...], g_ref[...]).astype(BF16)

    def seg(lo, hi):
        return jnp.dot(h, w_ref[:, lo:hi], preferred_element_type=F32)

    ones_bd = _half_ones()
    scale = HEAD_DIM ** -0.5
    qa_w = A_HEADS * HEAD_DIM
    kva_w = A_KV * HEAD_DIM
    qb_w = B_HEADS * HEAD_DIM
    kvb_w = B_KV * HEAD_DIM
    offs = np.cumsum([0, qa_w, kva_w, kva_w, qb_w, kvb_w, kvb_w, MEM_W, qa_w, qb_w, MEM_W])
    o_qa, o_ka, o_va, o_qb, o_kb, o_vb, o_qm, o_ga, o_gb, o_gm = (int(v) for v in offs[:-1])

    def do_qa():
        zq = seg(o_qa, o_qa + qa_w)
        tiles = [_head_norm_rope(zq[:, j * LANES:(j + 1) * LANES], gq_ref[...], cos_ref[...],
                                 sin_ref[...], ones_bd) * (scale * LOG2E) for j in range(qa_w // LANES)]
        _store_padded_heads(tiles, qa_ref, A_HEADS, A_KV, transposed=True)

    def do_ka():
        ka_ref[...] = _head_norm_rope(seg(o_ka, o_ka + kva_w), gk_ref[...], cos_ref[...],
                                      sin_ref[...], ones_bd).astype(BF16)

    def do_va():
        vt = seg(o_va, o_va + kva_w).T.astype(BF16)
        for g in range(A_KV):
            va_ref[g, :HEAD_DIM] = vt[g * HEAD_DIM:(g + 1) * HEAD_DIM]
            va_ref[g, HEAD_DIM:] = jnp.ones((V_ONES_ROWS, vt.shape[1]), BF16)

    def do_qb():
        zq = seg(o_qb, o_qb + qb_w) * (scale * LOG2E)
        _store_padded_heads([zq[:, j * LANES:(j + 1) * LANES] for j in range(qb_w // LANES)],
                            qb_ref, B_HEADS, B_KV, transposed=True)

    def do_kb():
        kb_ref[...] = seg(o_kb, o_kb + kvb_w).astype(BF16)

    def do_vb():
        vt = seg(o_vb, o_vb + kvb_w).T.astype(BF16)
        for j in range(vb_ref.shape[0]):
            for g in range(B_KV):
                vb_ref[j, g, :HEAD_DIM] = vt[g * HEAD_DIM:(g + 1) * HEAD_DIM, j * WINDOW:(j + 1) * WINDOW]
                vb_ref[j, g, HEAD_DIM:] = jnp.ones((V_ONES_ROWS, WINDOW), BF16)

    def do_ga():
        ga_ref[...] = _silu(seg(o_ga, o_ga + qa_w)).astype(BF16)

    def do_gb():
        gb_ref[...] = _silu(seg(o_gb, o_gb + qb_w)).astype(BF16)

    qm_scr[...] = (seg(o_qm, o_qm + MEM_W) * (M_HEAD_DIM ** -0.5 * LOG2E)).astype(BF16)
    gm_scr[...] = _silu(seg(o_gm, o_gm + MEM_W)).astype(BF16)
    _mem_attn_items(qm_scr, mkv_ref, gm_scr, ym_ref,
                    between=[do_ga, do_gb, do_qa, do_qb, do_ka, do_kb, do_va, do_vb])


def _in_even(x2d, gain, w_bf16, gq, gk, cos, sin, mkv, seq, tm):
    n_tok, d = x2d.shape
    n_in = w_bf16.shape[1]
    per_seq = seq // tm
    mlen = mkv.shape[2]
    widths = [A_HEADS * LANES, A_KV * HEAD_DIM, A_KV * HEAD_DIM,
              B_HEADS * LANES, B_KV * HEAD_DIM, B_KV * HEAD_DIM,
              A_HEADS * HEAD_DIM, B_HEADS * HEAD_DIM, MEM_W]
    row = lambda i: (i, 0)
    const = lambda i: (0, 0)
    out_shape = [jax.ShapeDtypeStruct((n_tok, w), BF16) for w in widths]
    out_specs = [pl.BlockSpec((tm, w), row) for w in widths]
    vrows = HEAD_DIM + V_ONES_ROWS
    out_shape[0] = jax.ShapeDtypeStruct((n_tok // tm, A_HEADS, LANES, tm), BF16)
    out_specs[0] = pl.BlockSpec((None, A_HEADS, LANES, tm), lambda i: (i, 0, 0, 0))
    out_shape[3] = jax.ShapeDtypeStruct((n_tok // tm, B_HEADS, LANES, tm), BF16)
    out_specs[3] = pl.BlockSpec((None, B_HEADS, LANES, tm), lambda i: (i, 0, 0, 0))
    out_shape[2] = jax.ShapeDtypeStruct((n_tok // tm, A_KV, vrows, tm), BF16)
    out_specs[2] = pl.BlockSpec((None, A_KV, vrows, tm), lambda i: (i, 0, 0, 0))
    out_shape[5] = jax.ShapeDtypeStruct((n_tok // WINDOW, B_KV, vrows, WINDOW), BF16)
    out_specs[5] = pl.BlockSpec((tm // WINDOW, B_KV, vrows, WINDOW), lambda i: (i, 0, 0, 0))
    return pl.pallas_call(
        _in_even_kernel,
        out_shape=out_shape,
        grid=(n_tok // tm,),
        in_specs=[pl.BlockSpec((tm, d), row),
                  pl.BlockSpec((1, d), const),
                  pl.BlockSpec((d, n_in), const),
                  pl.BlockSpec((1, LANES), const),
                  pl.BlockSpec((1, LANES), const),
                  pl.BlockSpec((tm, LANES), lambda i: (i % per_seq, 0)),
                  pl.BlockSpec((tm, LANES), lambda i: (i % per_seq, 0)),
                  pl.BlockSpec((None, None, mlen, 2 * MEM_W), lambda i: (0, i // per_seq, 0, 0))],
        out_specs=out_specs,
        scratch_shapes=[pltpu.VMEM((tm, MEM_W), BF16), pltpu.VMEM((tm, MEM_W), BF16)],
        compiler_params=_params("arbitrary"),
        name="in_proj_even",
    )(x2d, gain, w_bf16, gq, gk, cos, sin, mkv)


def _mid_kernel(ya_ref, yb_ref, ym_ref, x_ref, wo_ref, g_ref, wi_ref, mkv_ref,
                x1_ref, qc_ref, kc_ref, vc_ref, gc_ref, ym1_ref, qm_scr, gm_scr):
    acc = x_ref[...]
    off = 0
    for y_ref in (ya_ref, yb_ref, ym_ref):
        width = y_ref.shape[1]
        acc = acc + jnp.dot(y_ref[...], wo_ref[off:off + width, :], preferred_element_type=F32)
        off += width
    x1_ref[...] = acc
    h = _rms_rows(acc, g_ref[...]).astype(BF16)

    def seg(lo, hi):
        return jnp.dot(h, wi_ref[:, lo:hi], preferred_element_type=F32)

    cw = C_HEADS * HEAD_DIM
    half = cw // 2
    o_qc, o_kc, o_vc, o_qm, o_gc, o_gm = 0, cw, 2 * cw, 3 * cw, 3 * cw + MEM_W, 4 * cw + MEM_W

    def thunk(ref, base, part, fn):
        def run():
            lo = part * half
            ref[:, lo:lo + half] = fn(seg(base + lo, base + lo + half)).astype(BF16)
        return run

    plain = lambda z: z
    to_q = lambda z: z * (HEAD_DIM ** -0.5 * LOG2E)
    segments = [thunk(ref, base, part, fn)
                for ref, base, fn in ((gc_ref, o_gc, _silu), (qc_ref, o_qc, to_q),
                                      (kc_ref, o_kc, plain), (vc_ref, o_vc, plain))
                for part in range(2)]
    qm_scr[...] = (seg(o_qm, o_qm + MEM_W) * (M_HEAD_DIM ** -0.5 * LOG2E)).astype(BF16)
    gm_scr[...] = _silu(seg(o_gm, o_gm + MEM_W)).astype(BF16)
    _mem_attn_items(qm_scr, mkv_ref, gm_scr, ym1_ref, between=segments)


def _mid(ya, yb, ym, x2d, wo_bf16, gain, wi_bf16, mkv, seq, tm):
    n_tok, d = x2d.shape
    per_seq = seq // tm
    mlen = mkv.shape[2]
    cw = C_HEADS * HEAD_DIM
    row = lambda i: (i, 0)
    const = lambda i: (0, 0)
    once = pl.Buffered(1)
    out_widths = [cw, cw, cw, cw, MEM_W]
    return pl.pallas_call(
        _mid_kernel,
        out_shape=[jax.ShapeDtypeStruct((n_tok, d), F32)]
                  + [jax.ShapeDtypeStruct((n_tok, w), BF16) for w in out_widths],
        grid=(n_tok // tm,),
        in_specs=[pl.BlockSpec((tm, ya.shape[1]), row),
                  pl.BlockSpec((tm, yb.shape[1]), row),
                  pl.BlockSpec((tm, ym.shape[1]), row),
                  pl.BlockSpec((tm, d), row),
                  pl.BlockSpec(wo_bf16.shape, const, pipeline_mode=once),
                  pl.BlockSpec((1, d), const),
                  pl.BlockSpec(wi_bf16.shape, const, pipeline_mode=once),
                  pl.BlockSpec((None, None, mlen, 2 * MEM_W), lambda i: (1, i // per_seq, 0, 0))],
        out_specs=[pl.BlockSpec((tm, d), row)] + [pl.BlockSpec((tm, w), row) for w in out_widths],
        scratch_shapes=[pltpu.VMEM((tm, MEM_W), BF16), pltpu.VMEM((tm, MEM_W), BF16)],
        compiler_params=_params("arbitrary"),
        name="out_proj_in_proj_odd",
    )(ya, yb, ym, x2d, wo_bf16, gain, wi_bf16, mkv)


def _global_attn_kernel(q_ref, k_ref, vt_ref, gate_ref, o_ref, m_scr, acc_scr, s_scr, mc_scr):
    tq = q_ref.shape[2]
    n_chunks = vt_ref.shape[0]
    tk = vt_ref.shape[3]
    group = A_HEADS // A_KV
    qts = [jnp.concatenate([q_ref[g * group + i] for i in range(group)], axis=1)
           for g in range(A_KV)]
    m_scr[...] = jnp.full(m_scr.shape, NEG, F32)
    acc_scr[...] = jnp.zeros(acc_scr.shape, F32)

    parts = 2
    tp = tk // parts
    pieces = [(p, g) for p in range(parts) for g in range(A_KV)]

    def scores_piece(c, slot, p, g):
        start = pl.multiple_of(c * tk + p * tp, tp)
        st = jnp.dot(k_ref[pl.ds(start, tp), :], qts[g], preferred_element_type=F32)
        s_scr[slot, g, p * tp:(p + 1) * tp] = st
        mx = jnp.max(st, axis=0, keepdims=True)
        mc_scr[slot, g] = mx if p == 0 else jnp.maximum(mc_scr[slot, g], mx)

    def accumulate_piece(c, slot, p, g):
        m_old = m_scr[g]
        m_new = jnp.maximum(m_old, mc_scr[slot, g])
        pt = jnp.exp2(s_scr[slot, g, p * tp:(p + 1) * tp] - m_new).astype(BF16)
        pv = jnp.dot(vt_ref[c, g][:, p * tp:(p + 1) * tp], pt, preferred_element_type=F32)
        if p == 0:
            acc_scr[g] = jnp.exp2(m_old - m_new) * acc_scr[g] + pv
        else:
            acc_scr[g] = acc_scr[g] + pv
        if p == parts - 1:
            m_scr[g] = m_new

    def scores(c, slot):
        for p, g in pieces:
            scores_piece(c, slot, p, g)

    def accumulate(c, slot):
        for p, g in pieces:
            accumulate_piece(c, slot, p, g)

    def scores_and_accumulate(c_s, slot_s, c_a, slot_a):
        for p, g in pieces:
            scores_piece(c_s, slot_s, p, g)
            accumulate_piece(c_a, slot_a, p, g)

    scores(0, 0)

    def body(c2, carry):
        c = 2 * c2
        scores_and_accumulate(c + 1, 1, c, 0)
        scores_and_accumulate(c + 2, 0, c + 1, 1)
        return carry

    lax.fori_loop(0, n_chunks // 2 - 1, body, 0)
    scores_and_accumulate(n_chunks - 1, 1, n_chunks - 2, 0)
    accumulate(n_chunks - 1, 1)

    ot = [acc_scr[g, :HEAD_DIM] * (1.0 / acc_scr[g, HEAD_DIM:HEAD_DIM + 1]) for g in range(A_KV)]
    for j in range(A_HEADS // 2):
        g, i0 = (2 * j) // group, (2 * j) % group
        tile_t = jnp.concatenate([ot[g][:, i0 * tq:(i0 + 1) * tq],
                                  ot[g][:, (i0 + 1) * tq:(i0 + 2) * tq]], axis=0)
        gate = gate_ref[:, j * LANES:(j + 1) * LANES].astype(F32)
        o_ref[:, j * LANES:(j + 1) * LANES] = (tile_t.T * gate).astype(BF16)


def _global_attn(qat, ka, vat, gate, tq):
    b, s, _ = ka.shape
    _, n_kv, vrows, tk = vat.shape
    tm = qat.shape[-1]
    n_chunks = s // tk
    assert n_chunks % 2 == 0 and tm % tq == 0
    rows = (A_HEADS // A_KV) * tq
    vat = vat.reshape(b, n_chunks, n_kv, vrows, tk)
    qat = qat.reshape(b, s // tm, A_HEADS, LANES, tm)
    per_slab = tm // tq
    return pl.pallas_call(
        _global_attn_kernel,
        out_shape=jax.ShapeDtypeStruct((b, s, A_HEADS * HEAD_DIM), BF16),
        grid=(b, s // tq),
        in_specs=[pl.BlockSpec((None, None, A_HEADS, LANES, tq),
                               lambda bi, i: (bi, i // per_slab, 0, 0, i % per_slab)),
                  pl.BlockSpec((None, s, LANES), lambda bi, i: (bi, 0, 0)),
                  pl.BlockSpec((None, n_chunks, n_kv, vrows, tk), lambda bi, i: (bi, 0, 0, 0, 0)),
                  pl.BlockSpec((None, tq, A_HEADS * HEAD_DIM), lambda bi, i: (bi, i, 0))],
        out_specs=pl.BlockSpec((None, tq, A_HEADS * HEAD_DIM), lambda bi, i: (bi, i, 0)),
        scratch_shapes=[pltpu.VMEM((A_KV, 1, rows), F32),
                        pltpu.VMEM((A_KV, vrows, rows), F32),
                        pltpu.VMEM((2, A_KV, tk, rows), F32), pltpu.VMEM((2, A_KV, 1, rows), F32)],
        compiler_params=_params("arbitrary", "arbitrary"),
        name="global_attn",
    )(qat, ka, vat, gate)


def _window_attn_kernel(q_ref, k_ref, vt_ref, bias_ref, sink_ref, gate_ref, o_ref, *, blk):
    blocks_per_step = q_ref.shape[2] // blk
    nb = vt_ref.shape[0]
    step = pl.program_id(1)
    group = B_HEADS // B_KV

    def neighbours(t):
        i = step * blocks_per_step + t
        return jnp.maximum(i - 1, 0), i, jnp.minimum(i + 1, nb - 1)

    def scores(t, g):
        i = step * blocks_per_step + t
        qt = jnp.concatenate([q_ref[g * group + j][:, t * blk:(t + 1) * blk]
                              for j in range(group)], axis=1)
        k = jnp.concatenate([k_ref[pl.ds(pl.multiple_of(n * blk, blk), blk), :]
                             for n in neighbours(t)], axis=0)
        st = jnp.dot(k, qt, preferred_element_type=F32) + bias_ref[g]
        if t == 0:
            st = jnp.concatenate([st[:blk] + jnp.where(i == 0, NEG, 0.0), st[blk:]], axis=0)
        if t == blocks_per_step - 1:
            st = jnp.concatenate([st[:2 * blk], st[2 * blk:] + jnp.where(i == nb - 1, NEG, 0.0)], axis=0)
        m = jnp.maximum(jnp.max(st, axis=0, keepdims=True), sink_ref[g])
        return st, m

    def probs(st, m):
        return jnp.exp2(st - m).astype(BF16), m

    def output(t, g, pt, m):
        vt = jnp.concatenate([vt_ref[n, g] for n in neighbours(t)], axis=1)
        ot = jnp.dot(vt, pt, preferred_element_type=F32)
        denom = ot[HEAD_DIM:HEAD_DIM + 1] + jnp.exp2(sink_ref[g] - m)
        return ot[:HEAD_DIM] * (1.0 / denom)

    def store(t, ot):
        rows = slice(t * blk, (t + 1) * blk)
        for j in range(B_HEADS // 2):
            g, i0 = (2 * j) // group, (2 * j) % group
            tile_t = jnp.concatenate([ot[g][:, i0 * blk:(i0 + 1) * blk],
                                      ot[g][:, (i0 + 1) * blk:(i0 + 2) * blk]], axis=0)
            gate = gate_ref[rows, j * LANES:(j + 1) * LANES].astype(F32)
            o_ref[rows, j * LANES:(j + 1) * LANES] = (tile_t.T * gate).astype(BF16)

    items = [(t, g) for t in range(blocks_per_step) for g in range(B_KV)]
    sm, pb, outs = {}, {}, {}
    for n in range(len(items) + 2):
        if n < len(items):
            sm[n] = scores(*items[n])
        if 1 <= n <= len(items):
            pb[n - 1] = probs(*sm.pop(n - 1))
        if n >= 2:
            t, g = items[n - 2]
            outs[g] = output(t, g, *pb.pop(n - 2))
            if g == B_KV - 1:
                store(t, outs)


def _window_attn(qbt, kb, vbt, bias_t, sink_cols, gate, blk, blocks_per_step):
    b, s, _ = kb.shape
    nb = s // blk
    rows = (B_HEADS // B_KV) * blk
    tq = blk * blocks_per_step
    tm = qbt.shape[-1]
    assert tm % tq == 0
    per_slab = tm // tq
    vrows = vbt.shape[2]
    vbt = vbt.reshape(b, nb, B_KV, vrows, blk)
    qbt = qbt.reshape(b, s // tm, B_HEADS, LANES, tm)
    return pl.pallas_call(
        functools.partial(_window_attn_kernel, blk=blk),
        out_shape=jax.ShapeDtypeStruct((b, s, B_HEADS * HEAD_DIM), BF16),
        grid=(b, s // tq),
        in_specs=[pl.BlockSpec((None, None, B_HEADS, LANES, tq),
                               lambda bi, i: (bi, i // per_slab, 0, 0, i % per_slab)),
                  pl.BlockSpec((None, s, LANES), lambda bi, i: (bi, 0, 0)),
                  pl.BlockSpec((None, nb, B_KV, vrows, blk), lambda bi, i: (bi, 0, 0, 0, 0)),
                  pl.BlockSpec((B_KV, 3 * blk, rows), lambda bi, i: (0, 0, 0)),
                  pl.BlockSpec((B_KV, 1, rows), lambda bi, i: (0, 0, 0)),
                  pl.BlockSpec((None, tq, B_HEADS * HEAD_DIM), lambda bi, i: (bi, i, 0))],
        out_specs=pl.BlockSpec((None, tq, B_HEADS * HEAD_DIM), lambda bi, i: (bi, i, 0)),
        compiler_params=_params("arbitrary", "arbitrary"),
        name="window_attn",
    )(qbt, kb, vbt, bias_t, sink_cols, gate)


def _mem_attn_items(q_ref, kv_ref, gate_ref, o_ref, between=(), sub=256):
    between = list(between)
    mlen = kv_ref.shape[0]
    ones = jnp.ones((mlen, M_HEAD_DIM), BF16)

    def scores(t, h):
        lo, hi = h * M_HEAD_DIM, (h + 1) * M_HEAD_DIM
        s = lax.dot_general(q_ref[t * sub:(t + 1) * sub, lo:hi], kv_ref[:, lo:hi], _NT,
                            preferred_element_type=F32)
        return s, jnp.max(s, axis=-1, keepdims=True)

    def probs(s, m):
        return jnp.exp2(s - m).astype(BF16)

    def output(t, h, p):
        lo, hi = h * M_HEAD_DIM, (h + 1) * M_HEAD_DIM
        v = jnp.concatenate([kv_ref[:, MEM_W + lo:MEM_W + hi], ones], axis=1)
        o = jnp.dot(p, v, preferred_element_type=F32)
        o = o[:, :M_HEAD_DIM] * (1.0 / o[:, M_HEAD_DIM:])
        rows = slice(t * sub, (t + 1) * sub)
        o_ref[rows, lo:hi] = (o * gate_ref[rows, lo:hi].astype(F32)).astype(BF16)

    items = [(t, h) for t in range(q_ref.shape[0] // sub) for h in range(M_HEADS)]
    sm, pb = {}, {}
    for n in range(len(items) + 2):
        if n < len(items):
            sm[n] = scores(*items[n])
        if 1 <= n <= len(items):
            pb[n - 1] = probs(*sm.pop(n - 1))
        if n >= 2:
            output(*items[n - 2], pb.pop(n - 2))
        if between:
            between.pop(0)()
    for run in between:
        run()


def _nbr_attn_kernel(q_ref, k_ref, v_ref, bias_ref, gate_ref, o_ref, *, grid_rows, rows_per_step):
    rb = pl.program_id(2)
    nkeys = NA_ROWS * GRID_W
    n_pairs = q_ref.shape[1] // LANES
    ones = jnp.ones((nkeys, LANES), BF16)
    starts, deltas = [], []
    for t in range(rows_per_step):
        r = rb * rows_per_step + t
        rs = jnp.clip(r - NA_ROWS // 2, 0, grid_rows - NA_ROWS)
        starts.append(pl.multiple_of(rs * GRID_W, GRID_W))
        deltas.append(r - rs)
    items = [(t, j) for t in range(rows_per_step) for j in range(n_pairs)]

    def scores(t, j):
        lo, hi = j * LANES, (j + 1) * LANES
        qt = q_ref[t * GRID_W:(t + 1) * GRID_W, lo:hi]
        lane = _lane_iota(qt.shape)
        zero = jnp.zeros_like(qt)
        q = jnp.concatenate([jnp.where(lane < HEAD_DIM, qt, zero),
                             jnp.where(lane >= HEAD_DIM, qt, zero)], axis=0)
        k = k_ref[pl.ds(starts[t], nkeys), lo:hi]
        s = lax.dot_general(q, k, _NT, preferred_element_type=F32) + bias_ref[deltas[t], j]
        return s, jnp.max(s, axis=-1, keepdims=True)

    def probs(s, m):
        return jnp.exp2(s - m).astype(BF16)

    def output(t, j, p):
        lo, hi = j * LANES, (j + 1) * LANES
        v = jnp.concatenate([v_ref[pl.ds(starts[t], nkeys), lo:hi], ones], axis=1)
        o = jnp.dot(p, v, preferred_element_type=F32)
        o = o[:, :LANES] * (1.0 / o[:, LANES:])
        lane_o = _lane_iota((GRID_W, LANES))
        tile = jnp.where(lane_o < HEAD_DIM, o[:GRID_W, :], o[GRID_W:, :])
        rows = slice(t * GRID_W, (t + 1) * GRID_W)
        o_ref[rows, lo:hi] = (tile * gate_ref[rows, lo:hi].astype(F32)).astype(BF16)

    sm = {}
    pb = {}
    for n in range(len(items) + 2):
        if n < len(items):
            sm[n] = scores(*items[n])
        if 1 <= n <= len(items):
            pb[n - 1] = probs(*sm.pop(n - 1))
        if n >= 2:
            output(*items[n - 2], pb.pop(n - 2))


def _nbr_attn(qc, kc, vc, bias, gate, rows_per_step, head_splits):
    b, s, w = qc.shape
    grid_rows = s // GRID_W
    nkeys = NA_ROWS * GRID_W
    wh = w // head_splits
    tq = rows_per_step * GRID_W
    blk = lambda hh, bi, rb: (bi, rb, hh)
    return pl.pallas_call(
        functools.partial(_nbr_attn_kernel, grid_rows=grid_rows, rows_per_step=rows_per_step),
        out_shape=jax.ShapeDtypeStruct((b, s, w), BF16),
        grid=(head_splits, b, grid_rows // rows_per_step),
        in_specs=[pl.BlockSpec((None, tq, wh), blk),
                  pl.BlockSpec((None, s, wh), lambda hh, bi, rb: (bi, 0, hh)),
                  pl.BlockSpec((None, s, wh), lambda hh, bi, rb: (bi, 0, hh)),
                  pl.BlockSpec((NA_ROWS, wh // LANES, 2 * GRID_W, nkeys), lambda hh, bi, rb: (0, hh, 0, 0)),
                  pl.BlockSpec((None, tq, wh), blk)],
        out_specs=pl.BlockSpec((None, tq, wh), blk),
        compiler_params=_params("arbitrary", "arbitrary", "arbitrary"),
        name="nbr_attn",
    )(qc, kc, vc, bias, gate)


def _out_final_kernel(yc_ref, ym_ref, x_ref, w_ref, g_ref, o_ref):
    acc = x_ref[...]
    off = 0
    for y_ref in (yc_ref, ym_ref):
        width = y_ref.shape[1]
        acc = acc + jnp.dot(y_ref[...], w_ref[off:off + width, :], preferred_element_type=F32)
        off += width
    o_ref[...] = _rms_rows(acc, g_ref[...])


def _out_final(yc, ym, x2d, w_bf16, final_gain, tm):
    n_tok, d = x2d.shape
    row = lambda i: (i, 0)
    const = lambda i: (0, 0)
    return pl.pallas_call(
        _out_final_kernel,
        out_shape=jax.ShapeDtypeStruct((n_tok, d), F32),
        grid=(n_tok // tm,),
        in_specs=[pl.BlockSpec((tm, yc.shape[1]), row),
                  pl.BlockSpec((tm, ym.shape[1]), row),
                  pl.BlockSpec((tm, d), row),
                  pl.BlockSpec(w_bf16.shape, const),
                  pl.BlockSpec((1, d), const)],
        out_specs=pl.BlockSpec((tm, d), row),
        compiler_params=_params("arbitrary"),
        name="out_proj_final",
    )(yc, ym, x2d, w_bf16, final_gain)


def _rope_tables(seq):
    quarter = HEAD_DIM // 4
    freqs = jnp.power(ROPE_THETA, -jnp.arange(quarter, dtype=F32) / quarter)
    t = jnp.arange(seq)
    ang_r = (t // GRID_W).astype(F32)[:, None] * freqs
    ang_c = (t % GRID_W).astype(F32)[:, None] * freqs
    cos_h = jnp.concatenate([jnp.cos(ang_r), jnp.cos(ang_r), jnp.cos(ang_c), jnp.cos(ang_c)], axis=-1)
    sin_h = jnp.concatenate([-jnp.sin(ang_r), jnp.sin(ang_r), -jnp.sin(ang_c), jnp.sin(ang_c)], axis=-1)
    return jnp.tile(cos_h, (1, 2)), jnp.tile(sin_h, (1, 2))


def _t5_bucket(rel):
    nb = REL_BUCKETS // 2
    max_exact = nb // 2
    ret = jnp.where(rel > 0, nb, 0)
    n = jnp.abs(rel)
    nf = jnp.maximum(n, 1).astype(F32)
    large = max_exact + (jnp.log(nf / max_exact) / math.log(REL_MAX_DIST / max_exact)
                         * (nb - max_exact)).astype(jnp.int32)
    large = jnp.minimum(large, nb - 1)
    return ret + jnp.where(n < max_exact, n, large)


def _window_bias(rel_bias, blk):
    span = blk + 2 * WINDOW
    with jax.ensure_compile_time_eval():
        rel = (jnp.arange(span) - WINDOW)[None, :] - jnp.arange(blk)[:, None]
        bucket = np.asarray(_t5_bucket(rel))
        band = np.asarray(jnp.abs(rel) <= WINDOW)
    onehot = (jnp.asarray(bucket)[None] == jnp.arange(REL_BUCKETS)[:, None, None]).astype(F32)
    bias = jnp.einsum('bh,bqs->shq', rel_bias.astype(F32) * LOG2E, onehot,
                      precision=lax.Precision.HIGHEST)
    bias = jnp.where(band.T[:, None, :], bias, NEG)
    group = B_HEADS // B_KV
    return bias.reshape(span, B_KV, group * blk).transpose(1, 0, 2)


def _nbr_bias(rpb):
    col = np.arange(GRID_W)
    cs = np.clip(col - NA_COLS // 2, 0, GRID_W - NA_COLS)
    colmask = (col[None, :] >= cs[:, None]) & (col[None, :] < cs[:, None] + NA_COLS)
    dc = np.clip(col[None, :] - col[:, None] + NA_COLS - 1, 0, 2 * NA_COLS - 2)
    onehot = (dc[None] == np.arange(2 * NA_COLS - 1)[:, None, None]) & colmask[None]
    t = jnp.einsum('hrc,cqk->hqrk', rpb.astype(F32) * LOG2E, jnp.asarray(onehot, F32),
                   precision=lax.Precision.HIGHEST)
    t = t + jnp.asarray(np.where(colmask, 0.0, NEG), F32)[None, :, None, :]
    per_delta = [t[:, :, NA_ROWS - 1 - dl:2 * NA_ROWS - 1 - dl, :] for dl in range(NA_ROWS)]
    return jnp.stack(per_delta).reshape(NA_ROWS, C_HEADS // 2, 2 * GRID_W, NA_ROWS * GRID_W)


def kernel(x, mem, norm_gain, mem_norm_gain, w_in_even, w_out_even, q_norm_a, k_norm_a, sink_b,
           rel_bias, w_in_odd, w_out_odd, rpb_c, w_mem_kv, final_norm_gain):
    b, s, d = x.shape
    mlen = mem.shape[1]
    assert s % GRID_W == 0 and s // GRID_W >= NA_ROWS and s % 512 == 0
    tm = 512
    blk = 128

    x2d = x.reshape(b * s, d)
    mem_tm = math.gcd(b * mlen, 512)
    mkv = _mem_kv(mem.reshape(b * mlen, d), mem_norm_gain.reshape(1, d), w_mem_kv.astype(BF16), mem_tm)
    mkv = mkv.reshape(w_mem_kv.shape[0], b, mlen, 2 * MEM_W)

    cos, sin = _rope_tables(s)
    gq = jnp.tile(q_norm_a[0].astype(F32), 2).reshape(1, LANES)
    gk = jnp.tile(k_norm_a[0].astype(F32), 2).reshape(1, LANES)
    qa, ka, va, qb, kb, vb, ga, gb, ym = _in_even(
        x2d, norm_gain[0].reshape(1, d), w_in_even[0].astype(BF16), gq, gk, cos, sin, mkv, s, tm)
    r3 = lambda a: a.reshape(b, s, a.shape[-1])
    ya = _global_attn(qa, r3(ka), va, r3(ga), tq=256)
    group_b = B_HEADS // B_KV
    sink_cols = jnp.repeat(sink_b[0].astype(F32).reshape(B_KV, group_b) * LOG2E, blk,
                           axis=1).reshape(B_KV, 1, group_b * blk)
    yb = _window_attn(qb, r3(kb), vb, _window_bias(rel_bias, blk), sink_cols, r3(gb), blk,
                      blocks_per_step=4)

    x1, qc, kc, vc, gc, ym1 = _mid(ya.reshape(b * s, -1), yb.reshape(b * s, -1), ym, x2d,
                                   w_out_even[0].astype(BF16), norm_gain[1].reshape(1, d),
                                   w_in_odd[0].astype(BF16), mkv, s, tm)
    yc = _nbr_attn(r3(qc), r3(kc), r3(vc), _nbr_bias(rpb_c[0]), r3(gc), rows_per_step=8, head_splits=2)
    out = _out_final(yc.reshape(b * s, -1), ym1, x1, w_out_odd[0].astype(BF16),
                     final_norm_gain.reshape(1, d), tm)
    return out.reshape(b, s, d)
```

```python
import functools
import math

import jax
import jax.numpy as jnp
import numpy as np
from jax import lax
from jax.experimental import pallas as pl
from jax.experimental.pallas import tpu as pltpu

GRID_W = 64
HEAD_DIM = 64
A_HEADS = 8
A_KV = 2
B_HEADS = 8
B_KV = 2
WINDOW = 128
C_HEADS = 16
NA_ROWS = 8
NA_COLS = 16
M_HEADS = 4
M_HEAD_DIM = 128
MEM_W = M_HEADS * M_HEAD_DIM
REL_BUCKETS = 32
REL_MAX_DIST = 128
ROPE_THETA = 10000.0
EPS = 1e-6

LANES = 128
NEG = -1e30
LOG2E = math.log2(math.e)
V_ONES_ROWS = 16
VMEM_LIMIT_BYTES = 56 * 1024 * 1024

F32 = jnp.float32
BF16 = jnp.bfloat16

_NT = (((1,), (1,)), ((), ()))


def _params(*sem):
    return pltpu.CompilerParams(dimension_semantics=sem, vmem_limit_bytes=VMEM_LIMIT_BYTES)


def _rms_rows(x, gain):
    ms = jnp.mean(x * x, axis=-1, keepdims=True)
    return x * lax.rsqrt(ms + EPS) * gain


def _lane_iota(shape):
    return lax.broadcasted_iota(jnp.int32, shape, len(shape) - 1)


def _silu(x):
    return x * (1.0 / (1.0 + jnp.exp(-x)))


def _half_ones():
    r = lax.broadcasted_iota(jnp.int32, (LANES, LANES), 0) // HEAD_DIM
    c = lax.broadcasted_iota(jnp.int32, (LANES, LANES), 1) // HEAD_DIM
    return jnp.where(r == c, 1.0, 0.0).astype(BF16)


def _head_norm_rope(t, gain, cos, sin_signed, ones_bd):
    ss = jnp.dot((t * t).astype(BF16), ones_bd, preferred_element_type=F32)
    tn = t * lax.rsqrt(ss * (1.0 / HEAD_DIM) + EPS) * gain
    lane = _lane_iota(tn.shape)
    quarter = HEAD_DIM // 4
    partner = jnp.where((lane % (2 * quarter)) < quarter,
                        pltpu.roll(tn, LANES - quarter, 1), pltpu.roll(tn, quarter, 1))
    return tn * cos + partner * sin_signed


def _store_padded_heads(q_tiles, out_ref, n_heads, n_kv, transposed=False):
    group = n_heads // n_kv
    for h in range(n_heads):
        t = q_tiles[h // 2]
        src_half = h % 2
        dst_half = (h // group) % 2
        if src_half != dst_half:
            t = pltpu.roll(t, HEAD_DIM, 1)
        lane = _lane_iota(t.shape)
        keep = (lane >= HEAD_DIM) if dst_half == 1 else (lane < HEAD_DIM)
        padded = jnp.where(keep, t, 0.0)
        if transposed:
            out_ref[h] = padded.T.astype(BF16)
        else:
            out_ref[:, h * LANES:(h + 1) * LANES] = padded.astype(BF16)


def _mem_kv_kernel(mem_ref, g_ref, w_ref, o_ref):
    h = _rms_rows(mem_ref[...], g_ref[...]).astype(BF16)
    o_ref[...] = jnp.dot(h, w_ref[...], preferred_element_type=F32).astype(BF16)


def _mem_kv(mem2d, gain, w_bf16, tm):
    depth, d, n = w_bf16.shape
    rows = mem2d.shape[0]
    return pl.pallas_call(
        _mem_kv_kernel,
        out_shape=jax.ShapeDtypeStruct((depth, rows, n), BF16),
        grid=(depth, rows // tm),
        in_specs=[pl.BlockSpec((tm, d), lambda l, i: (i, 0)),
                  pl.BlockSpec((1, d), lambda l, i: (0, 0)),
                  pl.BlockSpec((None, d, n), lambda l, i: (l, 0, 0))],
        out_specs=pl.BlockSpec((None, tm, n), lambda l, i: (l, i, 0)),
        compiler_params=_params("arbitrary", "arbitrary"),
        name="mem_kv_proj",
    )(mem2d, gain, w_bf16)


def _in_even_kernel(x_ref, g_ref, w_ref, gq_ref, gk_ref, cos_ref, sin_ref, mkv_ref,
                    qa_ref, ka_ref, va_ref, qb_ref, kb_ref, vb_ref, ga_ref, gb_ref, ym_ref,
                    qm_scr, gm_scr):
    h = _rms_rows(x_ref[...], g_ref[...]).astype(BF16)

    def seg(lo, hi):
        return jnp.dot(h, w_ref[:, lo:hi], preferred_element_type=F32)

    ones_bd = _half_ones()
    scale = HEAD_DIM ** -0.5
    qa_w = A_HEADS * HEAD_DIM
    kva_w = A_KV * HEAD_DIM
    qb_w = B_HEADS * HEAD_DIM
    kvb_w = B_KV * HEAD_DIM
    offs = np.cumsum([0, qa_w, kva_w, kva_w, qb_w, kvb_w, kvb_w, MEM_W, qa_w, qb_w, MEM_W])
    o_qa, o_ka, o_va, o_qb, o_kb, o_vb, o_qm, o_ga, o_gb, o_gm = (int(v) for v in offs[:-1])

    def do_qa():
        zq = seg(o_qa, o_qa + qa_w)
        tiles = [_head_norm_rope(zq[:, j * LANES:(j + 1) * LANES], gq_ref[...], cos_ref[...],
                                 sin_ref[...], ones_bd) * (scale * LOG2E) for j in range(qa_w // LANES)]
        _store_padded_heads(tiles, qa_ref, A_HEADS, A_KV, transposed=True)

    def do_ka():
        ka_ref[...] = _head_norm_rope(seg(o_ka, o_ka + kva_w), gk_ref[...], cos_ref[...],
                                      sin_ref[...], ones_bd).astype(BF16)

    def do_va():
        vt = seg(o_va, o_va + kva_w).T.astype(BF16)
        for g in range(A_KV):
            va_ref[g, :HEAD_DIM] = vt[g * HEAD_DIM:(g + 1) * HEAD_DIM]
            va_ref[g, HEAD_DIM:] = jnp.ones((V_ONES_ROWS, vt.shape[1]), BF16)

    def do_qb():
        zq = seg(o_qb, o_qb + qb_w) * (scale * LOG2E)
        _store_padded_heads([zq[:, j * LANES:(j + 1) * LANES] for j in range(qb_w // LANES)],
                            qb_ref, B_HEADS, B_KV, transposed=True)

    def do_kb():
        kb_ref[...] = seg(o_kb, o_kb + kvb_w).astype(BF16)

    def do_vb():
        vt = seg(o_vb, o_vb + kvb_w).T.astype(BF16)
        for j in range(vb_ref.shape[0]):
            for g in range(B_KV):
                vb_ref[j, g, :HEAD_DIM] = vt[g * HEAD_DIM:(g + 1) * HEAD_DIM, j * WINDOW:(j + 1) * WINDOW]
                vb_ref[j, g, HEAD_DIM:] = jnp.ones((V_ONES_ROWS, WINDOW), BF16)

    def do_ga():
        ga_ref[...] = _silu(seg(o_ga, o_ga + qa_w)).astype(BF16)

    def do_gb():
        gb_ref[...] = _silu(seg(o_gb, o_gb + qb_w)).astype(BF16)

    qm_scr[...] = (seg(o_qm, o_qm + MEM_W) * (M_HEAD_DIM ** -0.5 * LOG2E)).astype(BF16)
    gm_scr[...] = _silu(seg(o_gm, o_gm + MEM_W)).astype(BF16)
    _mem_attn_items(qm_scr, mkv_ref, gm_scr, ym_ref,
                    between=[do_ga, do_gb, do_qa, do_qb, do_ka, do_kb, do_va, do_vb])


def _in_even(x2d, gain, w_bf16, gq, gk, cos, sin, mkv, seq, tm):
    n_tok, d = x2d.shape
    n_in = w_bf16.shape[1]
    per_seq = seq // tm
    mlen = mkv.shape[2]
    widths = [A_HEADS * LANES, A_KV * HEAD_DIM, A_KV * HEAD_DIM,
              B_HEADS * LANES, B_KV * HEAD_DIM, B_KV * HEAD_DIM,
              A_HEADS * HEAD_DIM, B_HEADS * HEAD_DIM, MEM_W]
    row = lambda i: (i, 0)
    const = lambda i: (0, 0)
    out_shape = [jax.ShapeDtypeStruct((n_tok, w), BF16) for w in widths]
    out_specs = [pl.BlockSpec((tm, w), row) for w in widths]
    vrows = HEAD_DIM + V_ONES_ROWS
    out_shape[0] = jax.ShapeDtypeStruct((n_tok // tm, A_HEADS, LANES, tm), BF16)
    out_specs[0] = pl.BlockSpec((None, A_HEADS, LANES, tm), lambda i: (i, 0, 0, 0))
    out_shape[3] = jax.ShapeDtypeStruct((n_tok // tm, B_HEADS, LANES, tm), BF16)
    out_specs[3] = pl.BlockSpec((None, B_HEADS, LANES, tm), lambda i: (i, 0, 0, 0))
    out_shape[2] = jax.ShapeDtypeStruct((n_tok // tm, A_KV, vrows, tm), BF16)
    out_specs[2] = pl.BlockSpec((None, A_KV, vrows, tm), lambda i: (i, 0, 0, 0))
    out_shape[5] = jax.ShapeDtypeStruct((n_tok // WINDOW, B_KV, vrows, WINDOW), BF16)
    out_specs[5] = pl.BlockSpec((tm // WINDOW, B_KV, vrows, WINDOW), lambda i: (i, 0, 0, 0))
    return pl.pallas_call(
        _in_even_kernel,
        out_shape=out_shape,
        grid=(n_tok // tm,),
        in_specs=[pl.BlockSpec((tm, d), row),
                  pl.BlockSpec((1, d), const),
                  pl.BlockSpec((d, n_in), const),
                  pl.BlockSpec((1, LANES), const),
                  pl.BlockSpec((1, LANES), const),
                  pl.BlockSpec((tm, LANES), lambda i: (i % per_seq, 0)),
                  pl.BlockSpec((tm, LANES), lambda i: (i % per_seq, 0)),
                  pl.BlockSpec((None, None, mlen, 2 * MEM_W), lambda i: (0, i // per_seq, 0, 0))],
        out_specs=out_specs,
        scratch_shapes=[pltpu.VMEM((tm, MEM_W), BF16), pltpu.VMEM((tm, MEM_W), BF16)],
        compiler_params=_params("arbitrary"),
        name="in_proj_even",
    )(x2d, gain, w_bf16, gq, gk, cos, sin, mkv)


def _mid_kernel(ya_ref, yb_ref, ym_ref, x_ref, wo_ref, g_ref, wi_ref, mkv_ref,
                x1_ref, qc_ref, kc_ref, vc_ref, gc_ref, ym1_ref, qm_scr, gm_scr):
    acc = x_ref[...]
    off = 0
    for y_ref in (ya_ref, yb_ref, ym_ref):
        width = y_ref.shape[1]
        acc = acc + jnp.dot(y_ref[...], wo_ref[off:off + width, :], preferred_element_type=F32)
        off += width
    x1_ref[...] = acc
    h = _rms_rows(acc, g_ref[...]).astype(BF16)

    def seg(lo, hi):
        return jnp.dot(h, wi_ref[:, lo:hi], preferred_element_type=F32)

    cw = C_HEADS * HEAD_DIM
    half = cw // 2
    o_qc, o_kc, o_vc, o_qm, o_gc, o_gm = 0, cw, 2 * cw, 3 * cw, 3 * cw + MEM_W, 4 * cw + MEM_W

    def thunk(ref, base, part, fn):
        def run():
            lo = part * half
            ref[:, lo:lo + half] = fn(seg(base + lo, base + lo + half)).astype(BF16)
        return run

    plain = lambda z: z
    to_q = lambda z: z * (HEAD_DIM ** -0.5 * LOG2E)
    segments = [thunk(ref, base, part, fn)
                for ref, base, fn in ((gc_ref, o_gc, _silu), (qc_ref, o_qc, to_q),
                                      (kc_ref, o_kc, plain), (vc_ref, o_vc, plain))
                for part in range(2)]
    qm_scr[...] = (seg(o_qm, o_qm + MEM_W) * (M_HEAD_DIM ** -0.5 * LOG2E)).astype(BF16)
    gm_scr[...] = _silu(seg(o_gm, o_gm + MEM_W)).astype(BF16)
    _mem_attn_items(qm_scr, mkv_ref, gm_scr, ym1_ref, between=segments)


def _mid(ya, yb, ym, x2d, wo_bf16, gain, wi_bf16, mkv, seq, tm):
    n_tok, d = x2d.shape
    per_seq = seq // tm
    mlen = mkv.shape[2]
    cw = C_HEADS * HEAD_DIM
    row = lambda i: (i, 0)
    const = lambda i: (0, 0)
    once = pl.Buffered(1)
    out_widths = [cw, cw, cw, cw, MEM_W]
    return pl.pallas_call(
        _mid_kernel,
        out_shape=[jax.ShapeDtypeStruct((n_tok, d), F32)]
                  + [jax.ShapeDtypeStruct((n_tok, w), BF16) for w in out_widths],
        grid=(n_tok // tm,),
        in_specs=[pl.BlockSpec((tm, ya.shape[1]), row),
                  pl.BlockSpec((tm, yb.shape[1]), row),
                  pl.BlockSpec((tm, ym.shape[1]), row),
                  pl.BlockSpec((tm, d), row),
                  pl.BlockSpec(wo_bf16.shape, const, pipeline_mode=once),
                  pl.BlockSpec((1, d), const),
                  pl.BlockSpec(wi_bf16.shape, const, pipeline_mode=once),
                  pl.BlockSpec((None, None, mlen, 2 * MEM_W), lambda i: (1, i // per_seq, 0, 0))],
        out_specs=[pl.BlockSpec((tm, d), row)] + [pl.BlockSpec((tm, w), row) for w in out_widths],
        scratch_shapes=[pltpu.VMEM((tm, MEM_W), BF16), pltpu.VMEM((tm, MEM_W), BF16)],
        compiler_params=_params("arbitrary"),
        name="out_proj_in_proj_odd",
    )(ya, yb, ym, x2d, wo_bf16, gain, wi_bf16, mkv)


def _global_attn_kernel(q_ref, k_ref, vt_ref, gate_ref, o_ref, m_scr, acc_scr, s_scr, mc_scr):
    tq = q_ref.shape[2]
    n_chunks = vt_ref.shape[0]
    tk = vt_ref.shape[3]
    group = A_HEADS // A_KV
    qts = [jnp.concatenate([q_ref[g * group + i] for i in range(group)], axis=1)
           for g in range(A_KV)]
    m_scr[...] = jnp.full(m_scr.shape, NEG, F32)
    acc_scr[...] = jnp.zeros(acc_scr.shape, F32)

    parts = 2
    tp = tk // parts
    pieces = [(p, g) for p in range(parts) for g in range(A_KV)]

    def scores_piece(c, slot, p, g):
        start = pl.multiple_of(c * tk + p * tp, tp)
        st = jnp.dot(k_ref[pl.ds(start, tp), :], qts[g], preferred_element_type=F32)
        s_scr[slot, g, p * tp:(p + 1) * tp] = st
        mx = jnp.max(st, axis=0, keepdims=True)
        mc_scr[slot, g] = mx if p == 0 else jnp.maximum(mc_scr[slot, g], mx)

    def accumulate_piece(c, slot, p, g):
        m_old = m_scr[g]
        m_new = jnp.maximum(m_old, mc_scr[slot, g])
        pt = jnp.exp2(s_scr[slot, g, p * tp:(p + 1) * tp] - m_new).astype(BF16)
        pv = jnp.dot(vt_ref[c, g][:, p * tp:(p + 1) * tp], pt, preferred_element_type=F32)
        if p == 0:
            acc_scr[g] = jnp.exp2(m_old - m_new) * acc_scr[g] + pv
        else:
            acc_scr[g] = acc_scr[g] + pv
        if p == parts - 1:
            m_scr[g] = m_new

    def scores(c, slot):
        for p, g in pieces:
            scores_piece(c, slot, p, g)

    def accumulate(c, slot):
        for p, g in pieces:
            accumulate_piece(c, slot, p, g)

    def scores_and_accumulate(c_s, slot_s, c_a, slot_a):
        for p, g in pieces:
            scores_piece(c_s, slot_s, p, g)
            accumulate_piece(c_a, slot_a, p, g)

    scores(0, 0)

    def body(c2, carry):
        c = 2 * c2
        scores_and_accumulate(c + 1, 1, c, 0)
        scores_and_accumulate(c + 2, 0, c + 1, 1)
        return carry

    lax.fori_loop(0, n_chunks // 2 - 1, body, 0)
    scores_and_accumulate(n_chunks - 1, 1, n_chunks - 2, 0)
    accumulate(n_chunks - 1, 1)

    ot = [acc_scr[g, :HEAD_DIM] * (1.0 / acc_scr[g, HEAD_DIM:HEAD_DIM + 1]) for g in range(A_KV)]
    for j in range(A_HEADS // 2):
        g, i0 = (2 * j) // group, (2 * j) % group
        tile_t = jnp.concatenate([ot[g][:, i0 * tq:(i0 + 1) * tq],
                                  ot[g][:, (i0 + 1) * tq:(i0 + 2) * tq]], axis=0)
        gate = gate_ref[:, j * LANES:(j + 1) * LANES].astype(F32)
        o_ref[:, j * LANES:(j + 1) * LANES] = (tile_t.T * gate).astype(BF16)


def _global_attn(qat, ka, vat, gate, tq):
    b, s, _ = ka.shape
    _, n_kv, vrows, tk = vat.shape
    tm = qat.shape[-1]
    n_chunks = s // tk
    assert n_chunks % 2 == 0 and tm % tq == 0
    rows = (A_HEADS // A_KV) * tq
    vat = vat.reshape(b, n_chunks, n_kv, vrows, tk)
    qat = qat.reshape(b, s // tm, A_HEADS, LANES, tm)
    per_slab = tm // tq
    return pl.pallas_call(
        _global_attn_kernel,
        out_shape=jax.ShapeDtypeStruct((b, s, A_HEADS * HEAD_DIM), BF16),
        grid=(b, s // tq),
        in_specs=[pl.BlockSpec((None, None, A_HEADS, LANES, tq),
                               lambda bi, i: (bi, i // per_slab, 0, 0, i % per_slab)),
                  pl.BlockSpec((None, s, LANES), lambda bi, i: (bi, 0, 0)),
                  pl.BlockSpec((None, n_chunks, n_kv, vrows, tk), lambda bi, i: (bi, 0, 0, 0, 0)),
                  pl.BlockSpec((None, tq, A_HEADS * HEAD_DIM), lambda bi, i: (bi, i, 0))],
        out_specs=pl.BlockSpec((None, tq, A_HEADS * HEAD_DIM), lambda bi, i: (bi, i, 0)),
        scratch_shapes=[pltpu.VMEM((A_KV, 1, rows), F32),
                        pltpu.VMEM((A_KV, vrows, rows), F32),
                        pltpu.VMEM((2, A_KV, tk, rows), F32), pltpu.VMEM((2, A_KV, 1, rows), F32)],
        compiler_params=_params("arbitrary", "arbitrary"),
        name="global_attn",
    )(qat, ka, vat, gate)


def _window_attn_kernel(q_ref, k_ref, vt_ref, bias_ref, sink_ref, gate_ref, o_ref, *, blk):
    blocks_per_step = q_ref.shape[2] // blk
    nb = vt_ref.shape[0]
    step = pl.program_id(1)
    group = B_HEADS // B_KV

    def neighbours(t):
        i = step * blocks_per_step + t
        return jnp.maximum(i - 1, 0), i, jnp.minimum(i + 1, nb - 1)

    def scores(t, g):
        i = step * blocks_per_step + t
        qt = jnp.concatenate([q_ref[g * group + j][:, t * blk:(t + 1) * blk]
                              for j in range(group)], axis=1)
        k = jnp.concatenate([k_ref[pl.ds(pl.multiple_of(n * blk, blk), blk), :]
                             for n in neighbours(t)], axis=0)
        st = jnp.dot(k, qt, preferred_element_type=F32) + bias_ref[g]
        if t == 0:
            st = jnp.concatenate([st[:blk] + jnp.where(i == 0, NEG, 0.0), st[blk:]], axis=0)
        if t == blocks_per_step - 1:
            st = jnp.concatenate([st[:2 * blk], st[2 * blk:] + jnp.where(i == nb - 1, NEG, 0.0)], axis=0)
        m = jnp.maximum(jnp.max(st, axis=0, keepdims=True), sink_ref[g])
        return st, m

    def probs(st, m):
        return jnp.exp2(st - m).astype(BF16), m

    def output(t, g, pt, m):
        vt = jnp.concatenate([vt_ref[n, g] for n in neighbours(t)], axis=1)
        ot = jnp.dot(vt, pt, preferred_element_type=F32)
        denom = ot[HEAD_DIM:HEAD_DIM + 1] + jnp.exp2(sink_ref[g] - m)
        return ot[:HEAD_DIM] * (1.0 / denom)

    def store(t, ot):
        rows = slice(t * blk, (t + 1) * blk)
        for j in range(B_HEADS // 2):
            g, i0 = (2 * j) // group, (2 * j) % group
            tile_t = jnp.concatenate([ot[g][:, i0 * blk:(i0 + 1) * blk],
                                      ot[g][:, (i0 + 1) * blk:(i0 + 2) * blk]], axis=0)
            gate = gate_ref[rows, j * LANES:(j + 1) * LANES].astype(F32)
            o_ref[rows, j * LANES:(j + 1) * LANES] = (tile_t.T * gate).astype(BF16)

    items = [(t, g) for t in range(blocks_per_step) for g in range(B_KV)]
    sm, pb, outs = {}, {}, {}
    for n in range(len(items) + 2):
        if n < len(items):
            sm[n] = scores(*items[n])
        if 1 <= n <= len(items):
            pb[n - 1] = probs(*sm.pop(n - 1))
        if n >= 2:
            t, g = items[n - 2]
            outs[g] = output(t, g, *pb.pop(n - 2))
            if g == B_KV - 1:
                store(t, outs)


def _window_attn(qbt, kb, vbt, bias_t, sink_cols, gate, blk, blocks_per_step):
    b, s, _ = kb.shape
    nb = s // blk
    rows = (B_HEADS // B_KV) * blk
    tq = blk * blocks_per_step
    tm = qbt.shape[-1]
    assert tm % tq == 0
    per_slab = tm // tq
    vrows = vbt.shape[2]
    vbt = vbt.reshape(b, nb, B_KV, vrows, blk)
    qbt = qbt.reshape(b, s // tm, B_HEADS, LANES, tm)
    return pl.pallas_call(
        functools.partial(_window_attn_kernel, blk=blk),
        out_shape=jax.ShapeDtypeStruct((b, s, B_HEADS * HEAD_DIM), BF16),
        grid=(b, s // tq),
        in_specs=[pl.BlockSpec((None, None, B_HEADS, LANES, tq),
                               lambda bi, i: (bi, i // per_slab, 0, 0, i % per_slab)),
                  pl.BlockSpec((None, s, LANES), lambda bi, i: (bi, 0, 0)),
                  pl.BlockSpec((None, nb, B_KV, vrows, blk), lambda bi, i: (bi, 0, 0, 0, 0)),
                  pl.BlockSpec((B_KV, 3 * blk, rows), lambda bi, i: (0, 0, 0)),
                  pl.BlockSpec((B_KV, 1, rows), lambda bi, i: (0, 0, 0)),
                  pl.BlockSpec((None, tq, B_HEADS * HEAD_DIM), lambda bi, i: (bi, i, 0))],
        out_specs=pl.BlockSpec((None, tq, B_HEADS * HEAD_DIM), lambda bi, i: (bi, i, 0)),
        compiler_params=_params("arbitrary", "arbitrary"),
        name="window_attn",
    )(qbt, kb, vbt, bias_t, sink_cols, gate)


def _mem_attn_items(q_ref, kv_ref, gate_ref, o_ref, between=(), sub=256):
    between = list(between)
    mlen = kv_ref.shape[0]
    ones = jnp.ones((mlen, M_HEAD_DIM), BF16)

    def scores(t, h):
        lo, hi = h * M_HEAD_DIM, (h + 1) * M_HEAD_DIM
        s = lax.dot_general(q_ref[t * sub:(t + 1) * sub, lo:hi], kv_ref[:, lo:hi], _NT,
                            preferred_element_type=F32)
        return s, jnp.max(s, axis=-1, keepdims=True)

    def probs(s, m):
        return jnp.exp2(s - m).astype(BF16)

    def output(t, h, p):
        lo, hi = h * M_HEAD_DIM, (h + 1) * M_HEAD_DIM
        v = jnp.concatenate([kv_ref[:, MEM_W + lo:MEM_W + hi], ones], axis=1)
        o = jnp.dot(p, v, preferred_element_type=F32)
        o = o[:, :M_HEAD_DIM] * (1.0 / o[:, M_HEAD_DIM:])
        rows = slice(t * sub, (t + 1) * sub)
        o_ref[rows, lo:hi] = (o * gate_ref[rows, lo:hi].astype(F32)).astype(BF16)

    items = [(t, h) for t in range(q_ref.shape[0] // sub) for h in range(M_HEADS)]
    sm, pb = {}, {}
    for n in range(len(items) + 2):
        if n < len(items):
            sm[n] = scores(*items[n])
        if 1 <= n <= len(items):
            pb[n - 1] = probs(*sm.pop(n - 1))
        if n >= 2:
            output(*items[n - 2], pb.pop(n - 2))
        if between:
            between.pop(0)()
    for run in between:
        run()


def _nbr_attn_kernel(q_ref, k_ref, v_ref, bias_ref, gate_ref, o_ref, *, grid_rows, rows_per_step):
    rb = pl.program_id(2)
    nkeys = NA_ROWS * GRID_W
    n_pairs = q_ref.shape[1] // LANES
    ones = jnp.ones((nkeys, LANES), BF16)
    starts, deltas = [], []
    for t in range(rows_per_step):
        r = rb * rows_per_step + t
        rs = jnp.clip(r - NA_ROWS // 2, 0, grid_rows - NA_ROWS)
        starts.append(pl.multiple_of(rs * GRID_W, GRID_W))
        deltas.append(r - rs)
    items = [(t, j) for t in range(rows_per_step) for j in range(n_pairs)]

    def scores(t, j):
        lo, hi = j * LANES, (j + 1) * LANES
        qt = q_ref[t * GRID_W:(t + 1) * GRID_W, lo:hi]
        lane = _lane_iota(qt.shape)
        zero = jnp.zeros_like(qt)
        q = jnp.concatenate([jnp.where(lane < HEAD_DIM, qt, zero),
                             jnp.where(lane >= HEAD_DIM, qt, zero)], axis=0)
        k = k_ref[pl.ds(starts[t], nkeys), lo:hi]
        s = lax.dot_general(q, k, _NT, preferred_element_type=F32) + bias_ref[deltas[t], j]
        return s, jnp.max(s, axis=-1, keepdims=True)

    def probs(s, m):
        return jnp.exp2(s - m).astype(BF16)

    def output(t, j, p):
        lo, hi = j * LANES, (j + 1) * LANES
        v = jnp.concatenate([v_ref[pl.ds(starts[t], nkeys), lo:hi], ones], axis=1)
        o = jnp.dot(p, v, preferred_element_type=F32)
        o = o[:, :LANES] * (1.0 / o[:, LANES:])
        lane_o = _lane_iota((GRID_W, LANES))
        tile = jnp.where(lane_o < HEAD_DIM, o[:GRID_W, :], o[GRID_W:, :])
        rows = slice(t * GRID_W, (t + 1) * GRID_W)
        o_ref[rows, lo:hi] = (tile * gate_ref[rows, lo:hi].astype(F32)).astype(BF16)

    sm = {}
    pb = {}
    for n in range(len(items) + 2):
        if n < len(items):
            sm[n] = scores(*items[n])
        if 1 <= n <= len(items):
            pb[n - 1] = probs(*sm.pop(n - 1))
        if n >= 2:
            output(*items[n - 2], pb.pop(n - 2))


def _nbr_attn(qc, kc, vc, bias, gate, rows_per_step, head_splits):
    b, s, w = qc.shape
    grid_rows = s // GRID_W
    nkeys = NA_ROWS * GRID_W
    wh = w // head_splits
    tq = rows_per_step * GRID_W
    blk = lambda hh, bi, rb: (bi, rb, hh)
    return pl.pallas_call(
        functools.partial(_nbr_attn_kernel, grid_rows=grid_rows, rows_per_step=rows_per_step),
        out_shape=jax.ShapeDtypeStruct((b, s, w), BF16),
        grid=(head_splits, b, grid_rows // rows_per_step),
        in_specs=[pl.BlockSpec((None, tq, wh), blk),
                  pl.BlockSpec((None, s, wh), lambda hh, bi, rb: (bi, 0, hh)),
                  pl.BlockSpec((None, s, wh), lambda hh, bi, rb: (bi, 0, hh)),
                  pl.BlockSpec((NA_ROWS, wh // LANES, 2 * GRID_W, nkeys), lambda hh, bi, rb: (0, hh, 0, 0)),
                  pl.BlockSpec((None, tq, wh), blk)],
        out_specs=pl.BlockSpec((None, tq, wh), blk),
        compiler_params=_params("arbitrary", "arbitrary", "arbitrary"),
        name="nbr_attn",
    )(qc, kc, vc, bias, gate)


def _out_final_kernel(yc_ref, ym_ref, x_ref, w_ref, g_ref, o_ref):
    acc = x_ref[...]
    off = 0
    for y_ref in (yc_ref, ym_ref):
        width = y_ref.shape[1]
        acc = acc + jnp.dot(y_ref[...], w_ref[off:off + width, :], preferred_element_type=F32)
        off += width
    o_ref[...] = _rms_rows(acc, g_ref[...])


def _out_final(yc, ym, x2d, w_bf16, final_gain, tm):
    n_tok, d = x2d.shape
    row = lambda i: (i, 0)
    const = lambda i: (0, 0)
    return pl.pallas_call(
        _out_final_kernel,
        out_shape=jax.ShapeDtypeStruct((n_tok, d), F32),
        grid=(n_tok // tm,),
        in_specs=[pl.BlockSpec((tm, yc.shape[1]), row),
                  pl.BlockSpec((tm, ym.shape[1]), row),
                  pl.BlockSpec((tm, d), row),
                  pl.BlockSpec(w_bf16.shape, const),
                  pl.BlockSpec((1, d), const)],
        out_specs=pl.BlockSpec((tm, d), row),
        compiler_params=_params("arbitrary"),
        name="out_proj_final",
    )(yc, ym, x2d, w_bf16, final_gain)


def _rope_tables(seq):
    quarter = HEAD_DIM // 4
    freqs = jnp.power(ROPE_THETA, -jnp.arange(quarter, dtype=F32) / quarter)
    t = jnp.arange(seq)
    ang_r = (t // GRID_W).astype(F32)[:, None] * freqs
    ang_c = (t % GRID_W).astype(F32)[:, None] * freqs
    cos_h = jnp.concatenate([jnp.cos(ang_r), jnp.cos(ang_r), jnp.cos(ang_c), jnp.cos(ang_c)], axis=-1)
    sin_h = jnp.concatenate([-jnp.sin(ang_r), jnp.sin(ang_r), -jnp.sin(ang_c), jnp.sin(ang_c)], axis=-1)
    return jnp.tile(cos_h, (1, 2)), jnp.tile(sin_h, (1, 2))


def _t5_bucket(rel):
    nb = REL_BUCKETS // 2
    max_exact = nb // 2
    ret = jnp.where(rel > 0, nb, 0)
    n = jnp.abs(rel)
    nf = jnp.maximum(n, 1).astype(F32)
    large = max_exact + (jnp.log(nf / max_exact) / math.log(REL_MAX_DIST / max_exact)
                         * (nb - max_exact)).astype(jnp.int32)
    large = jnp.minimum(large, nb - 1)
    return ret + jnp.where(n < max_exact, n, large)


def _window_bias(rel_bias, blk):
    span = blk + 2 * WINDOW
    reach = span - WINDOW - 1
    rel = jnp.arange(-reach, reach + 1)
    per_rel = jnp.where((jnp.abs(rel) <= WINDOW)[:, None],
                        rel_bias.astype(F32)[_t5_bucket(rel)] * LOG2E, NEG).T
    bias = jnp.stack([per_rel[:, reach - WINDOW - q:reach - WINDOW - q + span] for q in range(blk)],
                     axis=-1)
    group = B_HEADS // B_KV
    bias = bias.reshape(B_KV, group, span, blk).transpose(0, 2, 1, 3)
    return bias.reshape(B_KV, span, group * blk)


def _nbr_bias(rpb):
    col = np.arange(GRID_W)
    cs = np.clip(col - NA_COLS // 2, 0, GRID_W - NA_COLS)
    colmask = (col[None, :] >= cs[:, None]) & (col[None, :] < cs[:, None] + NA_COLS)
    dc = np.clip(col[None, :] - col[:, None] + NA_COLS - 1, 0, 2 * NA_COLS - 2)
    onehot = (dc[None] == np.arange(2 * NA_COLS - 1)[:, None, None]) & colmask[None]
    t = jnp.einsum('hrc,cqk->hqrk', rpb.astype(F32) * LOG2E, jnp.asarray(onehot, F32),
                   precision=lax.Precision.HIGHEST)
    t = t + jnp.asarray(np.where(colmask, 0.0, NEG), F32)[None, :, None, :]
    delta, i = np.meshgrid(np.arange(NA_ROWS), np.arange(NA_ROWS), indexing="ij")
    pick = (i - delta + NA_ROWS - 1)[:, :, None] == np.arange(2 * NA_ROWS - 1)
    out = jnp.einsum('dir,hqrk->dhqik', jnp.asarray(pick, F32), t, precision=lax.Precision.HIGHEST)
    return out.reshape(NA_ROWS, C_HEADS // 2, 2 * GRID_W, NA_ROWS * GRID_W)


def kernel(x, mem, norm_gain, mem_norm_gain, w_in_even, w_out_even, q_norm_a, k_norm_a, sink_b,
           rel_bias, w_in_odd, w_out_odd, rpb_c, w_mem_kv, final_norm_gain):
    b, s, d = x.shape
    mlen = mem.shape[1]
    assert s % GRID_W == 0 and s // GRID_W >= NA_ROWS and s % 512 == 0
    tm = 512
    blk = 128

    x2d = x.reshape(b * s, d)
    mem_tm = math.gcd(b * mlen, 512)
    mkv = _mem_kv(mem.reshape(b * mlen, d), mem_norm_gain.reshape(1, d), w_mem_kv.astype(BF16), mem_tm)
    mkv = mkv.reshape(w_mem_kv.shape[0], b, mlen, 2 * MEM_W)

    cos, sin = _rope_tables(s)
    gq = jnp.tile(q_norm_a[0].astype(F32), 2).reshape(1, LANES)
    gk = jnp.tile(k_norm_a[0].astype(F32), 2).reshape(1, LANES)
    qa, ka, va, qb, kb, vb, ga, gb, ym = _in_even(
        x2d, norm_gain[0].reshape(1, d), w_in_even[0].astype(BF16), gq, gk, cos, sin, mkv, s, tm)
    r3 = lambda a: a.reshape(b, s, a.shape[-1])
    ya = _global_attn(qa, r3(ka), va, r3(ga), tq=256)
    group_b = B_HEADS // B_KV
    sink_cols = jnp.repeat(sink_b[0].astype(F32).reshape(B_KV, group_b) * LOG2E, blk,
                           axis=1).reshape(B_KV, 1, group_b * blk)
    yb = _window_attn(qb, r3(kb), vb, _window_bias(rel_bias, blk), sink_cols, r3(gb), blk,
                      blocks_per_step=4)

    x1, qc, kc, vc, gc, ym1 = _mid(ya.reshape(b * s, -1), yb.reshape(b * s, -1), ym, x2d,
                                   w_out_even[0].astype(BF16), norm_gain[1].reshape(1, d),
                                   w_in_odd[0].astype(BF16), mkv, s, tm)
    yc = _nbr_attn(r3(qc), r3(kc), r3(vc), _nbr_bias(rpb_c[0]), r3(gc), rows_per_step=8, head_splits=2)
    out = _out_final(yc.reshape(b * s, -1), ym1, x1, w_out_odd[0].astype(BF16),
                     final_norm_gain.reshape(1, d), tm)
    return out.reshape(b, s, d)
```

```python
import functools
import math

import jax
import jax.numpy as jnp
import numpy as np
from jax import lax
from jax.experimental import pallas as pl
from jax.experimental.pallas import tpu as pltpu

GRID_W = 64
HEAD_DIM = 64
A_HEADS = 8
A_KV = 2
B_HEADS = 8
B_KV = 2
WINDOW = 128
C_HEADS = 16
NA_ROWS = 8
NA_COLS = 16
M_HEADS = 4
M_HEAD_DIM = 128
MEM_W = M_HEADS * M_HEAD_DIM
REL_BUCKETS = 32
REL_MAX_DIST = 128
ROPE_THETA = 10000.0
EPS = 1e-6

LANES = 128
NEG = -1e30
LOG2E = math.log2(math.e)
V_ONES_ROWS = 16
VMEM_LIMIT_BYTES = 56 * 1024 * 1024

F32 = jnp.float32
BF16 = jnp.bfloat16

_NT = (((1,), (1,)), ((), ()))


def _params(*sem):
    return pltpu.CompilerParams(dimension_semantics=sem, vmem_limit_bytes=VMEM_LIMIT_BYTES)


def _rms_rows(x, gain):
    ms = jnp.mean(x * x, axis=-1, keepdims=True)
    return x * lax.rsqrt(ms + EPS) * gain


def _lane_iota(shape):
    return lax.broadcasted_iota(jnp.int32, shape, len(shape) - 1)


def _silu(x):
    return x * (1.0 / (1.0 + jnp.exp(-x)))


def _half_ones():
    r = lax.broadcasted_iota(jnp.int32, (LANES, LANES), 0) // HEAD_DIM
    c = lax.broadcasted_iota(jnp.int32, (LANES, LANES), 1) // HEAD_DIM
    return jnp.where(r == c, 1.0, 0.0).astype(BF16)


def _head_norm_rope(t, gain, cos, sin_signed, ones_bd):
    ss = jnp.dot((t * t).astype(BF16), ones_bd, preferred_element_type=F32)
    tn = t * lax.rsqrt(ss * (1.0 / HEAD_DIM) + EPS) * gain
    lane = _lane_iota(tn.shape)
    quarter = HEAD_DIM // 4
    partner = jnp.where((lane % (2 * quarter)) < quarter,
                        pltpu.roll(tn, LANES - quarter, 1), pltpu.roll(tn, quarter, 1))
    return tn * cos + partner * sin_signed


def _store_padded_heads(q_tiles, out_ref, n_heads, n_kv, transposed=False):
    group = n_heads // n_kv
    for h in range(n_heads):
        t = q_tiles[h // 2]
        src_half = h % 2
        dst_half = (h // group) % 2
        if src_half != dst_half:
            t = pltpu.roll(t, HEAD_DIM, 1)
        lane = _lane_iota(t.shape)
        keep = (lane >= HEAD_DIM) if dst_half == 1 else (lane < HEAD_DIM)
        padded = jnp.where(keep, t, 0.0)
        if transposed:
            out_ref[h] = padded.T.astype(BF16)
        else:
            out_ref[:, h * LANES:(h + 1) * LANES] = padded.astype(BF16)


def _mem_kv_kernel(mem_ref, g_ref, w_ref, o_ref):
    h = _rms_rows(mem_ref[...], g_ref[...]).astype(BF16)
    o_ref[...] = jnp.dot(h, w_ref[...], preferred_element_type=F32).astype(BF16)


def _mem_kv(mem2d, gain, w_bf16, tm):
    depth, d, n = w_bf16.shape
    rows = mem2d.shape[0]
    return pl.pallas_call(
        _mem_kv_kernel,
        out_shape=jax.ShapeDtypeStruct((depth, rows, n), BF16),
        grid=(depth, rows // tm),
        in_specs=[pl.BlockSpec((tm, d), lambda l, i: (i, 0)),
                  pl.BlockSpec((1, d), lambda l, i: (0, 0)),
                  pl.BlockSpec((None, d, n), lambda l, i: (l, 0, 0))],
        out_specs=pl.BlockSpec((None, tm, n), lambda l, i: (l, i, 0)),
        compiler_params=_params("arbitrary", "arbitrary"),
        name="mem_kv_proj",
    )(mem2d, gain, w_bf16)


def _in_even_kernel(x_ref, g_ref, w_ref, gq_ref, gk_ref, cos_ref, sin_ref, mkv_ref,
                    qa_ref, ka_ref, va_ref, qb_ref, kb_ref, vb_ref, ga_ref, gb_ref, ym_ref,
                    qm_scr, gm_scr):
    h = _rms_rows(x_ref[...], g_ref[...]).astype(BF16)

    def seg(lo, hi):
        return jnp.dot(h, w_ref[:, lo:hi], preferred_element_type=F32)

    ones_bd = _half_ones()
    scale = HEAD_DIM ** -0.5
    qa_w = A_HEADS * HEAD_DIM
    kva_w = A_KV * HEAD_DIM
    qb_w = B_HEADS * HEAD_DIM
    kvb_w = B_KV * HEAD_DIM
    offs = np.cumsum([0, qa_w, kva_w, kva_w, qb_w, kvb_w, kvb_w, MEM_W, qa_w, qb_w, MEM_W])
    o_qa, o_ka, o_va, o_qb, o_kb, o_vb, o_qm, o_ga, o_gb, o_gm = (int(v) for v in offs[:-1])

    def do_qa():
        zq = seg(o_qa, o_qa + qa_w)
        tiles = [_head_norm_rope(zq[:, j * LANES:(j + 1) * LANES], gq_ref[...], cos_ref[...],
                                 sin_ref[...], ones_bd) * (scale * LOG2E) for j in range(qa_w // LANES)]
        _store_padded_heads(tiles, qa_ref, A_HEADS, A_KV, transposed=True)

    def do_ka():
        ka_ref[...] = _head_norm_rope(seg(o_ka, o_ka + kva_w), gk_ref[...], cos_ref[...],
                                      sin_ref[...], ones_bd).astype(BF16)

    def do_va():
        vt = seg(o_va, o_va + kva_w).T.astype(BF16)
        for g in range(A_KV):
            va_ref[g, :HEAD_DIM] = vt[g * HEAD_DIM:(g + 1) * HEAD_DIM]
            va_ref[g, HEAD_DIM:] = jnp.ones((V_ONES_ROWS, vt.shape[1]), BF16)

    def do_qb():
        zq = seg(o_qb, o_qb + qb_w) * (scale * LOG2E)
        _store_padded_heads([zq[:, j * LANES:(j + 1) * LANES] for j in range(qb_w // LANES)],
                            qb_ref, B_HEADS, B_KV, transposed=True)

    def do_kb():
        kb_ref[...] = seg(o_kb, o_kb + kvb_w).astype(BF16)

    def do_vb():
        vt = seg(o_vb, o_vb + kvb_w).T.astype(BF16)
        for j in range(vb_ref.shape[0]):
            for g in range(B_KV):
                vb_ref[j, g, :HEAD_DIM] = vt[g * HEAD_DIM:(g + 1) * HEAD_DIM, j * WINDOW:(j + 1) * WINDOW]
                vb_ref[j, g, HEAD_DIM:] = jnp.ones((V_ONES_ROWS, WINDOW), BF16)

    def do_ga():
        ga_ref[...] = _silu(seg(o_ga, o_ga + qa_w)).astype(BF16)

    def do_gb():
        gb_ref[...] = _silu(seg(o_gb, o_gb + qb_w)).astype(BF16)

    qm_scr[...] = (seg(o_qm, o_qm + MEM_W) * (M_HEAD_DIM ** -0.5 * LOG2E)).astype(BF16)
    gm_scr[...] = _silu(seg(o_gm, o_gm + MEM_W)).astype(BF16)
    _mem_attn_items(qm_scr, mkv_ref, gm_scr, ym_ref,
                    between=[do_qa, do_ga, do_qb, do_gb, do_va, do_ka, do_vb, do_kb])


def _in_even(x2d, gain, w_bf16, gq, gk, cos, sin, mkv, seq, tm):
    n_tok, d = x2d.shape
    n_in = w_bf16.shape[1]
    per_seq = seq // tm
    mlen = mkv.shape[2]
    widths = [A_HEADS * LANES, A_KV * HEAD_DIM, A_KV * HEAD_DIM,
              B_HEADS * LANES, B_KV * HEAD_DIM, B_KV * HEAD_DIM,
              A_HEADS * HEAD_DIM, B_HEADS * HEAD_DIM, MEM_W]
    row = lambda i: (i, 0)
    const = lambda i: (0, 0)
    out_shape = [jax.ShapeDtypeStruct((n_tok, w), BF16) for w in widths]
    out_specs = [pl.BlockSpec((tm, w), row) for w in widths]
    vrows = HEAD_DIM + V_ONES_ROWS
    out_shape[0] = jax.ShapeDtypeStruct((n_tok // tm, A_HEADS, LANES, tm), BF16)
    out_specs[0] = pl.BlockSpec((None, A_HEADS, LANES, tm), lambda i: (i, 0, 0, 0))
    out_shape[3] = jax.ShapeDtypeStruct((n_tok // tm, B_HEADS, LANES, tm), BF16)
    out_specs[3] = pl.BlockSpec((None, B_HEADS, LANES, tm), lambda i: (i, 0, 0, 0))
    out_shape[2] = jax.ShapeDtypeStruct((n_tok // tm, A_KV, vrows, tm), BF16)
    out_specs[2] = pl.BlockSpec((None, A_KV, vrows, tm), lambda i: (i, 0, 0, 0))
    out_shape[5] = jax.ShapeDtypeStruct((n_tok // WINDOW, B_KV, vrows, WINDOW), BF16)
    out_specs[5] = pl.BlockSpec((tm // WINDOW, B_KV, vrows, WINDOW), lambda i: (i, 0, 0, 0))
    return pl.pallas_call(
        _in_even_kernel,
        out_shape=out_shape,
        grid=(n_tok // tm,),
        in_specs=[pl.BlockSpec((tm, d), row),
                  pl.BlockSpec((1, d), const),
                  pl.BlockSpec((d, n_in), const),
                  pl.BlockSpec((1, LANES), const),
                  pl.BlockSpec((1, LANES), const),
                  pl.BlockSpec((tm, LANES), lambda i: (i % per_seq, 0)),
                  pl.BlockSpec((tm, LANES), lambda i: (i % per_seq, 0)),
                  pl.BlockSpec((None, None, mlen, 2 * MEM_W), lambda i: (0, i // per_seq, 0, 0))],
        out_specs=out_specs,
        scratch_shapes=[pltpu.VMEM((tm, MEM_W), BF16), pltpu.VMEM((tm, MEM_W), BF16)],
        compiler_params=_params("arbitrary"),
        name="in_proj_even",
    )(x2d, gain, w_bf16, gq, gk, cos, sin, mkv)


def _mid_kernel(ya_ref, yb_ref, ym_ref, x_ref, wo_ref, g_ref, wi_ref, mkv_ref,
                x1_ref, qc_ref, kc_ref, vc_ref, gc_ref, ym1_ref, qm_scr, gm_scr):
    acc = x_ref[...]
    off = 0
    for y_ref in (ya_ref, yb_ref, ym_ref):
        width = y_ref.shape[1]
        acc = acc + jnp.dot(y_ref[...], wo_ref[off:off + width, :], preferred_element_type=F32)
        off += width
    x1_ref[...] = acc
    h = _rms_rows(acc, g_ref[...]).astype(BF16)

    def seg(lo, hi):
        return jnp.dot(h, wi_ref[:, lo:hi], preferred_element_type=F32)

    cw = C_HEADS * HEAD_DIM
    half = cw // 2
    o_qc, o_kc, o_vc, o_qm, o_gc, o_gm = 0, cw, 2 * cw, 3 * cw, 3 * cw + MEM_W, 4 * cw + MEM_W

    def thunk(ref, base, part, fn):
        def run():
            lo = part * half
            ref[:, lo:lo + half] = fn(seg(base + lo, base + lo + half)).astype(BF16)
        return run

    plain = lambda z: z
    to_q = lambda z: z * (HEAD_DIM ** -0.5 * LOG2E)
    segments = [thunk(ref, base, part, fn)
                for ref, base, fn in ((gc_ref, o_gc, _silu), (qc_ref, o_qc, to_q),
                                      (kc_ref, o_kc, plain), (vc_ref, o_vc, plain))
                for part in range(2)]
    qm_scr[...] = (seg(o_qm, o_qm + MEM_W) * (M_HEAD_DIM ** -0.5 * LOG2E)).astype(BF16)
    gm_scr[...] = _silu(seg(o_gm, o_gm + MEM_W)).astype(BF16)
    _mem_attn_items(qm_scr, mkv_ref, gm_scr, ym1_ref, between=segments)


def _mid(ya, yb, ym, x2d, wo_bf16, gain, wi_bf16, mkv, seq, tm):
    n_tok, d = x2d.shape
    per_seq = seq // tm
    mlen = mkv.shape[2]
    cw = C_HEADS * HEAD_DIM
    row = lambda i: (i, 0)
    const = lambda i: (0, 0)
    once = pl.Buffered(1)
    out_widths = [cw, cw, cw, cw, MEM_W]
    return pl.pallas_call(
        _mid_kernel,
        out_shape=[jax.ShapeDtypeStruct((n_tok, d), F32)]
                  + [jax.ShapeDtypeStruct((n_tok, w), BF16) for w in out_widths],
        grid=(n_tok // tm,),
        in_specs=[pl.BlockSpec((tm, ya.shape[1]), row),
                  pl.BlockSpec((tm, yb.shape[1]), row),
                  pl.BlockSpec((tm, ym.shape[1]), row),
                  pl.BlockSpec((tm, d), row),
                  pl.BlockSpec(wo_bf16.shape, const, pipeline_mode=once),
                  pl.BlockSpec((1, d), const),
                  pl.BlockSpec(wi_bf16.shape, const, pipeline_mode=once),
                  pl.BlockSpec((None, None, mlen, 2 * MEM_W), lambda i: (1, i // per_seq, 0, 0))],
        out_specs=[pl.BlockSpec((tm, d), row)] + [pl.BlockSpec((tm, w), row) for w in out_widths],
        scratch_shapes=[pltpu.VMEM((tm, MEM_W), BF16), pltpu.VMEM((tm, MEM_W), BF16)],
        compiler_params=_params("arbitrary"),
        name="out_proj_in_proj_odd",
    )(ya, yb, ym, x2d, wo_bf16, gain, wi_bf16, mkv)


def _global_attn_kernel(q_ref, k_ref, vt_ref, gate_ref, o_ref, m_scr, acc_scr, s_scr, mc_scr):
    n_qb = m_scr.shape[0]
    slab_w = q_ref.shape[3]
    tq = q_ref.shape[0] * slab_w // n_qb
    n_chunks = vt_ref.shape[0]
    tk = vt_ref.shape[3]
    group = A_HEADS // A_KV

    def q_block(b, h):
        lo = b * tq
        return q_ref[lo // slab_w, h][:, lo % slab_w:lo % slab_w + tq]

    qts = [[jnp.concatenate([q_block(b, g * group + i) for i in range(group)], axis=1)
            for g in range(A_KV)] for b in range(n_qb)]
    m_scr[...] = jnp.full(m_scr.shape, NEG, F32)
    acc_scr[...] = jnp.zeros(acc_scr.shape, F32)

    parts = 2
    tp = tk // parts
    pieces = [(p, g) for p in range(parts) for g in range(A_KV)]

    def scores_piece(b, c, slot, p, g):
        start = pl.multiple_of(c * tk + p * tp, tp)
        st = jnp.dot(k_ref[pl.ds(start, tp), :], qts[b][g], preferred_element_type=F32)
        s_scr[slot, g, p * tp:(p + 1) * tp] = st
        mx = jnp.max(st, axis=0, keepdims=True)
        mc_scr[slot, g] = mx if p == 0 else jnp.maximum(mc_scr[slot, g], mx)

    def accumulate_piece(b, c, slot, p, g):
        m_old = m_scr[b, g]
        m_new = jnp.maximum(m_old, mc_scr[slot, g])
        pt = jnp.exp2(s_scr[slot, g, p * tp:(p + 1) * tp] - m_new).astype(BF16)
        pv = jnp.dot(vt_ref[c, g][:, p * tp:(p + 1) * tp], pt, preferred_element_type=F32)
        if p == 0:
            acc_scr[b, g] = jnp.exp2(m_old - m_new) * acc_scr[b, g] + pv
        else:
            acc_scr[b, g] = acc_scr[b, g] + pv
        if p == parts - 1:
            m_scr[b, g] = m_new

    def scores_and_accumulate(score_of, accumulate_of):
        for p, g in pieces:
            if score_of is not None:
                scores_piece(*score_of, p, g)
            if accumulate_of is not None:
                accumulate_piece(*accumulate_of, p, g)

    def finalize(b):
        rows = slice(b * tq, (b + 1) * tq)
        ot = [acc_scr[b, g, :HEAD_DIM] * (1.0 / acc_scr[b, g, HEAD_DIM:HEAD_DIM + 1])
              for g in range(A_KV)]
        for j in range(A_HEADS // 2):
            g, i0 = (2 * j) // group, (2 * j) % group
            tile_t = jnp.concatenate([ot[g][:, i0 * tq:(i0 + 1) * tq],
                                      ot[g][:, (i0 + 1) * tq:(i0 + 2) * tq]], axis=0)
            gate = gate_ref[rows, j * LANES:(j + 1) * LANES].astype(F32)
            o_ref[rows, j * LANES:(j + 1) * LANES] = (tile_t.T * gate).astype(BF16)

    scores_and_accumulate((0, 0, 0), None)
    for b in range(n_qb):
        def body(c2, carry, b=b):
            c = 2 * c2
            scores_and_accumulate((b, c + 1, 1), (b, c, 0))
            scores_and_accumulate((b, c + 2, 0), (b, c + 1, 1))
            return carry

        lax.fori_loop(0, n_chunks // 2 - 1, body, 0)
        scores_and_accumulate((b, n_chunks - 1, 1), (b, n_chunks - 2, 0))
        following = (b + 1, 0, 0) if b + 1 < n_qb else None
        scores_and_accumulate(following, (b, n_chunks - 1, 1))
        finalize(b)


def _global_attn(qat, ka, vat, gate, tq, n_qb):
    b, s, _ = ka.shape
    _, n_kv, vrows, tk = vat.shape
    tm = qat.shape[-1]
    n_chunks = s // tk
    step_q = n_qb * tq
    assert n_chunks % 2 == 0 and (tm % step_q == 0 or step_q % tm == 0)
    rows = (A_HEADS // A_KV) * tq
    vat = vat.reshape(b, n_chunks, n_kv, vrows, tk)
    qat = qat.reshape(b, s // tm, A_HEADS, LANES, tm)
    if step_q >= tm:
        q_spec = pl.BlockSpec((None, step_q // tm, A_HEADS, LANES, tm), lambda bi, i: (bi, i, 0, 0, 0))
    else:
        per_slab = tm // step_q
        q_spec = pl.BlockSpec((None, 1, A_HEADS, LANES, step_q),
                              lambda bi, i: (bi, i // per_slab, 0, 0, i % per_slab))
    return pl.pallas_call(
        _global_attn_kernel,
        out_shape=jax.ShapeDtypeStruct((b, s, A_HEADS * HEAD_DIM), BF16),
        grid=(b, s // step_q),
        in_specs=[q_spec,
                  pl.BlockSpec((None, s, LANES), lambda bi, i: (bi, 0, 0)),
                  pl.BlockSpec((None, n_chunks, n_kv, vrows, tk), lambda bi, i: (bi, 0, 0, 0, 0)),
                  pl.BlockSpec((None, step_q, A_HEADS * HEAD_DIM), lambda bi, i: (bi, i, 0))],
        out_specs=pl.BlockSpec((None, step_q, A_HEADS * HEAD_DIM), lambda bi, i: (bi, i, 0)),
        scratch_shapes=[pltpu.VMEM((n_qb, A_KV, 1, rows), F32),
                        pltpu.VMEM((n_qb, A_KV, vrows, rows), F32),
                        pltpu.VMEM((2, A_KV, tk, rows), F32), pltpu.VMEM((2, A_KV, 1, rows), F32)],
        compiler_params=_params("arbitrary", "arbitrary"),
        name="global_attn",
    )(qat, ka, vat, gate)


def _window_attn_kernel(q_ref, k_ref, vt_ref, bias_ref, sink_ref, gate_ref, o_ref, *, blk):
    blocks_per_step = q_ref.shape[2] // blk
    nb = vt_ref.shape[0]
    step = pl.program_id(1)
    group = B_HEADS // B_KV

    def neighbours(t):
        i = step * blocks_per_step + t
        return jnp.maximum(i - 1, 0), i, jnp.minimum(i + 1, nb - 1)

    def scores(t, g):
        i = step * blocks_per_step + t
        qt = jnp.concatenate([q_ref[g * group + j][:, t * blk:(t + 1) * blk]
                              for j in range(group)], axis=1)
        k = jnp.concatenate([k_ref[pl.ds(pl.multiple_of(n * blk, blk), blk), :]
                             for n in neighbours(t)], axis=0)
        st = jnp.dot(k, qt, preferred_element_type=F32) + bias_ref[g]
        if t == 0:
            st = jnp.concatenate([st[:blk] + jnp.where(i == 0, NEG, 0.0), st[blk:]], axis=0)
        if t == blocks_per_step - 1:
            st = jnp.concatenate([st[:2 * blk], st[2 * blk:] + jnp.where(i == nb - 1, NEG, 0.0)], axis=0)
        m = jnp.maximum(jnp.max(st, axis=0, keepdims=True), sink_ref[g])
        return st, m

    def probs(st, m):
        return jnp.exp2(st - m).astype(BF16), m

    def output(t, g, pt, m):
        vt = jnp.concatenate([vt_ref[n, g] for n in neighbours(t)], axis=1)
        ot = jnp.dot(vt, pt, preferred_element_type=F32)
        denom = ot[HEAD_DIM:HEAD_DIM + 1] + jnp.exp2(sink_ref[g] - m)
        return ot[:HEAD_DIM] * (1.0 / denom)

    def store(t, ot):
        rows = slice(t * blk, (t + 1) * blk)
        for j in range(B_HEADS // 2):
            g, i0 = (2 * j) // group, (2 * j) % group
            tile_t = jnp.concatenate([ot[g][:, i0 * blk:(i0 + 1) * blk],
                                      ot[g][:, (i0 + 1) * blk:(i0 + 2) * blk]], axis=0)
            gate = gate_ref[rows, j * LANES:(j + 1) * LANES].astype(F32)
            o_ref[rows, j * LANES:(j + 1) * LANES] = (tile_t.T * gate).astype(BF16)

    items = [(t, g) for t in range(blocks_per_step) for g in range(B_KV)]
    sm, pb, outs = {}, {}, {}
    for n in range(len(items) + 2):
        if n < len(items):
            sm[n] = scores(*items[n])
        if 1 <= n <= len(items):
            pb[n - 1] = probs(*sm.pop(n - 1))
        if n >= 2:
            t, g = items[n - 2]
            outs[g] = output(t, g, *pb.pop(n - 2))
            if g == B_KV - 1:
                store(t, outs)


def _window_attn(qbt, kb, vbt, bias_t, sink_cols, gate, blk, blocks_per_step):
    b, s, _ = kb.shape
    nb = s // blk
    rows = (B_HEADS // B_KV) * blk
    tq = blk * blocks_per_step
    tm = qbt.shape[-1]
    assert tm % tq == 0
    per_slab = tm // tq
    vrows = vbt.shape[2]
    vbt = vbt.reshape(b, nb, B_KV, vrows, blk)
    qbt = qbt.reshape(b, s // tm, B_HEADS, LANES, tm)
    return pl.pallas_call(
        functools.partial(_window_attn_kernel, blk=blk),
        out_shape=jax.ShapeDtypeStruct((b, s, B_HEADS * HEAD_DIM), BF16),
        grid=(b, s // tq),
        in_specs=[pl.BlockSpec((None, None, B_HEADS, LANES, tq),
                               lambda bi, i: (bi, i // per_slab, 0, 0, i % per_slab)),
                  pl.BlockSpec((None, s, LANES), lambda bi, i: (bi, 0, 0)),
                  pl.BlockSpec((None, nb, B_KV, vrows, blk), lambda bi, i: (bi, 0, 0, 0, 0)),
                  pl.BlockSpec((B_KV, 3 * blk, rows), lambda bi, i: (0, 0, 0)),
                  pl.BlockSpec((B_KV, 1, rows), lambda bi, i: (0, 0, 0)),
                  pl.BlockSpec((None, tq, B_HEADS * HEAD_DIM), lambda bi, i: (bi, i, 0))],
        out_specs=pl.BlockSpec((None, tq, B_HEADS * HEAD_DIM), lambda bi, i: (bi, i, 0)),
        compiler_params=_params("arbitrary", "arbitrary"),
        name="window_attn",
    )(qbt, kb, vbt, bias_t, sink_cols, gate)


def _mem_attn_items(q_ref, kv_ref, gate_ref, o_ref, between=(), sub=256):
    between = list(between)
    mlen = kv_ref.shape[0]
    ones = jnp.ones((mlen, M_HEAD_DIM), BF16)

    def scores(t, h):
        lo, hi = h * M_HEAD_DIM, (h + 1) * M_HEAD_DIM
        s = lax.dot_general(q_ref[t * sub:(t + 1) * sub, lo:hi], kv_ref[:, lo:hi], _NT,
                            preferred_element_type=F32)
        return s, jnp.max(s, axis=-1, keepdims=True)

    def probs(s, m):
        return jnp.exp2(s - m).astype(BF16)

    def output(t, h, p):
        lo, hi = h * M_HEAD_DIM, (h + 1) * M_HEAD_DIM
        v = jnp.concatenate([kv_ref[:, MEM_W + lo:MEM_W + hi], ones], axis=1)
        o = jnp.dot(p, v, preferred_element_type=F32)
        o = o[:, :M_HEAD_DIM] * (1.0 / o[:, M_HEAD_DIM:])
        rows = slice(t * sub, (t + 1) * sub)
        o_ref[rows, lo:hi] = (o * gate_ref[rows, lo:hi].astype(F32)).astype(BF16)

    items = [(t, h) for t in range(q_ref.shape[0] // sub) for h in range(M_HEADS)]
    sm, pb = {}, {}
    for n in range(len(items) + 2):
        if n < len(items):
            sm[n] = scores(*items[n])
        if 1 <= n <= len(items):
            pb[n - 1] = probs(*sm.pop(n - 1))
        if n >= 2:
            output(*items[n - 2], pb.pop(n - 2))
        if between:
            between.pop(0)()
    for run in between:
        run()


def _nbr_attn_kernel(q_ref, k_ref, v_ref, bias_ref, gate_ref, o_ref, *, grid_rows, rows_per_step):
    rb = pl.program_id(2)
    nkeys = NA_ROWS * GRID_W
    n_pairs = q_ref.shape[1] // LANES
    ones = jnp.ones((nkeys, LANES), BF16)
    starts, deltas = [], []
    for t in range(rows_per_step):
        r = rb * rows_per_step + t
        rs = jnp.clip(r - NA_ROWS // 2, 0, grid_rows - NA_ROWS)
        starts.append(pl.multiple_of(rs * GRID_W, GRID_W))
        deltas.append(r - rs)
    items = [(t, j) for t in range(rows_per_step) for j in range(n_pairs)]

    def scores(t, j):
        lo, hi = j * LANES, (j + 1) * LANES
        qt = q_ref[t * GRID_W:(t + 1) * GRID_W, lo:hi]
        lane = _lane_iota(qt.shape)
        zero = jnp.zeros_like(qt)
        q = jnp.concatenate([jnp.where(lane < HEAD_DIM, qt, zero),
                             jnp.where(lane >= HEAD_DIM, qt, zero)], axis=0)
        k = k_ref[pl.ds(starts[t], nkeys), lo:hi]
        s = lax.dot_general(q, k, _NT, preferred_element_type=F32) + bias_ref[deltas[t], j]
        return s, jnp.max(s, axis=-1, keepdims=True)

    def probs(s, m):
        return jnp.exp2(s - m).astype(BF16)

    def output(t, j, p):
        lo, hi = j * LANES, (j + 1) * LANES
        v = jnp.concatenate([v_ref[pl.ds(starts[t], nkeys), lo:hi], ones], axis=1)
        o = jnp.dot(p, v, preferred_element_type=F32)
        o = o[:, :LANES] * (1.0 / o[:, LANES:])
        lane_o = _lane_iota((GRID_W, LANES))
        tile = jnp.where(lane_o < HEAD_DIM, o[:GRID_W, :], o[GRID_W:, :])
        rows = slice(t * GRID_W, (t + 1) * GRID_W)
        o_ref[rows, lo:hi] = (tile * gate_ref[rows, lo:hi].astype(F32)).astype(BF16)

    sm = {}
    pb = {}
    for n in range(len(items) + 2):
        if n < len(items):
            sm[n] = scores(*items[n])
        if 1 <= n <= len(items):
            pb[n - 1] = probs(*sm.pop(n - 1))
        if n >= 2:
            output(*items[n - 2], pb.pop(n - 2))


def _nbr_attn(qc, kc, vc, bias, gate, rows_per_step, head_splits):
    b, s, w = qc.shape
    grid_rows = s // GRID_W
    nkeys = NA_ROWS * GRID_W
    wh = w // head_splits
    tq = rows_per_step * GRID_W
    blk = lambda hh, bi, rb: (bi, rb, hh)
    return pl.pallas_call(
        functools.partial(_nbr_attn_kernel, grid_rows=grid_rows, rows_per_step=rows_per_step),
        out_shape=jax.ShapeDtypeStruct((b, s, w), BF16),
        grid=(head_splits, b, grid_rows // rows_per_step),
        in_specs=[pl.BlockSpec((None, tq, wh), blk),
                  pl.BlockSpec((None, s, wh), lambda hh, bi, rb: (bi, 0, hh)),
                  pl.BlockSpec((None, s, wh), lambda hh, bi, rb: (bi, 0, hh)),
                  pl.BlockSpec((NA_ROWS, wh // LANES, 2 * GRID_W, nkeys), lambda hh, bi, rb: (0, hh, 0, 0)),
                  pl.BlockSpec((None, tq, wh), blk)],
        out_specs=pl.BlockSpec((None, tq, wh), blk),
        compiler_params=_params("arbitrary", "arbitrary", "arbitrary"),
        name="nbr_attn",
    )(qc, kc, vc, bias, gate)


def _out_final_kernel(yc_ref, ym_ref, x_ref, w_ref, g_ref, o_ref):
    acc = x_ref[...]
    off = 0
    for y_ref in (yc_ref, ym_ref):
        width = y_ref.shape[1]
        acc = acc + jnp.dot(y_ref[...], w_ref[off:off + width, :], preferred_element_type=F32)
        off += width
    o_ref[...] = _rms_rows(acc, g_ref[...])


def _out_final(yc, ym, x2d, w_bf16, final_gain, tm):
    n_tok, d = x2d.shape
    row = lambda i: (i, 0)
    const = lambda i: (0, 0)
    return pl.pallas_call(
        _out_final_kernel,
        out_shape=jax.ShapeDtypeStruct((n_tok, d), F32),
        grid=(n_tok // tm,),
        in_specs=[pl.BlockSpec((tm, yc.shape[1]), row),
                  pl.BlockSpec((tm, ym.shape[1]), row),
                  pl.BlockSpec((tm, d), row),
                  pl.BlockSpec(w_bf16.shape, const),
                  pl.BlockSpec((1, d), const)],
        out_specs=pl.BlockSpec((tm, d), row),
        compiler_params=_params("arbitrary"),
        name="out_proj_final",
    )(yc, ym, x2d, w_bf16, final_gain)


def _rope_tables(seq):
    quarter = HEAD_DIM // 4
    freqs = jnp.power(ROPE_THETA, -jnp.arange(quarter, dtype=F32) / quarter)
    t = jnp.arange(seq)
    ang_r = (t // GRID_W).astype(F32)[:, None] * freqs
    ang_c = (t % GRID_W).astype(F32)[:, None] * freqs
    cos_h = jnp.concatenate([jnp.cos(ang_r), jnp.cos(ang_r), jnp.cos(ang_c), jnp.cos(ang_c)], axis=-1)
    sin_h = jnp.concatenate([-jnp.sin(ang_r), jnp.sin(ang_r), -jnp.sin(ang_c), jnp.sin(ang_c)], axis=-1)
    return jnp.tile(cos_h, (1, 2)), jnp.tile(sin_h, (1, 2))


def _t5_bucket(rel):
    nb = REL_BUCKETS // 2
    max_exact = nb // 2
    ret = jnp.where(rel > 0, nb, 0)
    n = jnp.abs(rel)
    nf = jnp.maximum(n, 1).astype(F32)
    large = max_exact + (jnp.log(nf / max_exact) / math.log(REL_MAX_DIST / max_exact)
                         * (nb - max_exact)).astype(jnp.int32)
    large = jnp.minimum(large, nb - 1)
    return ret + jnp.where(n < max_exact, n, large)


def _window_bias(rel_bias, blk):
    span = blk + 2 * WINDOW
    reach = span - WINDOW - 1
    rel = jnp.arange(-reach, reach + 1)
    per_rel = jnp.where((jnp.abs(rel) <= WINDOW)[:, None],
                        rel_bias.astype(F32)[_t5_bucket(rel)] * LOG2E, NEG).T
    bias = jnp.stack([per_rel[:, reach - WINDOW - q:reach - WINDOW - q + span] for q in range(blk)],
                     axis=-1)
    group = B_HEADS // B_KV
    bias = bias.reshape(B_KV, group, span, blk).transpose(0, 2, 1, 3)
    return bias.reshape(B_KV, span, group * blk)


def _nbr_bias(rpb):
    col = np.arange(GRID_W)
    cs = np.clip(col - NA_COLS // 2, 0, GRID_W - NA_COLS)
    colmask = (col[None, :] >= cs[:, None]) & (col[None, :] < cs[:, None] + NA_COLS)
    dc = np.clip(col[None, :] - col[:, None] + NA_COLS - 1, 0, 2 * NA_COLS - 2)
    onehot = (dc[None] == np.arange(2 * NA_COLS - 1)[:, None, None]) & colmask[None]
    t = jnp.einsum('hrc,cqk->hqrk', rpb.astype(F32) * LOG2E, jnp.asarray(onehot, F32),
                   precision=lax.Precision.HIGHEST)
    t = t + jnp.asarray(np.where(colmask, 0.0, NEG), F32)[None, :, None, :]
    delta, i = np.meshgrid(np.arange(NA_ROWS), np.arange(NA_ROWS), indexing="ij")
    pick = (i - delta + NA_ROWS - 1)[:, :, None] == np.arange(2 * NA_ROWS - 1)
    out = jnp.einsum('dir,hqrk->dhqik', jnp.asarray(pick, F32), t, precision=lax.Precision.HIGHEST)
    return out.reshape(NA_ROWS, C_HEADS // 2, 2 * GRID_W, NA_ROWS * GRID_W)


def kernel(x, mem, norm_gain, mem_norm_gain, w_in_even, w_out_even, q_norm_a, k_norm_a, sink_b,
           rel_bias, w_in_odd, w_out_odd, rpb_c, w_mem_kv, final_norm_gain):
    b, s, d = x.shape
    mlen = mem.shape[1]
    assert s % GRID_W == 0 and s // GRID_W >= NA_ROWS and s % 512 == 0
    tm = 512
    blk = 128

    x2d = x.reshape(b * s, d)
    mem_tm = math.gcd(b * mlen, 512)
    mkv = _mem_kv(mem.reshape(b * mlen, d), mem_norm_gain.reshape(1, d), w_mem_kv.astype(BF16), mem_tm)
    mkv = mkv.reshape(w_mem_kv.shape[0], b, mlen, 2 * MEM_W)

    cos, sin = _rope_tables(s)
    gq = jnp.tile(q_norm_a[0].astype(F32), 2).reshape(1, LANES)
    gk = jnp.tile(k_norm_a[0].astype(F32), 2).reshape(1, LANES)
    qa, ka, va, qb, kb, vb, ga, gb, ym = _in_even(
        x2d, norm_gain[0].reshape(1, d), w_in_even[0].astype(BF16), gq, gk, cos, sin, mkv, s, tm)
    r3 = lambda a: a.reshape(b, s, a.shape[-1])
    ya = _global_attn(qa, r3(ka), va, r3(ga), tq=256, n_qb=4)
    group_b = B_HEADS // B_KV
    sink_cols = jnp.repeat(sink_b[0].astype(F32).reshape(B_KV, group_b) * LOG2E, blk,
                           axis=1).reshape(B_KV, 1, group_b * blk)
    yb = _window_attn(qb, r3(kb), vb, _window_bias(rel_bias, blk), sink_cols, r3(gb), blk,
                      blocks_per_step=4)

    x1, qc, kc, vc, gc, ym1 = _mid(ya.reshape(b * s, -1), yb.reshape(b * s, -1), ym, x2d,
                                   w_out_even[0].astype(BF16), norm_gain[1].reshape(1, d),
                                   w_in_odd[0].astype(BF16), mkv, s, tm)
    yc = _nbr_attn(r3(qc), r3(kc), r3(vc), _nbr_bias(rpb_c[0]), r3(gc), rows_per_step=8, head_splits=2)
    out = _out_final(yc.reshape(b * s, -1), ym1, x1, w_out_odd[0].astype(BF16),
                     final_norm_gain.reshape(1, d), tm)
    return out.reshape(b, s, d)
```

```python
import functools
import math

import jax
import jax.numpy as jnp
import numpy as np
from jax import lax
from jax.experimental import pallas as pl
from jax.experimental.pallas import tpu as pltpu

GRID_W = 64
HEAD_DIM = 64
A_HEADS = 8
A_KV = 2
B_HEADS = 8
B_KV = 2
WINDOW = 128
C_HEADS = 16
NA_ROWS = 8
NA_COLS = 16
M_HEADS = 4
M_HEAD_DIM = 128
MEM_W = M_HEADS * M_HEAD_DIM
REL_BUCKETS = 32
REL_MAX_DIST = 128
ROPE_THETA = 10000.0
EPS = 1e-6

LANES = 128
NEG = -1e30
LOG2E = math.log2(math.e)
V_ONES_ROWS = 16
A_CHUNK = 512
VMEM_LIMIT_BYTES = 56 * 1024 * 1024

F32 = jnp.float32
BF16 = jnp.bfloat16

_NT = (((1,), (1,)), ((), ()))


def _params(*sem):
    return pltpu.CompilerParams(dimension_semantics=sem, vmem_limit_bytes=VMEM_LIMIT_BYTES)


def _rms_rows(x, gain):
    ms = jnp.mean(x * x, axis=-1, keepdims=True)
    return x * lax.rsqrt(ms + EPS) * gain


def _lane_iota(shape):
    return lax.broadcasted_iota(jnp.int32, shape, len(shape) - 1)


def _silu(x):
    return x * (1.0 / (1.0 + jnp.exp(-x)))


def _half_ones():
    r = lax.broadcasted_iota(jnp.int32, (LANES, LANES), 0) // HEAD_DIM
    c = lax.broadcasted_iota(jnp.int32, (LANES, LANES), 1) // HEAD_DIM
    return jnp.where(r == c, 1.0, 0.0).astype(BF16)


def _head_norm_rope(t, gain, cos, sin_signed, ones_bd):
    ss = jnp.dot((t * t).astype(BF16), ones_bd, preferred_element_type=F32)
    tn = t * lax.rsqrt(ss * (1.0 / HEAD_DIM) + EPS) * gain
    lane = _lane_iota(tn.shape)
    quarter = HEAD_DIM // 4
    partner = jnp.where((lane % (2 * quarter)) < quarter,
                        pltpu.roll(tn, LANES - quarter, 1), pltpu.roll(tn, quarter, 1))
    return tn * cos + partner * sin_signed


def _store_padded_heads(q_tiles, out_ref, n_heads, n_kv, transposed=False):
    group = n_heads // n_kv
    for h in range(n_heads):
        t = q_tiles[h // 2]
        src_half = h % 2
        dst_half = (h // group) % 2
        if src_half != dst_half:
            t = pltpu.roll(t, HEAD_DIM, 1)
        lane = _lane_iota(t.shape)
        keep = (lane >= HEAD_DIM) if dst_half == 1 else (lane < HEAD_DIM)
        padded = jnp.where(keep, t, 0.0)
        if transposed:
            out_ref[h] = padded.T.astype(BF16)
        else:
            out_ref[:, h * LANES:(h + 1) * LANES] = padded.astype(BF16)


def _mem_kv_kernel(mem_ref, g_ref, w_ref, o_ref):
    h = _rms_rows(mem_ref[...], g_ref[...]).astype(BF16)
    o_ref[...] = jnp.dot(h, w_ref[...], preferred_element_type=F32).astype(BF16)


def _mem_kv(mem2d, gain, w_bf16, tm):
    depth, d, n = w_bf16.shape
    rows = mem2d.shape[0]
    return pl.pallas_call(
        _mem_kv_kernel,
        out_shape=jax.ShapeDtypeStruct((depth, rows, n), BF16),
        grid=(depth, rows // tm),
        in_specs=[pl.BlockSpec((tm, d), lambda l, i: (i, 0)),
                  pl.BlockSpec((1, d), lambda l, i: (0, 0)),
                  pl.BlockSpec((None, d, n), lambda l, i: (l, 0, 0))],
        out_specs=pl.BlockSpec((None, tm, n), lambda l, i: (l, i, 0)),
        compiler_params=_params("arbitrary", "arbitrary"),
        name="mem_kv_proj",
    )(mem2d, gain, w_bf16)


def _in_even_kernel(x_ref, g_ref, w_ref, gq_ref, gk_ref, cos_ref, sin_ref, mkv_ref,
                    qa_ref, ka_ref, va_ref, qb_ref, kb_ref, vb_ref, ga_ref, gb_ref, ym_ref,
                    qm_scr, gm_scr):
    h = _rms_rows(x_ref[...], g_ref[...]).astype(BF16)

    def seg(lo, hi):
        return jnp.dot(h, w_ref[:, lo:hi], preferred_element_type=F32)

    ones_bd = _half_ones()
    scale = HEAD_DIM ** -0.5
    qa_w = A_HEADS * HEAD_DIM
    kva_w = A_KV * HEAD_DIM
    qb_w = B_HEADS * HEAD_DIM
    kvb_w = B_KV * HEAD_DIM
    offs = np.cumsum([0, qa_w, kva_w, kva_w, qb_w, kvb_w, kvb_w, MEM_W, qa_w, qb_w, MEM_W])
    o_qa, o_ka, o_va, o_qb, o_kb, o_vb, o_qm, o_ga, o_gb, o_gm = (int(v) for v in offs[:-1])

    def do_qa():
        zq = seg(o_qa, o_qa + qa_w)
        tiles = [_head_norm_rope(zq[:, j * LANES:(j + 1) * LANES], gq_ref[...], cos_ref[...],
                                 sin_ref[...], ones_bd) * (scale * LOG2E) for j in range(qa_w // LANES)]
        _store_padded_heads(tiles, qa_ref, A_HEADS, A_KV, transposed=True)

    def do_ka():
        ka_ref[...] = _head_norm_rope(seg(o_ka, o_ka + kva_w), gk_ref[...], cos_ref[...],
                                      sin_ref[...], ones_bd).astype(BF16)

    def do_va():
        vt = seg(o_va, o_va + kva_w).T.astype(BF16)
        for j in range(va_ref.shape[0]):
            for g in range(A_KV):
                va_ref[j, g, :HEAD_DIM] = vt[g * HEAD_DIM:(g + 1) * HEAD_DIM,
                                             j * A_CHUNK:(j + 1) * A_CHUNK]
                va_ref[j, g, HEAD_DIM:] = jnp.ones((V_ONES_ROWS, A_CHUNK), BF16)

    def do_qb():
        zq = seg(o_qb, o_qb + qb_w) * (scale * LOG2E)
        _store_padded_heads([zq[:, j * LANES:(j + 1) * LANES] for j in range(qb_w // LANES)],
                            qb_ref, B_HEADS, B_KV, transposed=True)

    def do_kb():
        kb_ref[...] = seg(o_kb, o_kb + kvb_w).astype(BF16)

    def do_vb():
        vt = seg(o_vb, o_vb + kvb_w).T.astype(BF16)
        for j in range(vb_ref.shape[0]):
            for g in range(B_KV):
                vb_ref[j, g, :HEAD_DIM] = vt[g * HEAD_DIM:(g + 1) * HEAD_DIM, j * WINDOW:(j + 1) * WINDOW]
                vb_ref[j, g, HEAD_DIM:] = jnp.ones((V_ONES_ROWS, WINDOW), BF16)

    def do_ga():
        ga_ref[...] = _silu(seg(o_ga, o_ga + qa_w)).astype(BF16)

    def do_gb():
        gb_ref[...] = _silu(seg(o_gb, o_gb + qb_w)).astype(BF16)

    qm_scr[...] = (seg(o_qm, o_qm + MEM_W) * (M_HEAD_DIM ** -0.5 * LOG2E)).astype(BF16)
    gm_scr[...] = _silu(seg(o_gm, o_gm + MEM_W)).astype(BF16)
    _mem_attn_items(qm_scr, mkv_ref, gm_scr, ym_ref,
                    between=[do_qa, do_ga, do_qb, do_gb, do_va, do_ka, do_vb, do_kb])


def _in_even(x2d, gain, w_bf16, gq, gk, cos, sin, mkv, seq, tm):
    n_tok, d = x2d.shape
    n_in = w_bf16.shape[1]
    per_seq = seq // tm
    mlen = mkv.shape[2]
    widths = [A_HEADS * LANES, A_KV * HEAD_DIM, A_KV * HEAD_DIM,
              B_HEADS * LANES, B_KV * HEAD_DIM, B_KV * HEAD_DIM,
              A_HEADS * HEAD_DIM, B_HEADS * HEAD_DIM, MEM_W]
    row = lambda i: (i, 0)
    const = lambda i: (0, 0)
    out_shape = [jax.ShapeDtypeStruct((n_tok, w), BF16) for w in widths]
    out_specs = [pl.BlockSpec((tm, w), row) for w in widths]
    vrows = HEAD_DIM + V_ONES_ROWS
    out_shape[0] = jax.ShapeDtypeStruct((n_tok // tm, A_HEADS, LANES, tm), BF16)
    out_specs[0] = pl.BlockSpec((None, A_HEADS, LANES, tm), lambda i: (i, 0, 0, 0))
    out_shape[3] = jax.ShapeDtypeStruct((n_tok // tm, B_HEADS, LANES, tm), BF16)
    out_specs[3] = pl.BlockSpec((None, B_HEADS, LANES, tm), lambda i: (i, 0, 0, 0))
    out_shape[2] = jax.ShapeDtypeStruct((n_tok // A_CHUNK, A_KV, vrows, A_CHUNK), BF16)
    out_specs[2] = pl.BlockSpec((tm // A_CHUNK, A_KV, vrows, A_CHUNK), lambda i: (i, 0, 0, 0))
    out_shape[5] = jax.ShapeDtypeStruct((n_tok // WINDOW, B_KV, vrows, WINDOW), BF16)
    out_specs[5] = pl.BlockSpec((tm // WINDOW, B_KV, vrows, WINDOW), lambda i: (i, 0, 0, 0))
    return pl.pallas_call(
        _in_even_kernel,
        out_shape=out_shape,
        grid=(n_tok // tm,),
        in_specs=[pl.BlockSpec((tm, d), row),
                  pl.BlockSpec((1, d), const),
                  pl.BlockSpec((d, n_in), const, pipeline_mode=pl.Buffered(1)),
                  pl.BlockSpec((1, LANES), const),
                  pl.BlockSpec((1, LANES), const),
                  pl.BlockSpec((tm, LANES), lambda i: (i % per_seq, 0)),
                  pl.BlockSpec((tm, LANES), lambda i: (i % per_seq, 0)),
                  pl.BlockSpec((None, None, mlen, 2 * MEM_W), lambda i: (0, i // per_seq, 0, 0))],
        out_specs=out_specs,
        scratch_shapes=[pltpu.VMEM((tm, MEM_W), BF16), pltpu.VMEM((tm, MEM_W), BF16)],
        compiler_params=_params("arbitrary"),
        name="in_proj_even",
    )(x2d, gain, w_bf16, gq, gk, cos, sin, mkv)


def _mid_kernel(ya_ref, yb_ref, ym_ref, x_ref, wo_ref, g_ref, wi_ref, mkv_ref,
                x1_ref, qc_ref, kc_ref, vc_ref, gc_ref, ym1_ref, qm_scr, gm_scr):
    acc = x_ref[...]
    off = 0
    for y_ref in (ya_ref, yb_ref, ym_ref):
        width = y_ref.shape[1]
        acc = acc + jnp.dot(y_ref[...], wo_ref[off:off + width, :], preferred_element_type=F32)
        off += width
    x1_ref[...] = acc
    h = _rms_rows(acc, g_ref[...]).astype(BF16)

    def seg(lo, hi):
        return jnp.dot(h, wi_ref[:, lo:hi], preferred_element_type=F32)

    cw = C_HEADS * HEAD_DIM
    half = cw // 2
    o_qc, o_kc, o_vc, o_qm, o_gc, o_gm = 0, cw, 2 * cw, 3 * cw, 3 * cw + MEM_W, 4 * cw + MEM_W

    def thunk(ref, base, part, fn):
        def run():
            lo = part * half
            ref[:, lo:lo + half] = fn(seg(base + lo, base + lo + half)).astype(BF16)
        return run

    plain = lambda z: z
    to_q = lambda z: z * (HEAD_DIM ** -0.5 * LOG2E)
    segments = [thunk(ref, base, part, fn)
                for ref, base, fn in ((gc_ref, o_gc, _silu), (qc_ref, o_qc, to_q),
                                      (kc_ref, o_kc, plain), (vc_ref, o_vc, plain))
                for part in range(2)]
    qm_scr[...] = (seg(o_qm, o_qm + MEM_W) * (M_HEAD_DIM ** -0.5 * LOG2E)).astype(BF16)
    gm_scr[...] = _silu(seg(o_gm, o_gm + MEM_W)).astype(BF16)
    _mem_attn_items(qm_scr, mkv_ref, gm_scr, ym1_ref, between=segments)


def _mid(ya, yb, ym, x2d, wo_bf16, gain, wi_bf16, mkv, seq, tm):
    n_tok, d = x2d.shape
    per_seq = seq // tm
    mlen = mkv.shape[2]
    cw = C_HEADS * HEAD_DIM
    row = lambda i: (i, 0)
    const = lambda i: (0, 0)
    once = pl.Buffered(1)
    out_widths = [cw, cw, cw, cw, MEM_W]
    return pl.pallas_call(
        _mid_kernel,
        out_shape=[jax.ShapeDtypeStruct((n_tok, d), F32)]
                  + [jax.ShapeDtypeStruct((n_tok, w), BF16) for w in out_widths],
        grid=(n_tok // tm,),
        in_specs=[pl.BlockSpec((tm, ya.shape[1]), row),
                  pl.BlockSpec((tm, yb.shape[1]), row),
                  pl.BlockSpec((tm, ym.shape[1]), row),
                  pl.BlockSpec((tm, d), row),
                  pl.BlockSpec(wo_bf16.shape, const, pipeline_mode=once),
                  pl.BlockSpec((1, d), const),
                  pl.BlockSpec(wi_bf16.shape, const, pipeline_mode=once),
                  pl.BlockSpec((None, None, mlen, 2 * MEM_W), lambda i: (1, i // per_seq, 0, 0))],
        out_specs=[pl.BlockSpec((tm, d), row)] + [pl.BlockSpec((tm, w), row) for w in out_widths],
        scratch_shapes=[pltpu.VMEM((tm, MEM_W), BF16), pltpu.VMEM((tm, MEM_W), BF16)],
        compiler_params=_params("arbitrary"),
        name="out_proj_in_proj_odd",
    )(ya, yb, ym, x2d, wo_bf16, gain, wi_bf16, mkv)


def _global_attn_kernel(q_ref, k_ref, vt_ref, gate_ref, o_ref, m_scr, acc_scr, s_scr, mc_scr):
    n_qb = m_scr.shape[0]
    slab_w = q_ref.shape[3]
    tq = q_ref.shape[0] * slab_w // n_qb
    n_chunks = vt_ref.shape[0]
    tk = vt_ref.shape[3]
    group = A_HEADS // A_KV

    def q_block(b, h):
        lo = b * tq
        return q_ref[lo // slab_w, h][:, lo % slab_w:lo % slab_w + tq]

    qts = [[jnp.concatenate([q_block(b, g * group + i) for i in range(group)], axis=1)
            for g in range(A_KV)] for b in range(n_qb)]
    m_scr[...] = jnp.full(m_scr.shape, NEG, F32)
    acc_scr[...] = jnp.zeros(acc_scr.shape, F32)

    parts = 2
    tp = tk // parts
    pieces = [(p, g) for p in range(parts) for g in range(A_KV)]

    def scores_piece(b, c, slot, p, g):
        start = pl.multiple_of(c * tk + p * tp, tp)
        st = jnp.dot(k_ref[pl.ds(start, tp), :], qts[b][g], preferred_element_type=F32)
        s_scr[slot, g, p * tp:(p + 1) * tp] = st
        mx = jnp.max(st, axis=0, keepdims=True)
        mc_scr[slot, g] = mx if p == 0 else jnp.maximum(mc_scr[slot, g], mx)

    def accumulate_piece(b, c, slot, p, g):
        m_old = m_scr[b, g]
        m_new = jnp.maximum(m_old, mc_scr[slot, g])
        pt = jnp.exp2(s_scr[slot, g, p * tp:(p + 1) * tp] - m_new).astype(BF16)
        pv = jnp.dot(vt_ref[c, g][:, p * tp:(p + 1) * tp], pt, preferred_element_type=F32)
        if p == 0:
            acc_scr[b, g] = jnp.exp2(m_old - m_new) * acc_scr[b, g] + pv
        else:
            acc_scr[b, g] = acc_scr[b, g] + pv
        if p == parts - 1:
            m_scr[b, g] = m_new

    def scores_and_accumulate(score_of, accumulate_of):
        for p, g in pieces:
            if score_of is not None:
                scores_piece(*score_of, p, g)
            if accumulate_of is not None:
                accumulate_piece(*accumulate_of, p, g)

    def finalize(b):
        rows = slice(b * tq, (b + 1) * tq)
        ot = [acc_scr[b, g, :HEAD_DIM] * (1.0 / acc_scr[b, g, HEAD_DIM:HEAD_DIM + 1])
              for g in range(A_KV)]
        for j in range(A_HEADS // 2):
            g, i0 = (2 * j) // group, (2 * j) % group
            tile_t = jnp.concatenate([ot[g][:, i0 * tq:(i0 + 1) * tq],
                                      ot[g][:, (i0 + 1) * tq:(i0 + 2) * tq]], axis=0)
            gate = gate_ref[rows, j * LANES:(j + 1) * LANES].astype(F32)
            o_ref[rows, j * LANES:(j + 1) * LANES] = (tile_t.T * gate).astype(BF16)

    scores_and_accumulate((0, 0, 0), None)
    for b in range(n_qb):
        def body(c2, carry, b=b):
            c = 2 * c2
            scores_and_accumulate((b, c + 1, 1), (b, c, 0))
            scores_and_accumulate((b, c + 2, 0), (b, c + 1, 1))
            return carry

        lax.fori_loop(0, n_chunks // 2 - 1, body, 0)
        scores_and_accumulate((b, n_chunks - 1, 1), (b, n_chunks - 2, 0))
        following = (b + 1, 0, 0) if b + 1 < n_qb else None
        scores_and_accumulate(following, (b, n_chunks - 1, 1))
        finalize(b)


def _global_attn(qat, ka, vat, gate, tq, n_qb):
    b, s, _ = ka.shape
    _, n_kv, vrows, tk = vat.shape
    tm = qat.shape[-1]
    n_chunks = s // tk
    step_q = n_qb * tq
    assert n_chunks % 2 == 0 and (tm % step_q == 0 or step_q % tm == 0)
    rows = (A_HEADS // A_KV) * tq
    vat = vat.reshape(b, n_chunks, n_kv, vrows, tk)
    qat = qat.reshape(b, s // tm, A_HEADS, LANES, tm)
    if step_q >= tm:
        q_spec = pl.BlockSpec((None, step_q // tm, A_HEADS, LANES, tm), lambda bi, i: (bi, i, 0, 0, 0))
    else:
        per_slab = tm // step_q
        q_spec = pl.BlockSpec((None, 1, A_HEADS, LANES, step_q),
                              lambda bi, i: (bi, i // per_slab, 0, 0, i % per_slab))
    return pl.pallas_call(
        _global_attn_kernel,
        out_shape=jax.ShapeDtypeStruct((b, s, A_HEADS * HEAD_DIM), BF16),
        grid=(b, s // step_q),
        in_specs=[q_spec,
                  pl.BlockSpec((None, s, LANES), lambda bi, i: (bi, 0, 0)),
                  pl.BlockSpec((None, n_chunks, n_kv, vrows, tk), lambda bi, i: (bi, 0, 0, 0, 0)),
                  pl.BlockSpec((None, step_q, A_HEADS * HEAD_DIM), lambda bi, i: (bi, i, 0))],
        out_specs=pl.BlockSpec((None, step_q, A_HEADS * HEAD_DIM), lambda bi, i: (bi, i, 0)),
        scratch_shapes=[pltpu.VMEM((n_qb, A_KV, 1, rows), F32),
                        pltpu.VMEM((n_qb, A_KV, vrows, rows), F32),
                        pltpu.VMEM((2, A_KV, tk, rows), F32), pltpu.VMEM((2, A_KV, 1, rows), F32)],
        compiler_params=_params("arbitrary", "arbitrary"),
        name="global_attn",
    )(qat, ka, vat, gate)


def _window_attn_kernel(q_ref, k_ref, vt_ref, bias_ref, sink_ref, gate_ref, o_ref, *, blk):
    blocks_per_step = q_ref.shape[2] // blk
    nb = vt_ref.shape[0]
    step = pl.program_id(1)
    group = B_HEADS // B_KV

    def neighbours(t):
        i = step * blocks_per_step + t
        return jnp.maximum(i - 1, 0), i, jnp.minimum(i + 1, nb - 1)

    def scores(t, g):
        i = step * blocks_per_step + t
        qt = jnp.concatenate([q_ref[g * group + j][:, t * blk:(t + 1) * blk]
                              for j in range(group)], axis=1)
        k = jnp.concatenate([k_ref[pl.ds(pl.multiple_of(n * blk, blk), blk), :]
                             for n in neighbours(t)], axis=0)
        st = jnp.dot(k, qt, preferred_element_type=F32) + bias_ref[g]
        if t == 0:
            st = jnp.concatenate([st[:blk] + jnp.where(i == 0, NEG, 0.0), st[blk:]], axis=0)
        if t == blocks_per_step - 1:
            st = jnp.concatenate([st[:2 * blk], st[2 * blk:] + jnp.where(i == nb - 1, NEG, 0.0)], axis=0)
        m = jnp.maximum(jnp.max(st, axis=0, keepdims=True), sink_ref[g])
        return st, m

    def probs(st, m):
        return jnp.exp2(st - m).astype(BF16), m

    def output(t, g, pt, m):
        vt = jnp.concatenate([vt_ref[n, g] for n in neighbours(t)], axis=1)
        ot = jnp.dot(vt, pt, preferred_element_type=F32)
        denom = ot[HEAD_DIM:HEAD_DIM + 1] + jnp.exp2(sink_ref[g] - m)
        return ot[:HEAD_DIM] * (1.0 / denom)

    def store(t, ot):
        rows = slice(t * blk, (t + 1) * blk)
        for j in range(B_HEADS // 2):
            g, i0 = (2 * j) // group, (2 * j) % group
            tile_t = jnp.concatenate([ot[g][:, i0 * blk:(i0 + 1) * blk],
                                      ot[g][:, (i0 + 1) * blk:(i0 + 2) * blk]], axis=0)
            gate = gate_ref[rows, j * LANES:(j + 1) * LANES].astype(F32)
            o_ref[rows, j * LANES:(j + 1) * LANES] = (tile_t.T * gate).astype(BF16)

    items = [(t, g) for t in range(blocks_per_step) for g in range(B_KV)]
    sm, pb, outs = {}, {}, {}
    for n in range(len(items) + 2):
        if n < len(items):
            sm[n] = scores(*items[n])
        if 1 <= n <= len(items):
            pb[n - 1] = probs(*sm.pop(n - 1))
        if n >= 2:
            t, g = items[n - 2]
            outs[g] = output(t, g, *pb.pop(n - 2))
            if g == B_KV - 1:
                store(t, outs)


def _window_attn(qbt, kb, vbt, bias_t, sink_cols, gate, blk, blocks_per_step):
    b, s, _ = kb.shape
    nb = s // blk
    rows = (B_HEADS // B_KV) * blk
    tq = blk * blocks_per_step
    tm = qbt.shape[-1]
    assert tm % tq == 0
    per_slab = tm // tq
    vrows = vbt.shape[2]
    vbt = vbt.reshape(b, nb, B_KV, vrows, blk)
    qbt = qbt.reshape(b, s // tm, B_HEADS, LANES, tm)
    return pl.pallas_call(
        functools.partial(_window_attn_kernel, blk=blk),
        out_shape=jax.ShapeDtypeStruct((b, s, B_HEADS * HEAD_DIM), BF16),
        grid=(b, s // tq),
        in_specs=[pl.BlockSpec((None, None, B_HEADS, LANES, tq),
                               lambda bi, i: (bi, i // per_slab, 0, 0, i % per_slab)),
                  pl.BlockSpec((None, s, LANES), lambda bi, i: (bi, 0, 0)),
                  pl.BlockSpec((None, nb, B_KV, vrows, blk), lambda bi, i: (bi, 0, 0, 0, 0)),
                  pl.BlockSpec((B_KV, 3 * blk, rows), lambda bi, i: (0, 0, 0)),
                  pl.BlockSpec((B_KV, 1, rows), lambda bi, i: (0, 0, 0)),
                  pl.BlockSpec((None, tq, B_HEADS * HEAD_DIM), lambda bi, i: (bi, i, 0))],
        out_specs=pl.BlockSpec((None, tq, B_HEADS * HEAD_DIM), lambda bi, i: (bi, i, 0)),
        compiler_params=_params("arbitrary", "arbitrary"),
        name="window_attn",
    )(qbt, kb, vbt, bias_t, sink_cols, gate)


def _mem_attn_items(q_ref, kv_ref, gate_ref, o_ref, between=(), sub=256):
    between = list(between)
    mlen = kv_ref.shape[0]
    ones = jnp.ones((mlen, M_HEAD_DIM), BF16)

    def scores(t, h):
        lo, hi = h * M_HEAD_DIM, (h + 1) * M_HEAD_DIM
        s = lax.dot_general(q_ref[t * sub:(t + 1) * sub, lo:hi], kv_ref[:, lo:hi], _NT,
                            preferred_element_type=F32)
        return s, jnp.max(s, axis=-1, keepdims=True)

    def probs(s, m):
        return jnp.exp2(s - m).astype(BF16)

    def output(t, h, p):
        lo, hi = h * M_HEAD_DIM, (h + 1) * M_HEAD_DIM
        v = jnp.concatenate([kv_ref[:, MEM_W + lo:MEM_W + hi], ones], axis=1)
        o = jnp.dot(p, v, preferred_element_type=F32)
        o = o[:, :M_HEAD_DIM] * (1.0 / o[:, M_HEAD_DIM:])
        rows = slice(t * sub, (t + 1) * sub)
        o_ref[rows, lo:hi] = (o * gate_ref[rows, lo:hi].astype(F32)).astype(BF16)

    items = [(t, h) for t in range(q_ref.shape[0] // sub) for h in range(M_HEADS)]
    sm, pb = {}, {}
    for n in range(len(items) + 2):
        if n < len(items):
            sm[n] = scores(*items[n])
        if 1 <= n <= len(items):
            pb[n - 1] = probs(*sm.pop(n - 1))
        if n >= 2:
            output(*items[n - 2], pb.pop(n - 2))
        if between:
            between.pop(0)()
    for run in between:
        run()


def _nbr_attn_kernel(q_ref, k_ref, v_ref, bias_ref, gate_ref, o_ref, *, grid_rows, rows_per_step):
    rb = pl.program_id(2)
    nkeys = NA_ROWS * GRID_W
    n_pairs = q_ref.shape[1] // LANES
    ones = jnp.ones((nkeys, LANES), BF16)
    starts, deltas = [], []
    for t in range(rows_per_step):
        r = rb * rows_per_step + t
        rs = jnp.clip(r - NA_ROWS // 2, 0, grid_rows - NA_ROWS)
        starts.append(pl.multiple_of(rs * GRID_W, GRID_W))
        deltas.append(r - rs)
    items = [(t, j) for t in range(rows_per_step) for j in range(n_pairs)]

    def scores(t, j):
        lo, hi = j * LANES, (j + 1) * LANES
        qt = q_ref[t * GRID_W:(t + 1) * GRID_W, lo:hi]
        lane = _lane_iota(qt.shape)
        zero = jnp.zeros_like(qt)
        q = jnp.concatenate([jnp.where(lane < HEAD_DIM, qt, zero),
                             jnp.where(lane >= HEAD_DIM, qt, zero)], axis=0)
        k = k_ref[pl.ds(starts[t], nkeys), lo:hi]
        s = lax.dot_general(q, k, _NT, preferred_element_type=F32) + bias_ref[deltas[t], j]
        return s, jnp.max(s, axis=-1, keepdims=True)

    def probs(s, m):
        return jnp.exp2(s - m).astype(BF16)

    def output(t, j, p):
        lo, hi = j * LANES, (j + 1) * LANES
        v = jnp.concatenate([v_ref[pl.ds(starts[t], nkeys), lo:hi], ones], axis=1)
        o = jnp.dot(p, v, preferred_element_type=F32)
        o = o[:, :LANES] * (1.0 / o[:, LANES:])
        lane_o = _lane_iota((GRID_W, LANES))
        tile = jnp.where(lane_o < HEAD_DIM, o[:GRID_W, :], o[GRID_W:, :])
        rows = slice(t * GRID_W, (t + 1) * GRID_W)
        o_ref[rows, lo:hi] = (tile * gate_ref[rows, lo:hi].astype(F32)).astype(BF16)

    sm = {}
    pb = {}
    for n in range(len(items) + 2):
        if n < len(items):
            sm[n] = scores(*items[n])
        if 1 <= n <= len(items):
            pb[n - 1] = probs(*sm.pop(n - 1))
        if n >= 2:
            output(*items[n - 2], pb.pop(n - 2))


def _nbr_attn(qc, kc, vc, bias, gate, rows_per_step, head_splits):
    b, s, w = qc.shape
    grid_rows = s // GRID_W
    nkeys = NA_ROWS * GRID_W
    wh = w // head_splits
    tq = rows_per_step * GRID_W
    blk = lambda hh, bi, rb: (bi, rb, hh)
    return pl.pallas_call(
        functools.partial(_nbr_attn_kernel, grid_rows=grid_rows, rows_per_step=rows_per_step),
        out_shape=jax.ShapeDtypeStruct((b, s, w), BF16),
        grid=(head_splits, b, grid_rows // rows_per_step),
        in_specs=[pl.BlockSpec((None, tq, wh), blk),
                  pl.BlockSpec((None, s, wh), lambda hh, bi, rb: (bi, 0, hh)),
                  pl.BlockSpec((None, s, wh), lambda hh, bi, rb: (bi, 0, hh)),
                  pl.BlockSpec((NA_ROWS, wh // LANES, 2 * GRID_W, nkeys), lambda hh, bi, rb: (0, hh, 0, 0)),
                  pl.BlockSpec((None, tq, wh), blk)],
        out_specs=pl.BlockSpec((None, tq, wh), blk),
        compiler_params=_params("arbitrary", "arbitrary", "arbitrary"),
        name="nbr_attn",
    )(qc, kc, vc, bias, gate)


def _out_final_kernel(yc_ref, ym_ref, x_ref, w_ref, g_ref, o_ref):
    acc = x_ref[...]
    off = 0
    for y_ref in (yc_ref, ym_ref):
        width = y_ref.shape[1]
        acc = acc + jnp.dot(y_ref[...], w_ref[off:off + width, :], preferred_element_type=F32)
        off += width
    o_ref[...] = _rms_rows(acc, g_ref[...])


def _out_final(yc, ym, x2d, w_bf16, final_gain, tm):
    n_tok, d = x2d.shape
    row = lambda i: (i, 0)
    const = lambda i: (0, 0)
    return pl.pallas_call(
        _out_final_kernel,
        out_shape=jax.ShapeDtypeStruct((n_tok, d), F32),
        grid=(n_tok // tm,),
        in_specs=[pl.BlockSpec((tm, yc.shape[1]), row),
                  pl.BlockSpec((tm, ym.shape[1]), row),
                  pl.BlockSpec((tm, d), row),
                  pl.BlockSpec(w_bf16.shape, const),
                  pl.BlockSpec((1, d), const)],
        out_specs=pl.BlockSpec((tm, d), row),
        compiler_params=_params("arbitrary"),
        name="out_proj_final",
    )(yc, ym, x2d, w_bf16, final_gain)


def _rope_tables(seq):
    quarter = HEAD_DIM // 4
    freqs = jnp.power(ROPE_THETA, -jnp.arange(quarter, dtype=F32) / quarter)
    t = jnp.arange(seq)
    ang_r = (t // GRID_W).astype(F32)[:, None] * freqs
    ang_c = (t % GRID_W).astype(F32)[:, None] * freqs
    cos_h = jnp.concatenate([jnp.cos(ang_r), jnp.cos(ang_r), jnp.cos(ang_c), jnp.cos(ang_c)], axis=-1)
    sin_h = jnp.concatenate([-jnp.sin(ang_r), jnp.sin(ang_r), -jnp.sin(ang_c), jnp.sin(ang_c)], axis=-1)
    return jnp.tile(cos_h, (1, 2)), jnp.tile(sin_h, (1, 2))


def _t5_bucket(rel):
    nb = REL_BUCKETS // 2
    max_exact = nb // 2
    ret = jnp.where(rel > 0, nb, 0)
    n = jnp.abs(rel)
    nf = jnp.maximum(n, 1).astype(F32)
    large = max_exact + (jnp.log(nf / max_exact) / math.log(REL_MAX_DIST / max_exact)
                         * (nb - max_exact)).astype(jnp.int32)
    large = jnp.minimum(large, nb - 1)
    return ret + jnp.where(n < max_exact, n, large)


def _window_bias(rel_bias, blk):
    span = blk + 2 * WINDOW
    reach = span - WINDOW - 1
    rel = jnp.arange(-reach, reach + 1)
    per_rel = jnp.where((jnp.abs(rel) <= WINDOW)[:, None],
                        rel_bias.astype(F32)[_t5_bucket(rel)] * LOG2E, NEG).T
    bias = jnp.stack([per_rel[:, reach - WINDOW - q:reach - WINDOW - q + span] for q in range(blk)],
                     axis=-1)
    group = B_HEADS // B_KV
    bias = bias.reshape(B_KV, group, span, blk).transpose(0, 2, 1, 3)
    return bias.reshape(B_KV, span, group * blk)


def _nbr_bias(rpb):
    col = np.arange(GRID_W)
    cs = np.clip(col - NA_COLS // 2, 0, GRID_W - NA_COLS)
    colmask = (col[None, :] >= cs[:, None]) & (col[None, :] < cs[:, None] + NA_COLS)
    dc = np.clip(col[None, :] - col[:, None] + NA_COLS - 1, 0, 2 * NA_COLS - 2)
    onehot = (dc[None] == np.arange(2 * NA_COLS - 1)[:, None, None]) & colmask[None]
    t = jnp.einsum('hrc,cqk->hqrk', rpb.astype(F32) * LOG2E, jnp.asarray(onehot, F32),
                   precision=lax.Precision.HIGHEST)
    t = t + jnp.asarray(np.where(colmask, 0.0, NEG), F32)[None, :, None, :]
    delta, i = np.meshgrid(np.arange(NA_ROWS), np.arange(NA_ROWS), indexing="ij")
    pick = (i - delta + NA_ROWS - 1)[:, :, None] == np.arange(2 * NA_ROWS - 1)
    out = jnp.einsum('dir,hqrk->dhqik', jnp.asarray(pick, F32), t, precision=lax.Precision.HIGHEST)
    return out.reshape(NA_ROWS, C_HEADS // 2, 2 * GRID_W, NA_ROWS * GRID_W)


def kernel(x, mem, norm_gain, mem_norm_gain, w_in_even, w_out_even, q_norm_a, k_norm_a, sink_b,
           rel_bias, w_in_odd, w_out_odd, rpb_c, w_mem_kv, final_norm_gain):
    b, s, d = x.shape
    mlen = mem.shape[1]
    assert s % GRID_W == 0 and s // GRID_W >= NA_ROWS and s % 512 == 0
    tm = 512
    blk = 128

    x2d = x.reshape(b * s, d)
    mem_tm = math.gcd(b * mlen, 512)
    mkv = _mem_kv(mem.reshape(b * mlen, d), mem_norm_gain.reshape(1, d), w_mem_kv.astype(BF16), mem_tm)
    mkv = mkv.reshape(w_mem_kv.shape[0], b, mlen, 2 * MEM_W)

    cos, sin = _rope_tables(s)
    gq = jnp.tile(q_norm_a[0].astype(F32), 2).reshape(1, LANES)
    gk = jnp.tile(k_norm_a[0].astype(F32), 2).reshape(1, LANES)
    qa, ka, va, qb, kb, vb, ga, gb, ym = _in_even(
        x2d, norm_gain[0].reshape(1, d), w_in_even[0].astype(BF16), gq, gk, cos, sin, mkv, s,
        math.gcd(s, 2 * tm))
    r3 = lambda a: a.reshape(b, s, a.shape[-1])
    ya = _global_attn(qa, r3(ka), va, r3(ga), tq=256, n_qb=4)
    group_b = B_HEADS // B_KV
    sink_cols = jnp.repeat(sink_b[0].astype(F32).reshape(B_KV, group_b) * LOG2E, blk,
                           axis=1).reshape(B_KV, 1, group_b * blk)
    yb = _window_attn(qb, r3(kb), vb, _window_bias(rel_bias, blk), sink_cols, r3(gb), blk,
                      blocks_per_step=4)

    x1, qc, kc, vc, gc, ym1 = _mid(ya.reshape(b * s, -1), yb.reshape(b * s, -1), ym, x2d,
                                   w_out_even[0].astype(BF16), norm_gain[1].reshape(1, d),
                                   w_in_odd[0].astype(BF16), mkv, s, tm)
    yc = _nbr_attn(r3(qc), r3(kc), r3(vc), _nbr_bias(rpb_c[0]), r3(gc), rows_per_step=8, head_splits=2)
    out = _out_final(yc.reshape(b * s, -1), ym1, x1, w_out_odd[0].astype(BF16),
                     final_norm_gain.reshape(1, d), 2 * tm)
    return out.reshape(b, s, d)
```

```python
import functools
import math

import jax
import jax.numpy as jnp
import numpy as np
from jax import lax
from jax.experimental import pallas as pl
from jax.experimental.pallas import tpu as pltpu

GRID_W = 64
HEAD_DIM = 64
A_HEADS = 8
A_KV = 2
B_HEADS = 8
B_KV = 2
WINDOW = 128
C_HEADS = 16
NA_ROWS = 8
NA_COLS = 16
M_HEADS = 4
M_HEAD_DIM = 128
MEM_W = M_HEADS * M_HEAD_DIM
REL_BUCKETS = 32
REL_MAX_DIST = 128
ROPE_THETA = 10000.0
EPS = 1e-6

LANES = 128
NEG = -1e30
LOG2E = math.log2(math.e)
V_ONES_ROWS = 16
A_CHUNK = 512
VMEM_LIMIT_BYTES = 56 * 1024 * 1024

F32 = jnp.float32
BF16 = jnp.bfloat16

_NT = (((1,), (1,)), ((), ()))


def _params(*sem):
    return pltpu.CompilerParams(dimension_semantics=sem, vmem_limit_bytes=VMEM_LIMIT_BYTES)


def _rms_rows(x, gain):
    ms = jnp.mean(x * x, axis=-1, keepdims=True)
    return x * lax.rsqrt(ms + EPS) * gain


def _lane_iota(shape):
    return lax.broadcasted_iota(jnp.int32, shape, len(shape) - 1)


def _silu(x):
    return x * (1.0 / (1.0 + jnp.exp(-x)))


def _half_ones():
    r = lax.broadcasted_iota(jnp.int32, (LANES, LANES), 0) // HEAD_DIM
    c = lax.broadcasted_iota(jnp.int32, (LANES, LANES), 1) // HEAD_DIM
    return jnp.where(r == c, 1.0, 0.0).astype(BF16)


def _head_norm_rope(t, gain, cos, sin_signed, ones_bd):
    ss = jnp.dot((t * t).astype(BF16), ones_bd, preferred_element_type=F32)
    tn = t * lax.rsqrt(ss * (1.0 / HEAD_DIM) + EPS) * gain
    lane = _lane_iota(tn.shape)
    quarter = HEAD_DIM // 4
    partner = jnp.where((lane % (2 * quarter)) < quarter,
                        pltpu.roll(tn, LANES - quarter, 1), pltpu.roll(tn, quarter, 1))
    return tn * cos + partner * sin_signed


def _store_padded_heads(q_tiles, out_ref, n_heads, n_kv, transposed=False):
    group = n_heads // n_kv
    for h in range(n_heads):
        t = q_tiles[h // 2]
        src_half = h % 2
        dst_half = (h // group) % 2
        if src_half != dst_half:
            t = pltpu.roll(t, HEAD_DIM, 1)
        lane = _lane_iota(t.shape)
        keep = (lane >= HEAD_DIM) if dst_half == 1 else (lane < HEAD_DIM)
        padded = jnp.where(keep, t, 0.0)
        if transposed:
            out_ref[h] = padded.T.astype(BF16)
        else:
            out_ref[:, h * LANES:(h + 1) * LANES] = padded.astype(BF16)


def _mem_kv_kernel(mem_ref, g_ref, w_ref, o_ref):
    h = _rms_rows(mem_ref[...], g_ref[...]).astype(BF16)
    o_ref[...] = jnp.dot(h, w_ref[...], preferred_element_type=F32).astype(BF16)


def _mem_kv(mem2d, gain, w_bf16, tm):
    depth, d, n = w_bf16.shape
    rows = mem2d.shape[0]
    return pl.pallas_call(
        _mem_kv_kernel,
        out_shape=jax.ShapeDtypeStruct((depth, rows, n), BF16),
        grid=(depth, rows // tm),
        in_specs=[pl.BlockSpec((tm, d), lambda l, i: (i, 0)),
                  pl.BlockSpec((1, d), lambda l, i: (0, 0)),
                  pl.BlockSpec((None, d, n), lambda l, i: (l, 0, 0))],
        out_specs=pl.BlockSpec((None, tm, n), lambda l, i: (l, i, 0)),
        compiler_params=_params("arbitrary", "arbitrary"),
        name="mem_kv_proj",
    )(mem2d, gain, w_bf16)


def _in_even_kernel(x_ref, g_ref, w_ref, gq_ref, gk_ref, cos_ref, sin_ref, mkv_ref,
                    qa_ref, ka_ref, va_ref, qb_ref, kb_ref, vb_ref, ga_ref, gb_ref, ym_ref,
                    qm_scr, gm_scr):
    h = _rms_rows(x_ref[...], g_ref[...]).astype(BF16)

    def seg(lo, hi):
        return jnp.dot(h, w_ref[:, lo:hi], preferred_element_type=F32)

    ones_bd = _half_ones()
    scale = HEAD_DIM ** -0.5
    qa_w = A_HEADS * HEAD_DIM
    kva_w = A_KV * HEAD_DIM
    qb_w = B_HEADS * HEAD_DIM
    kvb_w = B_KV * HEAD_DIM
    offs = np.cumsum([0, qa_w, kva_w, kva_w, qb_w, kvb_w, kvb_w, MEM_W, qa_w, qb_w, MEM_W])
    o_qa, o_ka, o_va, o_qb, o_kb, o_vb, o_qm, o_ga, o_gb, o_gm = (int(v) for v in offs[:-1])

    def do_qa():
        zq = seg(o_qa, o_qa + qa_w)
        tiles = [_head_norm_rope(zq[:, j * LANES:(j + 1) * LANES], gq_ref[...], cos_ref[...],
                                 sin_ref[...], ones_bd) * (scale * LOG2E) for j in range(qa_w // LANES)]
        _store_padded_heads(tiles, qa_ref, A_HEADS, A_KV, transposed=True)

    def do_ka():
        ka_ref[...] = _head_norm_rope(seg(o_ka, o_ka + kva_w), gk_ref[...], cos_ref[...],
                                      sin_ref[...], ones_bd).astype(BF16)

    def do_va():
        vt = seg(o_va, o_va + kva_w).T.astype(BF16)
        for j in range(va_ref.shape[0]):
            for g in range(A_KV):
                va_ref[j, g, :HEAD_DIM] = vt[g * HEAD_DIM:(g + 1) * HEAD_DIM,
                                             j * A_CHUNK:(j + 1) * A_CHUNK]
                va_ref[j, g, HEAD_DIM:] = jnp.ones((V_ONES_ROWS, A_CHUNK), BF16)

    def do_qb():
        zq = seg(o_qb, o_qb + qb_w) * (scale * LOG2E)
        _store_padded_heads([zq[:, j * LANES:(j + 1) * LANES] for j in range(qb_w // LANES)],
                            qb_ref, B_HEADS, B_KV, transposed=True)

    def do_kb():
        kb_ref[...] = seg(o_kb, o_kb + kvb_w).astype(BF16)

    def do_vb():
        vt = seg(o_vb, o_vb + kvb_w).T.astype(BF16)
        for j in range(vb_ref.shape[0]):
            for g in range(B_KV):
                vb_ref[j, g, :HEAD_DIM] = vt[g * HEAD_DIM:(g + 1) * HEAD_DIM, j * WINDOW:(j + 1) * WINDOW]
                vb_ref[j, g, HEAD_DIM:] = jnp.ones((V_ONES_ROWS, WINDOW), BF16)

    def do_ga():
        ga_ref[...] = _silu(seg(o_ga, o_ga + qa_w)).astype(BF16)

    def do_gb():
        gb_ref[...] = _silu(seg(o_gb, o_gb + qb_w)).astype(BF16)

    qm_scr[...] = (seg(o_qm, o_qm + MEM_W) * (M_HEAD_DIM ** -0.5 * LOG2E)).astype(BF16)
    gm_scr[...] = _silu(seg(o_gm, o_gm + MEM_W)).astype(BF16)
    _mem_attn_items(qm_scr, mkv_ref, gm_scr, ym_ref,
                    between=[do_qa, do_ga, do_qb, do_gb, do_va, do_ka, do_vb, do_kb])


def _in_even(x2d, gain, w_bf16, gq, gk, cos, sin, mkv, seq, tm):
    n_tok, d = x2d.shape
    n_in = w_bf16.shape[1]
    per_seq = seq // tm
    mlen = mkv.shape[2]
    widths = [A_HEADS * LANES, A_KV * HEAD_DIM, A_KV * HEAD_DIM,
              B_HEADS * LANES, B_KV * HEAD_DIM, B_KV * HEAD_DIM,
              A_HEADS * HEAD_DIM, B_HEADS * HEAD_DIM, MEM_W]
    row = lambda i: (i, 0)
    const = lambda i: (0, 0)
    out_shape = [jax.ShapeDtypeStruct((n_tok, w), BF16) for w in widths]
    out_specs = [pl.BlockSpec((tm, w), row) for w in widths]
    vrows = HEAD_DIM + V_ONES_ROWS
    out_shape[0] = jax.ShapeDtypeStruct((n_tok // tm, A_HEADS, LANES, tm), BF16)
    out_specs[0] = pl.BlockSpec((None, A_HEADS, LANES, tm), lambda i: (i, 0, 0, 0))
    out_shape[3] = jax.ShapeDtypeStruct((n_tok // tm, B_HEADS, LANES, tm), BF16)
    out_specs[3] = pl.BlockSpec((None, B_HEADS, LANES, tm), lambda i: (i, 0, 0, 0))
    out_shape[2] = jax.ShapeDtypeStruct((n_tok // A_CHUNK, A_KV, vrows, A_CHUNK), BF16)
    out_specs[2] = pl.BlockSpec((tm // A_CHUNK, A_KV, vrows, A_CHUNK), lambda i: (i, 0, 0, 0))
    out_shape[5] = jax.ShapeDtypeStruct((n_tok // WINDOW, B_KV, vrows, WINDOW), BF16)
    out_specs[5] = pl.BlockSpec((tm // WINDOW, B_KV, vrows, WINDOW), lambda i: (i, 0, 0, 0))
    return pl.pallas_call(
        _in_even_kernel,
        out_shape=out_shape,
        grid=(n_tok // tm,),
        in_specs=[pl.BlockSpec((tm, d), row),
                  pl.BlockSpec((1, d), const),
                  pl.BlockSpec((d, n_in), const, pipeline_mode=pl.Buffered(1)),
                  pl.BlockSpec((1, LANES), const),
                  pl.BlockSpec((1, LANES), const),
                  pl.BlockSpec((tm, LANES), lambda i: (i % per_seq, 0)),
                  pl.BlockSpec((tm, LANES), lambda i: (i % per_seq, 0)),
                  pl.BlockSpec((None, None, mlen, 2 * MEM_W), lambda i: (0, i // per_seq, 0, 0))],
        out_specs=out_specs,
        scratch_shapes=[pltpu.VMEM((tm, MEM_W), BF16), pltpu.VMEM((tm, MEM_W), BF16)],
        compiler_params=_params("arbitrary"),
        name="in_proj_even",
    )(x2d, gain, w_bf16, gq, gk, cos, sin, mkv)


def _mid_kernel(ya_ref, yb_ref, ym_ref, x_ref, wo_ref, g_ref, wi_ref, mkv_ref,
                x1_ref, qc_ref, kc_ref, vc_ref, gc_ref, ym1_ref, qm_scr, gm_scr):
    acc = x_ref[...]
    off = 0
    for y_ref in (ya_ref, yb_ref, ym_ref):
        width = y_ref.shape[1]
        acc = acc + jnp.dot(y_ref[...], wo_ref[off:off + width, :], preferred_element_type=F32)
        off += width
    x1_ref[...] = acc
    h = _rms_rows(acc, g_ref[...]).astype(BF16)

    def seg(lo, hi):
        return jnp.dot(h, wi_ref[:, lo:hi], preferred_element_type=F32)

    cw = C_HEADS * HEAD_DIM
    half = cw // 2
    o_qc, o_kc, o_vc, o_qm, o_gc, o_gm = 0, cw, 2 * cw, 3 * cw, 3 * cw + MEM_W, 4 * cw + MEM_W

    def thunk(ref, base, part, fn):
        def run():
            lo = part * half
            ref[:, lo:lo + half] = fn(seg(base + lo, base + lo + half)).astype(BF16)
        return run

    plain = lambda z: z
    to_q = lambda z: z * (HEAD_DIM ** -0.5 * LOG2E)
    segments = [thunk(ref, base, part, fn)
                for ref, base, fn in ((gc_ref, o_gc, _silu), (qc_ref, o_qc, to_q),
                                      (kc_ref, o_kc, plain), (vc_ref, o_vc, plain))
                for part in range(2)]
    qm_scr[...] = (seg(o_qm, o_qm + MEM_W) * (M_HEAD_DIM ** -0.5 * LOG2E)).astype(BF16)
    gm_scr[...] = _silu(seg(o_gm, o_gm + MEM_W)).astype(BF16)
    _mem_attn_items(qm_scr, mkv_ref, gm_scr, ym1_ref, between=segments)


def _mid(ya, yb, ym, x2d, wo_bf16, gain, wi_bf16, mkv, seq, tm):
    n_tok, d = x2d.shape
    per_seq = seq // tm
    mlen = mkv.shape[2]
    cw = C_HEADS * HEAD_DIM
    row = lambda i: (i, 0)
    const = lambda i: (0, 0)
    once = pl.Buffered(1)
    out_widths = [cw, cw, cw, cw, MEM_W]
    return pl.pallas_call(
        _mid_kernel,
        out_shape=[jax.ShapeDtypeStruct((n_tok, d), F32)]
                  + [jax.ShapeDtypeStruct((n_tok, w), BF16) for w in out_widths],
        grid=(n_tok // tm,),
        in_specs=[pl.BlockSpec((tm, ya.shape[1]), row),
                  pl.BlockSpec((tm, yb.shape[1]), row),
                  pl.BlockSpec((tm, ym.shape[1]), row),
                  pl.BlockSpec((tm, d), row),
                  pl.BlockSpec(wo_bf16.shape, const, pipeline_mode=once),
                  pl.BlockSpec((1, d), const),
                  pl.BlockSpec(wi_bf16.shape, const, pipeline_mode=once),
                  pl.BlockSpec((None, None, mlen, 2 * MEM_W), lambda i: (1, i // per_seq, 0, 0))],
        out_specs=[pl.BlockSpec((tm, d), row)] + [pl.BlockSpec((tm, w), row) for w in out_widths],
        scratch_shapes=[pltpu.VMEM((tm, MEM_W), BF16), pltpu.VMEM((tm, MEM_W), BF16)],
        compiler_params=_params("arbitrary"),
        name="out_proj_in_proj_odd",
    )(ya, yb, ym, x2d, wo_bf16, gain, wi_bf16, mkv)


def _global_attn_kernel(q_ref, k_ref, vt_ref, gate_ref, o_ref, m_scr, acc_scr, s_scr, mc_scr):
    n_qb = m_scr.shape[0]
    slab_w = q_ref.shape[3]
    tq = q_ref.shape[0] * slab_w // n_qb
    n_chunks = vt_ref.shape[0]
    tk = vt_ref.shape[3]
    group = A_HEADS // A_KV

    def q_block(b, h):
        lo = b * tq
        return q_ref[lo // slab_w, h][:, lo % slab_w:lo % slab_w + tq]

    qts = [[jnp.concatenate([q_block(b, g * group + i) for i in range(group)], axis=1)
            for g in range(A_KV)] for b in range(n_qb)]
    m_scr[...] = jnp.full(m_scr.shape, NEG, F32)
    acc_scr[...] = jnp.zeros(acc_scr.shape, F32)

    parts = 2
    tp = tk // parts
    pieces = [(p, g) for p in range(parts) for g in range(A_KV)]

    def scores_piece(b, c, slot, p, g):
        start = pl.multiple_of(c * tk + p * tp, tp)
        st = jnp.dot(k_ref[pl.ds(start, tp), :], qts[b][g], preferred_element_type=F32)
        s_scr[slot, g, p * tp:(p + 1) * tp] = st
        mx = jnp.max(st, axis=0, keepdims=True)
        mc_scr[slot, g] = mx if p == 0 else jnp.maximum(mc_scr[slot, g], mx)

    def accumulate_piece(b, c, slot, p, g):
        m_old = m_scr[b, g]
        m_new = jnp.maximum(m_old, mc_scr[slot, g])
        pt = jnp.exp2(s_scr[slot, g, p * tp:(p + 1) * tp] - m_new).astype(BF16)
        pv = jnp.dot(vt_ref[c, g][:, p * tp:(p + 1) * tp], pt, preferred_element_type=F32)
        if p == 0:
            acc_scr[b, g] = jnp.exp2(m_old - m_new) * acc_scr[b, g] + pv
        else:
            acc_scr[b, g] = acc_scr[b, g] + pv
        if p == parts - 1:
            m_scr[b, g] = m_new

    def scores_and_accumulate(score_of, accumulate_of):
        for p, g in pieces:
            if score_of is not None:
                scores_piece(*score_of, p, g)
            if accumulate_of is not None:
                accumulate_piece(*accumulate_of, p, g)

    def finalize(b):
        rows = slice(b * tq, (b + 1) * tq)
        ot = [acc_scr[b, g, :HEAD_DIM] * (1.0 / acc_scr[b, g, HEAD_DIM:HEAD_DIM + 1])
              for g in range(A_KV)]
        for j in range(A_HEADS // 2):
            g, i0 = (2 * j) // group, (2 * j) % group
            tile_t = jnp.concatenate([ot[g][:, i0 * tq:(i0 + 1) * tq],
                                      ot[g][:, (i0 + 1) * tq:(i0 + 2) * tq]], axis=0)
            gate = gate_ref[rows, j * LANES:(j + 1) * LANES].astype(F32)
            o_ref[rows, j * LANES:(j + 1) * LANES] = (tile_t.T * gate).astype(BF16)

    scores_and_accumulate((0, 0, 0), None)
    for b in range(n_qb):
        def body(c2, carry, b=b):
            c = 2 * c2
            scores_and_accumulate((b, c + 1, 1), (b, c, 0))
            scores_and_accumulate((b, c + 2, 0), (b, c + 1, 1))
            return carry

        lax.fori_loop(0, n_chunks // 2 - 1, body, 0)
        scores_and_accumulate((b, n_chunks - 1, 1), (b, n_chunks - 2, 0))
        following = (b + 1, 0, 0) if b + 1 < n_qb else None
        scores_and_accumulate(following, (b, n_chunks - 1, 1))
        finalize(b)


def _global_attn(qat, ka, vat, gate, tq, n_qb):
    b, s, _ = ka.shape
    _, n_kv, vrows, tk = vat.shape
    tm = qat.shape[-1]
    n_chunks = s // tk
    step_q = n_qb * tq
    assert n_chunks % 2 == 0 and (tm % step_q == 0 or step_q % tm == 0)
    rows = (A_HEADS // A_KV) * tq
    vat = vat.reshape(b, n_chunks, n_kv, vrows, tk)
    qat = qat.reshape(b, s // tm, A_HEADS, LANES, tm)
    if step_q >= tm:
        q_spec = pl.BlockSpec((None, step_q // tm, A_HEADS, LANES, tm), lambda bi, i: (bi, i, 0, 0, 0))
    else:
        per_slab = tm // step_q
        q_spec = pl.BlockSpec((None, 1, A_HEADS, LANES, step_q),
                              lambda bi, i: (bi, i // per_slab, 0, 0, i % per_slab))
    return pl.pallas_call(
        _global_attn_kernel,
        out_shape=jax.ShapeDtypeStruct((b, s, A_HEADS * HEAD_DIM), BF16),
        grid=(b, s // step_q),
        in_specs=[q_spec,
                  pl.BlockSpec((None, s, LANES), lambda bi, i: (bi, 0, 0)),
                  pl.BlockSpec((None, n_chunks, n_kv, vrows, tk), lambda bi, i: (bi, 0, 0, 0, 0)),
                  pl.BlockSpec((None, step_q, A_HEADS * HEAD_DIM), lambda bi, i: (bi, i, 0))],
        out_specs=pl.BlockSpec((None, step_q, A_HEADS * HEAD_DIM), lambda bi, i: (bi, i, 0)),
        scratch_shapes=[pltpu.VMEM((n_qb, A_KV, 1, rows), F32),
                        pltpu.VMEM((n_qb, A_KV, vrows, rows), F32),
                        pltpu.VMEM((2, A_KV, tk, rows), F32), pltpu.VMEM((2, A_KV, 1, rows), F32)],
        compiler_params=_params("arbitrary", "arbitrary"),
        name="global_attn",
    )(qat, ka, vat, gate)


def _window_attn_kernel(q_ref, k_ref, vt_ref, bias_ref, sink_ref, gate_ref, o_ref, *, blk):
    blocks_per_step = q_ref.shape[2] // blk
    nb = vt_ref.shape[0]
    step = pl.program_id(1)
    group = B_HEADS // B_KV

    def neighbours(t):
        i = step * blocks_per_step + t
        return jnp.maximum(i - 1, 0), i, jnp.minimum(i + 1, nb - 1)

    def scores(t, g):
        i = step * blocks_per_step + t
        qt = jnp.concatenate([q_ref[g * group + j][:, t * blk:(t + 1) * blk]
                              for j in range(group)], axis=1)
        k = jnp.concatenate([k_ref[pl.ds(pl.multiple_of(n * blk, blk), blk), :]
                             for n in neighbours(t)], axis=0)
        st = jnp.dot(k, qt, preferred_element_type=F32) + bias_ref[g]
        if t == 0:
            st = jnp.concatenate([st[:blk] + jnp.where(i == 0, NEG, 0.0), st[blk:]], axis=0)
        if t == blocks_per_step - 1:
            st = jnp.concatenate([st[:2 * blk], st[2 * blk:] + jnp.where(i == nb - 1, NEG, 0.0)], axis=0)
        m = jnp.maximum(jnp.max(st, axis=0, keepdims=True), sink_ref[g])
        return st, m

    def probs(st, m):
        return jnp.exp2(st - m).astype(BF16), m

    def output(t, g, pt, m):
        vt = jnp.concatenate([vt_ref[n, g] for n in neighbours(t)], axis=1)
        ot = jnp.dot(vt, pt, preferred_element_type=F32)
        denom = ot[HEAD_DIM:HEAD_DIM + 1] + jnp.exp2(sink_ref[g] - m)
        return ot[:HEAD_DIM] * (1.0 / denom)

    def store(t, ot):
        rows = slice(t * blk, (t + 1) * blk)
        for j in range(B_HEADS // 2):
            g, i0 = (2 * j) // group, (2 * j) % group
            tile_t = jnp.concatenate([ot[g][:, i0 * blk:(i0 + 1) * blk],
                                      ot[g][:, (i0 + 1) * blk:(i0 + 2) * blk]], axis=0)
            gate = gate_ref[rows, j * LANES:(j + 1) * LANES].astype(F32)
            o_ref[rows, j * LANES:(j + 1) * LANES] = (tile_t.T * gate).astype(BF16)

    items = [(t, g) for t in range(blocks_per_step) for g in range(B_KV)]
    sm, pb, outs = {}, {}, {}
    for n in range(len(items) + 2):
        if n < len(items):
            sm[n] = scores(*items[n])
        if 1 <= n <= len(items):
            pb[n - 1] = probs(*sm.pop(n - 1))
        if n >= 2:
            t, g = items[n - 2]
            outs[g] = output(t, g, *pb.pop(n - 2))
            if g == B_KV - 1:
                store(t, outs)


def _window_attn(qbt, kb, vbt, bias_t, sink_cols, gate, blk, blocks_per_step):
    b, s, _ = kb.shape
    nb = s // blk
    rows = (B_HEADS // B_KV) * blk
    tq = blk * blocks_per_step
    tm = qbt.shape[-1]
    assert tm % tq == 0
    per_slab = tm // tq
    vrows = vbt.shape[2]
    vbt = vbt.reshape(b, nb, B_KV, vrows, blk)
    qbt = qbt.reshape(b, s // tm, B_HEADS, LANES, tm)
    return pl.pallas_call(
        functools.partial(_window_attn_kernel, blk=blk),
        out_shape=jax.ShapeDtypeStruct((b, s, B_HEADS * HEAD_DIM), BF16),
        grid=(b, s // tq),
        in_specs=[pl.BlockSpec((None, None, B_HEADS, LANES, tq),
                               lambda bi, i: (bi, i // per_slab, 0, 0, i % per_slab)),
                  pl.BlockSpec((None, s, LANES), lambda bi, i: (bi, 0, 0)),
                  pl.BlockSpec((None, nb, B_KV, vrows, blk), lambda bi, i: (bi, 0, 0, 0, 0)),
                  pl.BlockSpec((B_KV, 3 * blk, rows), lambda bi, i: (0, 0, 0)),
                  pl.BlockSpec((B_KV, 1, rows), lambda bi, i: (0, 0, 0)),
                  pl.BlockSpec((None, tq, B_HEADS * HEAD_DIM), lambda bi, i: (bi, i, 0))],
        out_specs=pl.BlockSpec((None, tq, B_HEADS * HEAD_DIM), lambda bi, i: (bi, i, 0)),
        compiler_params=_params("arbitrary", "arbitrary"),
        name="window_attn",
    )(qbt, kb, vbt, bias_t, sink_cols, gate)


def _mem_attn_items(q_ref, kv_ref, gate_ref, o_ref, between=(), sub=256):
    between = list(between)
    mlen = kv_ref.shape[0]
    ones = jnp.ones((mlen, M_HEAD_DIM), BF16)

    def scores(t, h):
        lo, hi = h * M_HEAD_DIM, (h + 1) * M_HEAD_DIM
        s = lax.dot_general(q_ref[t * sub:(t + 1) * sub, lo:hi], kv_ref[:, lo:hi], _NT,
                            preferred_element_type=F32)
        return s, jnp.max(s, axis=-1, keepdims=True)

    def probs(s, m):
        return jnp.exp2(s - m).astype(BF16)

    def output(t, h, p):
        lo, hi = h * M_HEAD_DIM, (h + 1) * M_HEAD_DIM
        v = jnp.concatenate([kv_ref[:, MEM_W + lo:MEM_W + hi], ones], axis=1)
        o = jnp.dot(p, v, preferred_element_type=F32)
        o = o[:, :M_HEAD_DIM] * (1.0 / o[:, M_HEAD_DIM:])
        rows = slice(t * sub, (t + 1) * sub)
        o_ref[rows, lo:hi] = (o * gate_ref[rows, lo:hi].astype(F32)).astype(BF16)

    items = [(t, h) for t in range(q_ref.shape[0] // sub) for h in range(M_HEADS)]
    sm, pb = {}, {}
    for n in range(len(items) + 2):
        if n < len(items):
            sm[n] = scores(*items[n])
        if 1 <= n <= len(items):
            pb[n - 1] = probs(*sm.pop(n - 1))
        if n >= 2:
            output(*items[n - 2], pb.pop(n - 2))
        if between:
            between.pop(0)()
    for run in between:
        run()


def _nbr_attn_kernel(q_ref, k_ref, v_ref, bias_ref, gate_ref, o_ref, *, grid_rows, rows_per_step):
    rb = pl.program_id(2)
    nkeys = NA_ROWS * GRID_W
    n_pairs = q_ref.shape[1] // LANES
    ones = jnp.ones((nkeys, LANES), BF16)
    starts, first_offsets = [], []
    for t in range(rows_per_step):
        r = rb * rows_per_step + t
        rs = jnp.clip(r - NA_ROWS // 2, 0, grid_rows - NA_ROWS)
        starts.append(pl.multiple_of(rs * GRID_W, GRID_W))
        first_offsets.append(rs - r + NA_ROWS - 1)
    items = [(t, j) for t in range(rows_per_step) for j in range(n_pairs)]

    def scores(t, j):
        lo, hi = j * LANES, (j + 1) * LANES
        qt = q_ref[t * GRID_W:(t + 1) * GRID_W, lo:hi]
        lane = _lane_iota(qt.shape)
        zero = jnp.zeros_like(qt)
        q = jnp.concatenate([jnp.where(lane < HEAD_DIM, qt, zero),
                             jnp.where(lane >= HEAD_DIM, qt, zero)], axis=0)
        k = k_ref[pl.ds(starts[t], nkeys), lo:hi]
        par, m0 = first_offsets[t] % 2, first_offsets[t] // 2
        bias = jnp.concatenate([bias_ref[j, par, m0 + i] for i in range(NA_ROWS // 2)], axis=1)
        s = lax.dot_general(q, k, _NT, preferred_element_type=F32) + bias
        return s, jnp.max(s, axis=-1, keepdims=True)

    def probs(s, m):
        return jnp.exp2(s - m).astype(BF16)

    def output(t, j, p):
        lo, hi = j * LANES, (j + 1) * LANES
        v = jnp.concatenate([v_ref[pl.ds(starts[t], nkeys), lo:hi], ones], axis=1)
        o = jnp.dot(p, v, preferred_element_type=F32)
        o = o[:, :LANES] * (1.0 / o[:, LANES:])
        lane_o = _lane_iota((GRID_W, LANES))
        tile = jnp.where(lane_o < HEAD_DIM, o[:GRID_W, :], o[GRID_W:, :])
        rows = slice(t * GRID_W, (t + 1) * GRID_W)
        o_ref[rows, lo:hi] = (tile * gate_ref[rows, lo:hi].astype(F32)).astype(BF16)

    sm = {}
    pb = {}
    for n in range(len(items) + 2):
        if n < len(items):
            sm[n] = scores(*items[n])
        if 1 <= n <= len(items):
            pb[n - 1] = probs(*sm.pop(n - 1))
        if n >= 2:
            output(*items[n - 2], pb.pop(n - 2))


def _nbr_attn(qc, kc, vc, bias, gate, rows_per_step, head_splits):
    b, s, w = qc.shape
    grid_rows = s // GRID_W
    nkeys = NA_ROWS * GRID_W
    wh = w // head_splits
    tq = rows_per_step * GRID_W
    blk = lambda hh, bi, rb: (bi, rb, hh)
    return pl.pallas_call(
        functools.partial(_nbr_attn_kernel, grid_rows=grid_rows, rows_per_step=rows_per_step),
        out_shape=jax.ShapeDtypeStruct((b, s, w), BF16),
        grid=(head_splits, b, grid_rows // rows_per_step),
        in_specs=[pl.BlockSpec((None, tq, wh), blk),
                  pl.BlockSpec((None, s, wh), lambda hh, bi, rb: (bi, 0, hh)),
                  pl.BlockSpec((None, s, wh), lambda hh, bi, rb: (bi, 0, hh)),
                  pl.BlockSpec((wh // LANES,) + bias.shape[1:], lambda hh, bi, rb: (hh, 0, 0, 0, 0)),
                  pl.BlockSpec((None, tq, wh), blk)],
        out_specs=pl.BlockSpec((None, tq, wh), blk),
        compiler_params=_params("arbitrary", "arbitrary", "arbitrary"),
        name="nbr_attn",
    )(qc, kc, vc, bias, gate)


def _out_final_kernel(yc_ref, ym_ref, x_ref, w_ref, g_ref, o_ref):
    acc = x_ref[...]
    off = 0
    for y_ref in (yc_ref, ym_ref):
        width = y_ref.shape[1]
        acc = acc + jnp.dot(y_ref[...], w_ref[off:off + width, :], preferred_element_type=F32)
        off += width
    o_ref[...] = _rms_rows(acc, g_ref[...])


def _out_final(yc, ym, x2d, w_bf16, final_gain, tm):
    n_tok, d = x2d.shape
    row = lambda i: (i, 0)
    const = lambda i: (0, 0)
    return pl.pallas_call(
        _out_final_kernel,
        out_shape=jax.ShapeDtypeStruct((n_tok, d), F32),
        grid=(n_tok // tm,),
        in_specs=[pl.BlockSpec((tm, yc.shape[1]), row),
                  pl.BlockSpec((tm, ym.shape[1]), row),
                  pl.BlockSpec((tm, d), row),
                  pl.BlockSpec(w_bf16.shape, const),
                  pl.BlockSpec((1, d), const)],
        out_specs=pl.BlockSpec((tm, d), row),
        compiler_params=_params("arbitrary"),
        name="out_proj_final",
    )(yc, ym, x2d, w_bf16, final_gain)


def _rope_tables(seq):
    quarter = HEAD_DIM // 4
    freqs = jnp.power(ROPE_THETA, -jnp.arange(quarter, dtype=F32) / quarter)
    t = jnp.arange(seq)
    ang_r = (t // GRID_W).astype(F32)[:, None] * freqs
    ang_c = (t % GRID_W).astype(F32)[:, None] * freqs
    cos_h = jnp.concatenate([jnp.cos(ang_r), jnp.cos(ang_r), jnp.cos(ang_c), jnp.cos(ang_c)], axis=-1)
    sin_h = jnp.concatenate([-jnp.sin(ang_r), jnp.sin(ang_r), -jnp.sin(ang_c), jnp.sin(ang_c)], axis=-1)
    return jnp.tile(cos_h, (1, 2)), jnp.tile(sin_h, (1, 2))


def _t5_bucket(rel):
    nb = REL_BUCKETS // 2
    max_exact = nb // 2
    ret = jnp.where(rel > 0, nb, 0)
    n = jnp.abs(rel)
    nf = jnp.maximum(n, 1).astype(F32)
    large = max_exact + (jnp.log(nf / max_exact) / math.log(REL_MAX_DIST / max_exact)
                         * (nb - max_exact)).astype(jnp.int32)
    large = jnp.minimum(large, nb - 1)
    return ret + jnp.where(n < max_exact, n, large)


def _window_bias(rel_bias, blk):
    span = blk + 2 * WINDOW
    reach = span - WINDOW - 1
    rel = jnp.arange(-reach, reach + 1)
    per_rel = jnp.where((jnp.abs(rel) <= WINDOW)[:, None],
                        rel_bias.astype(F32)[_t5_bucket(rel)] * LOG2E, NEG).T
    bias = jnp.stack([per_rel[:, reach - WINDOW - q:reach - WINDOW - q + span] for q in range(blk)],
                     axis=-1)
    group = B_HEADS // B_KV
    bias = bias.reshape(B_KV, group, span, blk).transpose(0, 2, 1, 3)
    return bias.reshape(B_KV, span, group * blk)


def _nbr_bias(rpb):
    col = np.arange(GRID_W)
    cs = np.clip(col - NA_COLS // 2, 0, GRID_W - NA_COLS)
    colmask = (col[None, :] >= cs[:, None]) & (col[None, :] < cs[:, None] + NA_COLS)
    dc = np.clip(col[None, :] - col[:, None] + NA_COLS - 1, 0, 2 * NA_COLS - 2)
    onehot = (dc[None] == np.arange(2 * NA_COLS - 1)[:, None, None]) & colmask[None]
    t = jnp.einsum('hrc,cqk->hrqk', rpb.astype(F32) * LOG2E, jnp.asarray(onehot, F32),
                   precision=lax.Precision.HIGHEST)
    t = t + jnp.asarray(np.where(colmask, 0.0, NEG), F32)
    n_tiles = NA_ROWS - 1
    tiles = jnp.stack([jnp.concatenate([t[:, par:par + 2 * n_tiles:2], t[:, par + 1:par + 1 + 2 * n_tiles:2]],
                                       axis=-1) for par in range(2)], axis=1)
    tiles = tiles.reshape(C_HEADS // 2, 2, 2, n_tiles, GRID_W, LANES).transpose(0, 2, 3, 1, 4, 5)
    return tiles.reshape(C_HEADS // 2, 2, n_tiles, 2 * GRID_W, LANES)


def kernel(x, mem, norm_gain, mem_norm_gain, w_in_even, w_out_even, q_norm_a, k_norm_a, sink_b,
           rel_bias, w_in_odd, w_out_odd, rpb_c, w_mem_kv, final_norm_gain):
    b, s, d = x.shape
    mlen = mem.shape[1]
    assert s % GRID_W == 0 and s // GRID_W >= NA_ROWS and s % 512 == 0
    tm = 512
    blk = 128

    x2d = x.reshape(b * s, d)
    mem_tm = math.gcd(b * mlen, 512)
    mkv = _mem_kv(mem.reshape(b * mlen, d), mem_norm_gain.reshape(1, d), w_mem_kv.astype(BF16), mem_tm)
    mkv = mkv.reshape(w_mem_kv.shape[0], b, mlen, 2 * MEM_W)

    cos, sin = _rope_tables(s)
    gq = jnp.tile(q_norm_a[0].astype(F32), 2).reshape(1, LANES)
    gk = jnp.tile(k_norm_a[0].astype(F32), 2).reshape(1, LANES)
    qa, ka, va, qb, kb, vb, ga, gb, ym = _in_even(
        x2d, norm_gain[0].reshape(1, d), w_in_even[0].astype(BF16), gq, gk, cos, sin, mkv, s,
        math.gcd(s, 2 * tm))
    r3 = lambda a: a.reshape(b, s, a.shape[-1])
    ya = _global_attn(qa, r3(ka), va, r3(ga), tq=256, n_qb=4)
    group_b = B_HEADS // B_KV
    sink_cols = jnp.repeat(sink_b[0].astype(F32).reshape(B_KV, group_b) * LOG2E, blk,
                           axis=1).reshape(B_KV, 1, group_b * blk)
    yb = _window_attn(qb, r3(kb), vb, _window_bias(rel_bias, blk), sink_cols, r3(gb), blk,
                      blocks_per_step=4)

    x1, qc, kc, vc, gc, ym1 = _mid(ya.reshape(b * s, -1), yb.reshape(b * s, -1), ym, x2d,
                                   w_out_even[0].astype(BF16), norm_gain[1].reshape(1, d),
                                   w_in_odd[0].astype(BF16), mkv, s, tm)
    yc = _nbr_attn(r3(qc), r3(kc), r3(vc), _nbr_bias(rpb_c[0]), r3(gc), rows_per_step=8, head_splits=2)
    out = _out_final(yc.reshape(b * s, -1), ym1, x1, w_out_odd[0].astype(BF16),
                     final_norm_gain.reshape(1, d), 2 * tm)
    return out.reshape(b, s, d)
```

```python
import functools
import math

import jax
import jax.numpy as jnp
import numpy as np
from jax import lax
from jax.experimental import pallas as pl
from jax.experimental.pallas import tpu as pltpu

GRID_W = 64
HEAD_DIM = 64
A_HEADS = 8
A_KV = 2
B_HEADS = 8
B_KV = 2
WINDOW = 128
C_HEADS = 16
NA_ROWS = 8
NA_COLS = 16
M_HEADS = 4
M_HEAD_DIM = 128
MEM_W = M_HEADS * M_HEAD_DIM
REL_BUCKETS = 32
REL_MAX_DIST = 128
ROPE_THETA = 10000.0
EPS = 1e-6

LANES = 128
NEG = -1e30
LOG2E = math.log2(math.e)
V_ONES_ROWS = 16
A_CHUNK = 512
VMEM_LIMIT_BYTES = 56 * 1024 * 1024

F32 = jnp.float32
BF16 = jnp.bfloat16

_NT = (((1,), (1,)), ((), ()))


def _params(*sem):
    return pltpu.CompilerParams(dimension_semantics=sem, vmem_limit_bytes=VMEM_LIMIT_BYTES)


def _rms_rows(x, gain):
    ms = jnp.mean(x * x, axis=-1, keepdims=True)
    return x * lax.rsqrt(ms + EPS) * gain


def _lane_iota(shape):
    return lax.broadcasted_iota(jnp.int32, shape, len(shape) - 1)


def _silu(x):
    return x * (1.0 / (1.0 + jnp.exp(-x)))


def _half_ones():
    r = lax.broadcasted_iota(jnp.int32, (LANES, LANES), 0) // HEAD_DIM
    c = lax.broadcasted_iota(jnp.int32, (LANES, LANES), 1) // HEAD_DIM
    return jnp.where(r == c, 1.0, 0.0).astype(BF16)


def _head_norm_rope(t, gain, cos, sin_signed, ones_bd):
    ss = jnp.dot((t * t).astype(BF16), ones_bd, preferred_element_type=F32)
    tn = t * lax.rsqrt(ss * (1.0 / HEAD_DIM) + EPS) * gain
    lane = _lane_iota(tn.shape)
    quarter = HEAD_DIM // 4
    partner = jnp.where((lane % (2 * quarter)) < quarter,
                        pltpu.roll(tn, LANES - quarter, 1), pltpu.roll(tn, quarter, 1))
    return tn * cos + partner * sin_signed


def _store_padded_heads(q_tiles, out_ref, n_heads, n_kv, transposed=False):
    group = n_heads // n_kv
    for h in range(n_heads):
        t = q_tiles[h // 2]
        src_half = h % 2
        dst_half = (h // group) % 2
        if src_half != dst_half:
            t = pltpu.roll(t, HEAD_DIM, 1)
        lane = _lane_iota(t.shape)
        keep = (lane >= HEAD_DIM) if dst_half == 1 else (lane < HEAD_DIM)
        padded = jnp.where(keep, t, 0.0)
        if transposed:
            out_ref[h] = padded.T.astype(BF16)
        else:
            out_ref[:, h * LANES:(h + 1) * LANES] = padded.astype(BF16)


def _mem_kv_kernel(mem_ref, g_ref, w_ref, o_ref):
    h = _rms_rows(mem_ref[...], g_ref[...]).astype(BF16)
    o_ref[...] = jnp.dot(h, w_ref[...], preferred_element_type=F32).astype(BF16)


def _mem_kv(mem2d, gain, w_bf16, tm):
    depth, d, n = w_bf16.shape
    rows = mem2d.shape[0]
    return pl.pallas_call(
        _mem_kv_kernel,
        out_shape=jax.ShapeDtypeStruct((depth, rows, n), BF16),
        grid=(depth, rows // tm),
        in_specs=[pl.BlockSpec((tm, d), lambda l, i: (i, 0)),
                  pl.BlockSpec((1, d), lambda l, i: (0, 0)),
                  pl.BlockSpec((None, d, n), lambda l, i: (l, 0, 0))],
        out_specs=pl.BlockSpec((None, tm, n), lambda l, i: (l, i, 0)),
        compiler_params=_params("arbitrary", "arbitrary"),
        name="mem_kv_proj",
    )(mem2d, gain, w_bf16)


def _in_even_kernel(x_ref, g_ref, w_ref, gq_ref, gk_ref, cos_ref, sin_ref, mkv_ref,
                    qa_ref, ka_ref, va_ref, qb_ref, kb_ref, vb_ref, ga_ref, gb_ref, ym_ref,
                    qm_scr, gm_scr):
    h = _rms_rows(x_ref[...], g_ref[...]).astype(BF16)

    def seg(lo, hi):
        return jnp.dot(h, w_ref[:, lo:hi], preferred_element_type=F32)

    ones_bd = _half_ones()
    scale = HEAD_DIM ** -0.5
    qa_w = A_HEADS * HEAD_DIM
    kva_w = A_KV * HEAD_DIM
    qb_w = B_HEADS * HEAD_DIM
    kvb_w = B_KV * HEAD_DIM
    offs = np.cumsum([0, qa_w, kva_w, kva_w, qb_w, kvb_w, kvb_w, MEM_W, qa_w, qb_w, MEM_W])
    o_qa, o_ka, o_va, o_qb, o_kb, o_vb, o_qm, o_ga, o_gb, o_gm = (int(v) for v in offs[:-1])

    def do_qa():
        zq = seg(o_qa, o_qa + qa_w)
        tiles = [_head_norm_rope(zq[:, j * LANES:(j + 1) * LANES], gq_ref[...], cos_ref[...],
                                 sin_ref[...], ones_bd) * (scale * LOG2E) for j in range(qa_w // LANES)]
        _store_padded_heads(tiles, qa_ref, A_HEADS, A_KV, transposed=True)

    def do_ka():
        ka_ref[...] = _head_norm_rope(seg(o_ka, o_ka + kva_w), gk_ref[...], cos_ref[...],
                                      sin_ref[...], ones_bd).astype(BF16)

    def do_va():
        vt = seg(o_va, o_va + kva_w).T.astype(BF16)
        for j in range(va_ref.shape[0]):
            for g in range(A_KV):
                va_ref[j, g, :HEAD_DIM] = vt[g * HEAD_DIM:(g + 1) * HEAD_DIM,
                                             j * A_CHUNK:(j + 1) * A_CHUNK]
                va_ref[j, g, HEAD_DIM:] = jnp.ones((V_ONES_ROWS, A_CHUNK), BF16)

    def do_qb():
        zq = seg(o_qb, o_qb + qb_w) * (scale * LOG2E)
        _store_padded_heads([zq[:, j * LANES:(j + 1) * LANES] for j in range(qb_w // LANES)],
                            qb_ref, B_HEADS, B_KV, transposed=True)

    def do_kb():
        kb_ref[...] = seg(o_kb, o_kb + kvb_w).astype(BF16)

    def do_vb():
        vt = seg(o_vb, o_vb + kvb_w).T.astype(BF16)
        for j in range(vb_ref.shape[0]):
            for g in range(B_KV):
                vb_ref[j, g, :HEAD_DIM] = vt[g * HEAD_DIM:(g + 1) * HEAD_DIM, j * WINDOW:(j + 1) * WINDOW]
                vb_ref[j, g, HEAD_DIM:] = jnp.ones((V_ONES_ROWS, WINDOW), BF16)

    def do_ga():
        ga_ref[...] = _silu(seg(o_ga, o_ga + qa_w)).astype(BF16)

    def do_gb():
        gb_ref[...] = _silu(seg(o_gb, o_gb + qb_w)).astype(BF16)

    qm_scr[...] = (seg(o_qm, o_qm + MEM_W) * (M_HEAD_DIM ** -0.5 * LOG2E)).astype(BF16)
    gm_scr[...] = _silu(seg(o_gm, o_gm + MEM_W)).astype(BF16)
    _mem_attn_items(qm_scr, mkv_ref, gm_scr, ym_ref,
                    between=[do_qa, do_ga, do_qb, do_gb, do_va, do_ka, do_vb, do_kb])


def _in_even(x2d, gain, w_bf16, gq, gk, cos, sin, mkv, seq, tm):
    n_tok, d = x2d.shape
    n_in = w_bf16.shape[1]
    per_seq = seq // tm
    mlen = mkv.shape[2]
    widths = [A_HEADS * LANES, A_KV * HEAD_DIM, A_KV * HEAD_DIM,
              B_HEADS * LANES, B_KV * HEAD_DIM, B_KV * HEAD_DIM,
              A_HEADS * HEAD_DIM, B_HEADS * HEAD_DIM, MEM_W]
    row = lambda i: (i, 0)
    const = lambda i: (0, 0)
    out_shape = [jax.ShapeDtypeStruct((n_tok, w), BF16) for w in widths]
    out_specs = [pl.BlockSpec((tm, w), row) for w in widths]
    vrows = HEAD_DIM + V_ONES_ROWS
    out_shape[0] = jax.ShapeDtypeStruct((n_tok // tm, A_HEADS, LANES, tm), BF16)
    out_specs[0] = pl.BlockSpec((None, A_HEADS, LANES, tm), lambda i: (i, 0, 0, 0))
    out_shape[3] = jax.ShapeDtypeStruct((n_tok // tm, B_HEADS, LANES, tm), BF16)
    out_specs[3] = pl.BlockSpec((None, B_HEADS, LANES, tm), lambda i: (i, 0, 0, 0))
    out_shape[2] = jax.ShapeDtypeStruct((n_tok // A_CHUNK, A_KV, vrows, A_CHUNK), BF16)
    out_specs[2] = pl.BlockSpec((tm // A_CHUNK, A_KV, vrows, A_CHUNK), lambda i: (i, 0, 0, 0))
    out_shape[5] = jax.ShapeDtypeStruct((n_tok // WINDOW, B_KV, vrows, WINDOW), BF16)
    out_specs[5] = pl.BlockSpec((tm // WINDOW, B_KV, vrows, WINDOW), lambda i: (i, 0, 0, 0))
    return pl.pallas_call(
        _in_even_kernel,
        out_shape=out_shape,
        grid=(n_tok // tm,),
        in_specs=[pl.BlockSpec((tm, d), row),
                  pl.BlockSpec((1, d), const),
                  pl.BlockSpec((d, n_in), const, pipeline_mode=pl.Buffered(1)),
                  pl.BlockSpec((1, LANES), const),
                  pl.BlockSpec((1, LANES), const),
                  pl.BlockSpec((tm, LANES), lambda i: (i % per_seq, 0)),
                  pl.BlockSpec((tm, LANES), lambda i: (i % per_seq, 0)),
                  pl.BlockSpec((None, None, mlen, 2 * MEM_W), lambda i: (0, i // per_seq, 0, 0))],
        out_specs=out_specs,
        scratch_shapes=[pltpu.VMEM((tm, MEM_W), BF16), pltpu.VMEM((tm, MEM_W), BF16)],
        compiler_params=_params("arbitrary"),
        name="in_proj_even",
    )(x2d, gain, w_bf16, gq, gk, cos, sin, mkv)


def _mid_kernel(ya_ref, yb_ref, ym_ref, x_ref, wo_ref, g_ref, wi_ref, mkv_ref,
                x1_ref, qc_ref, kc_ref, vc_ref, gc_ref, ym1_ref, qm_scr, gm_scr):
    acc = x_ref[...]
    off = 0
    for y_ref in (ya_ref, yb_ref, ym_ref):
        width = y_ref.shape[1]
        acc = acc + jnp.dot(y_ref[...], wo_ref[off:off + width, :], preferred_element_type=F32)
        off += width
    x1_ref[...] = acc
    h = _rms_rows(acc, g_ref[...]).astype(BF16)

    def seg(lo, hi):
        return jnp.dot(h, wi_ref[:, lo:hi], preferred_element_type=F32)

    cw = C_HEADS * HEAD_DIM
    half = cw // 2
    o_qc, o_kc, o_vc, o_qm, o_gc, o_gm = 0, cw, 2 * cw, 3 * cw, 3 * cw + MEM_W, 4 * cw + MEM_W

    def thunk(ref, base, part, fn):
        def run():
            lo = part * half
            ref[:, lo:lo + half] = fn(seg(base + lo, base + lo + half)).astype(BF16)
        return run

    plain = lambda z: z
    to_q = lambda z: z * (HEAD_DIM ** -0.5 * LOG2E)
    segments = [thunk(ref, base, part, fn)
                for ref, base, fn in ((gc_ref, o_gc, _silu), (qc_ref, o_qc, to_q),
                                      (kc_ref, o_kc, plain), (vc_ref, o_vc, plain))
                for part in range(2)]
    qm_scr[...] = (seg(o_qm, o_qm + MEM_W) * (M_HEAD_DIM ** -0.5 * LOG2E)).astype(BF16)
    gm_scr[...] = _silu(seg(o_gm, o_gm + MEM_W)).astype(BF16)
    _mem_attn_items(qm_scr, mkv_ref, gm_scr, ym1_ref, between=segments)


def _mid(ya, yb, ym, x2d, wo_bf16, gain, wi_bf16, mkv, seq, tm):
    n_tok, d = x2d.shape
    per_seq = seq // tm
    mlen = mkv.shape[2]
    cw = C_HEADS * HEAD_DIM
    row = lambda i: (i, 0)
    const = lambda i: (0, 0)
    once = pl.Buffered(1)
    out_widths = [cw, cw, cw, cw, MEM_W]
    return pl.pallas_call(
        _mid_kernel,
        out_shape=[jax.ShapeDtypeStruct((n_tok, d), F32)]
                  + [jax.ShapeDtypeStruct((n_tok, w), BF16) for w in out_widths],
        grid=(n_tok // tm,),
        in_specs=[pl.BlockSpec((tm, ya.shape[1]), row),
                  pl.BlockSpec((tm, yb.shape[1]), row),
                  pl.BlockSpec((tm, ym.shape[1]), row),
                  pl.BlockSpec((tm, d), row),
                  pl.BlockSpec(wo_bf16.shape, const, pipeline_mode=once),
                  pl.BlockSpec((1, d), const),
                  pl.BlockSpec(wi_bf16.shape, const, pipeline_mode=once),
                  pl.BlockSpec((None, None, mlen, 2 * MEM_W), lambda i: (1, i // per_seq, 0, 0))],
        out_specs=[pl.BlockSpec((tm, d), row)] + [pl.BlockSpec((tm, w), row) for w in out_widths],
        scratch_shapes=[pltpu.VMEM((tm, MEM_W), BF16), pltpu.VMEM((tm, MEM_W), BF16)],
        compiler_params=_params("arbitrary"),
        name="out_proj_in_proj_odd",
    )(ya, yb, ym, x2d, wo_bf16, gain, wi_bf16, mkv)


def _global_attn_kernel(q_ref, k_ref, vt_ref, gate_ref, o_ref, m_scr, acc_scr, s_scr, mc_scr):
    n_qb = m_scr.shape[0]
    slab_w = q_ref.shape[3]
    tq = q_ref.shape[0] * slab_w // n_qb
    n_chunks = vt_ref.shape[0]
    tk = vt_ref.shape[3]
    group = A_HEADS // A_KV

    def q_block(b, h):
        lo = b * tq
        return q_ref[lo // slab_w, h][:, lo % slab_w:lo % slab_w + tq]

    qts = [[jnp.concatenate([q_block(b, g * group + i) for i in range(group)], axis=1)
            for g in range(A_KV)] for b in range(n_qb)]
    m_scr[...] = jnp.full(m_scr.shape, NEG, F32)
    acc_scr[...] = jnp.zeros(acc_scr.shape, F32)

    parts = 2
    tp = tk // parts
    pieces = [(p, g) for p in range(parts) for g in range(A_KV)]

    def scores_piece(b, c, slot, p, g):
        start = pl.multiple_of(c * tk + p * tp, tp)
        st = jnp.dot(k_ref[pl.ds(start, tp), :], qts[b][g], preferred_element_type=F32)
        s_scr[slot, g, p * tp:(p + 1) * tp] = st
        mx = jnp.max(st, axis=0, keepdims=True)
        mc_scr[slot, g] = mx if p == 0 else jnp.maximum(mc_scr[slot, g], mx)

    def accumulate_piece(b, c, slot, p, g):
        m_old = m_scr[b, g]
        m_new = jnp.maximum(m_old, mc_scr[slot, g])
        pt = jnp.exp2(s_scr[slot, g, p * tp:(p + 1) * tp] - m_new).astype(BF16)
        pv = jnp.dot(vt_ref[c, g][:, p * tp:(p + 1) * tp], pt, preferred_element_type=F32)
        if p == 0:
            acc_scr[b, g] = jnp.exp2(m_old - m_new) * acc_scr[b, g] + pv
        else:
            acc_scr[b, g] = acc_scr[b, g] + pv
        if p == parts - 1:
            m_scr[b, g] = m_new

    def scores_and_accumulate(score_of, accumulate_of):
        for p, g in pieces:
            if score_of is not None:
                scores_piece(*score_of, p, g)
            if accumulate_of is not None:
                accumulate_piece(*accumulate_of, p, g)

    def finalize(b):
        rows = slice(b * tq, (b + 1) * tq)
        ot = [acc_scr[b, g, :HEAD_DIM] * (1.0 / acc_scr[b, g, HEAD_DIM:HEAD_DIM + 1])
              for g in range(A_KV)]
        for j in range(A_HEADS // 2):
            g, i0 = (2 * j) // group, (2 * j) % group
            tile_t = jnp.concatenate([ot[g][:, i0 * tq:(i0 + 1) * tq],
                                      ot[g][:, (i0 + 1) * tq:(i0 + 2) * tq]], axis=0)
            gate = gate_ref[rows, j * LANES:(j + 1) * LANES].astype(F32)
            o_ref[rows, j * LANES:(j + 1) * LANES] = (tile_t.T * gate).astype(BF16)

    scores_and_accumulate((0, 0, 0), None)
    for b in range(n_qb):
        def body(c2, carry, b=b):
            c = 2 * c2
            scores_and_accumulate((b, c + 1, 1), (b, c, 0))
            scores_and_accumulate((b, c + 2, 0), (b, c + 1, 1))
            return carry

        lax.fori_loop(0, n_chunks // 2 - 1, body, 0)
        scores_and_accumulate((b, n_chunks - 1, 1), (b, n_chunks - 2, 0))
        following = (b + 1, 0, 0) if b + 1 < n_qb else None
        scores_and_accumulate(following, (b, n_chunks - 1, 1))
        finalize(b)


def _global_attn(qat, ka, vat, gate, tq, n_qb):
    b, s, _ = ka.shape
    _, n_kv, vrows, tk = vat.shape
    tm = qat.shape[-1]
    n_chunks = s // tk
    step_q = n_qb * tq
    assert n_chunks % 2 == 0 and (tm % step_q == 0 or step_q % tm == 0)
    rows = (A_HEADS // A_KV) * tq
    vat = vat.reshape(b, n_chunks, n_kv, vrows, tk)
    qat = qat.reshape(b, s // tm, A_HEADS, LANES, tm)
    if step_q >= tm:
        q_spec = pl.BlockSpec((None, step_q // tm, A_HEADS, LANES, tm), lambda bi, i: (bi, i, 0, 0, 0))
    else:
        per_slab = tm // step_q
        q_spec = pl.BlockSpec((None, 1, A_HEADS, LANES, step_q),
                              lambda bi, i: (bi, i // per_slab, 0, 0, i % per_slab))
    return pl.pallas_call(
        _global_attn_kernel,
        out_shape=jax.ShapeDtypeStruct((b, s, A_HEADS * HEAD_DIM), BF16),
        grid=(b, s // step_q),
        in_specs=[q_spec,
                  pl.BlockSpec((None, s, LANES), lambda bi, i: (bi, 0, 0)),
                  pl.BlockSpec((None, n_chunks, n_kv, vrows, tk), lambda bi, i: (bi, 0, 0, 0, 0)),
                  pl.BlockSpec((None, step_q, A_HEADS * HEAD_DIM), lambda bi, i: (bi, i, 0))],
        out_specs=pl.BlockSpec((None, step_q, A_HEADS * HEAD_DIM), lambda bi, i: (bi, i, 0)),
        scratch_shapes=[pltpu.VMEM((n_qb, A_KV, 1, rows), F32),
                        pltpu.VMEM((n_qb, A_KV, vrows, rows), F32),
                        pltpu.VMEM((2, A_KV, tk, rows), F32), pltpu.VMEM((2, A_KV, 1, rows), F32)],
        compiler_params=_params("arbitrary", "arbitrary"),
        name="global_attn",
    )(qat, ka, vat, gate)


def _window_attn_kernel(q_ref, k_ref, vt_ref, bias_ref, sink_ref, gate_ref, o_ref, *, blk):
    blocks_per_step = q_ref.shape[2] // blk
    nb = vt_ref.shape[0]
    step = pl.program_id(1)
    group = B_HEADS // B_KV

    def neighbours(t):
        i = step * blocks_per_step + t
        return jnp.maximum(i - 1, 0), i, jnp.minimum(i + 1, nb - 1)

    def scores(t, g):
        i = step * blocks_per_step + t
        qt = jnp.concatenate([q_ref[g * group + j][:, t * blk:(t + 1) * blk]
                              for j in range(group)], axis=1)
        k = jnp.concatenate([k_ref[pl.ds(pl.multiple_of(n * blk, blk), blk), :]
                             for n in neighbours(t)], axis=0)
        st = jnp.dot(k, qt, preferred_element_type=F32) + bias_ref[g]
        if t == 0:
            st = jnp.concatenate([st[:blk] + jnp.where(i == 0, NEG, 0.0), st[blk:]], axis=0)
        if t == blocks_per_step - 1:
            st = jnp.concatenate([st[:2 * blk], st[2 * blk:] + jnp.where(i == nb - 1, NEG, 0.0)], axis=0)
        m = jnp.maximum(jnp.max(st, axis=0, keepdims=True), sink_ref[g])
        return st, m

    def probs(st, m):
        return jnp.exp2(st - m).astype(BF16), m

    def output(t, g, pt, m):
        vt = jnp.concatenate([vt_ref[n, g] for n in neighbours(t)], axis=1)
        ot = jnp.dot(vt, pt, preferred_element_type=F32)
        denom = ot[HEAD_DIM:HEAD_DIM + 1] + jnp.exp2(sink_ref[g] - m)
        return ot[:HEAD_DIM] * (1.0 / denom)

    def store(t, ot):
        rows = slice(t * blk, (t + 1) * blk)
        for j in range(B_HEADS // 2):
            g, i0 = (2 * j) // group, (2 * j) % group
            tile_t = jnp.concatenate([ot[g][:, i0 * blk:(i0 + 1) * blk],
                                      ot[g][:, (i0 + 1) * blk:(i0 + 2) * blk]], axis=0)
            gate = gate_ref[rows, j * LANES:(j + 1) * LANES].astype(F32)
            o_ref[rows, j * LANES:(j + 1) * LANES] = (tile_t.T * gate).astype(BF16)

    items = [(t, g) for t in range(blocks_per_step) for g in range(B_KV)]
    sm, pb, outs = {}, {}, {}
    for n in range(len(items) + 2):
        if n < len(items):
            sm[n] = scores(*items[n])
        if 1 <= n <= len(items):
            pb[n - 1] = probs(*sm.pop(n - 1))
        if n >= 2:
            t, g = items[n - 2]
            outs[g] = output(t, g, *pb.pop(n - 2))
            if g == B_KV - 1:
                store(t, outs)


def _window_attn(qbt, kb, vbt, bias_t, sink_cols, gate, blk, blocks_per_step):
    b, s, _ = kb.shape
    nb = s // blk
    rows = (B_HEADS // B_KV) * blk
    tq = blk * blocks_per_step
    tm = qbt.shape[-1]
    assert tm % tq == 0
    per_slab = tm // tq
    vrows = vbt.shape[2]
    vbt = vbt.reshape(b, nb, B_KV, vrows, blk)
    qbt = qbt.reshape(b, s // tm, B_HEADS, LANES, tm)
    return pl.pallas_call(
        functools.partial(_window_attn_kernel, blk=blk),
        out_shape=jax.ShapeDtypeStruct((b, s, B_HEADS * HEAD_DIM), BF16),
        grid=(b, s // tq),
        in_specs=[pl.BlockSpec((None, None, B_HEADS, LANES, tq),
                               lambda bi, i: (bi, i // per_slab, 0, 0, i % per_slab)),
                  pl.BlockSpec((None, s, LANES), lambda bi, i: (bi, 0, 0)),
                  pl.BlockSpec((None, nb, B_KV, vrows, blk), lambda bi, i: (bi, 0, 0, 0, 0)),
                  pl.BlockSpec((B_KV, 3 * blk, rows), lambda bi, i: (0, 0, 0)),
                  pl.BlockSpec((B_KV, 1, rows), lambda bi, i: (0, 0, 0)),
                  pl.BlockSpec((None, tq, B_HEADS * HEAD_DIM), lambda bi, i: (bi, i, 0))],
        out_specs=pl.BlockSpec((None, tq, B_HEADS * HEAD_DIM), lambda bi, i: (bi, i, 0)),
        compiler_params=_params("arbitrary", "arbitrary"),
        name="window_attn",
    )(qbt, kb, vbt, bias_t, sink_cols, gate)


def _mem_attn_items(q_ref, kv_ref, gate_ref, o_ref, between=(), sub=256):
    between = list(between)
    mlen = kv_ref.shape[0]
    ones = jnp.ones((mlen, M_HEAD_DIM), BF16)

    def scores(t, h):
        lo, hi = h * M_HEAD_DIM, (h + 1) * M_HEAD_DIM
        s = lax.dot_general(q_ref[t * sub:(t + 1) * sub, lo:hi], kv_ref[:, lo:hi], _NT,
                            preferred_element_type=F32)
        return s, jnp.max(s, axis=-1, keepdims=True)

    def probs(s, m):
        return jnp.exp2(s - m).astype(BF16)

    def output(t, h, p):
        lo, hi = h * M_HEAD_DIM, (h + 1) * M_HEAD_DIM
        v = jnp.concatenate([kv_ref[:, MEM_W + lo:MEM_W + hi], ones], axis=1)
        o = jnp.dot(p, v, preferred_element_type=F32)
        o = o[:, :M_HEAD_DIM] * (1.0 / o[:, M_HEAD_DIM:])
        rows = slice(t * sub, (t + 1) * sub)
        o_ref[rows, lo:hi] = (o * gate_ref[rows, lo:hi].astype(F32)).astype(BF16)

    items = [(t, h) for t in range(q_ref.shape[0] // sub) for h in range(M_HEADS)]
    sm, pb = {}, {}
    for n in range(len(items) + 2):
        if n < len(items):
            sm[n] = scores(*items[n])
        if 1 <= n <= len(items):
            pb[n - 1] = probs(*sm.pop(n - 1))
        if n >= 2:
            output(*items[n - 2], pb.pop(n - 2))
        if between:
            between.pop(0)()
    for run in between:
        run()


def _nbr_attn_kernel(q_ref, k_ref, v_ref, bias_ref, gate_ref, o_ref, *, grid_rows, rows_per_step):
    rb = pl.program_id(2)
    nkeys = NA_ROWS * GRID_W
    n_pairs = q_ref.shape[1] // LANES
    ones = jnp.ones((nkeys, LANES), BF16)
    starts, first_offsets = [], []
    for t in range(rows_per_step):
        r = rb * rows_per_step + t
        rs = jnp.clip(r - NA_ROWS // 2, 0, grid_rows - NA_ROWS)
        starts.append(pl.multiple_of(rs * GRID_W, GRID_W))
        first_offsets.append(rs - r + NA_ROWS - 1)
    items = [(t, j) for t in range(rows_per_step) for j in range(n_pairs)]

    def scores(t, j):
        lo, hi = j * LANES, (j + 1) * LANES
        qt = q_ref[t * GRID_W:(t + 1) * GRID_W, lo:hi]
        lane = _lane_iota(qt.shape)
        zero = jnp.zeros_like(qt)
        q = jnp.concatenate([jnp.where(lane < HEAD_DIM, qt, zero),
                             jnp.where(lane >= HEAD_DIM, qt, zero)], axis=0)
        k = k_ref[pl.ds(starts[t], nkeys), lo:hi]
        par, m0 = first_offsets[t] % 2, first_offsets[t] // 2
        bias = jnp.concatenate([bias_ref[j, par, m0 + i] for i in range(NA_ROWS // 2)], axis=1)
        s = lax.dot_general(q, k, _NT, preferred_element_type=F32) + bias
        return s, jnp.max(s, axis=-1, keepdims=True)

    def probs(s, m):
        return jnp.exp2(s - m).astype(BF16)

    def output(t, j, p):
        lo, hi = j * LANES, (j + 1) * LANES
        v = jnp.concatenate([v_ref[pl.ds(starts[t], nkeys), lo:hi], ones], axis=1)
        o = jnp.dot(p, v, preferred_element_type=F32)
        o = o[:, :LANES] * (1.0 / o[:, LANES:])
        lane_o = _lane_iota((GRID_W, LANES))
        tile = jnp.where(lane_o < HEAD_DIM, o[:GRID_W, :], o[GRID_W:, :])
        rows = slice(t * GRID_W, (t + 1) * GRID_W)
        o_ref[rows, lo:hi] = (tile * gate_ref[rows, lo:hi].astype(F32)).astype(BF16)

    sm = {}
    pb = {}
    for n in range(len(items) + 2):
        if n < len(items):
            sm[n] = scores(*items[n])
        if 1 <= n <= len(items):
            pb[n - 1] = probs(*sm.pop(n - 1))
        if n >= 2:
            output(*items[n - 2], pb.pop(n - 2))


def _nbr_attn(qc, kc, vc, bias, gate, rows_per_step, head_splits):
    b, s, w = qc.shape
    grid_rows = s // GRID_W
    nkeys = NA_ROWS * GRID_W
    wh = w // head_splits
    tq = rows_per_step * GRID_W
    blk = lambda hh, bi, rb: (bi, rb, hh)
    return pl.pallas_call(
        functools.partial(_nbr_attn_kernel, grid_rows=grid_rows, rows_per_step=rows_per_step),
        out_shape=jax.ShapeDtypeStruct((b, s, w), BF16),
        grid=(head_splits, b, grid_rows // rows_per_step),
        in_specs=[pl.BlockSpec((None, tq, wh), blk),
                  pl.BlockSpec((None, s, wh), lambda hh, bi, rb: (bi, 0, hh)),
                  pl.BlockSpec((None, s, wh), lambda hh, bi, rb: (bi, 0, hh)),
                  pl.BlockSpec((wh // LANES,) + bias.shape[1:], lambda hh, bi, rb: (hh, 0, 0, 0, 0)),
                  pl.BlockSpec((None, tq, wh), blk)],
        out_specs=pl.BlockSpec((None, tq, wh), blk),
        compiler_params=_params("arbitrary", "arbitrary", "arbitrary"),
        name="nbr_attn",
    )(qc, kc, vc, bias, gate)


def _out_final_kernel(yc_ref, ym_ref, x_ref, w_ref, g_ref, o_ref):
    acc = x_ref[...]
    off = 0
    for y_ref in (yc_ref, ym_ref):
        width = y_ref.shape[1]
        acc = acc + jnp.dot(y_ref[...], w_ref[off:off + width, :], preferred_element_type=F32)
        off += width
    o_ref[...] = _rms_rows(acc, g_ref[...])


def _out_final(yc, ym, x2d, w_bf16, final_gain, tm):
    n_tok, d = x2d.shape
    row = lambda i: (i, 0)
    const = lambda i: (0, 0)
    return pl.pallas_call(
        _out_final_kernel,
        out_shape=jax.ShapeDtypeStruct((n_tok, d), F32),
        grid=(n_tok // tm,),
        in_specs=[pl.BlockSpec((tm, yc.shape[1]), row),
                  pl.BlockSpec((tm, ym.shape[1]), row),
                  pl.BlockSpec((tm, d), row),
                  pl.BlockSpec(w_bf16.shape, const),
                  pl.BlockSpec((1, d), const)],
        out_specs=pl.BlockSpec((tm, d), row),
        compiler_params=_params("arbitrary"),
        name="out_proj_final",
    )(yc, ym, x2d, w_bf16, final_gain)


def _rope_tables(seq):
    quarter = HEAD_DIM // 4
    freqs = jnp.power(ROPE_THETA, -jnp.arange(quarter, dtype=F32) / quarter)
    t = jnp.arange(seq)
    ang_r = (t // GRID_W).astype(F32)[:, None] * freqs
    ang_c = (t % GRID_W).astype(F32)[:, None] * freqs
    cos_h = jnp.concatenate([jnp.cos(ang_r), jnp.cos(ang_r), jnp.cos(ang_c), jnp.cos(ang_c)], axis=-1)
    sin_h = jnp.concatenate([-jnp.sin(ang_r), jnp.sin(ang_r), -jnp.sin(ang_c), jnp.sin(ang_c)], axis=-1)
    return jnp.tile(cos_h, (1, 2)), jnp.tile(sin_h, (1, 2))


def _t5_bucket(rel):
    nb = REL_BUCKETS // 2
    max_exact = nb // 2
    ret = jnp.where(rel > 0, nb, 0)
    n = jnp.abs(rel)
    nf = jnp.maximum(n, 1).astype(F32)
    large = max_exact + (jnp.log(nf / max_exact) / math.log(REL_MAX_DIST / max_exact)
                         * (nb - max_exact)).astype(jnp.int32)
    large = jnp.minimum(large, nb - 1)
    return ret + jnp.where(n < max_exact, n, large)


def _window_bias(rel_bias, blk):
    span = blk + 2 * WINDOW
    reach = span - WINDOW - 1
    rel = jnp.arange(-reach, reach + 1)
    per_rel = jnp.where((jnp.abs(rel) <= WINDOW)[:, None],
                        rel_bias.astype(F32)[_t5_bucket(rel)] * LOG2E, NEG).T
    bias = jnp.stack([per_rel[:, reach - WINDOW - q:reach - WINDOW - q + span] for q in range(blk)],
                     axis=-1)
    group = B_HEADS // B_KV
    bias = bias.reshape(B_KV, group, span, blk).transpose(0, 2, 1, 3)
    return bias.reshape(B_KV, span, group * blk)


def _nbr_bias(rpb):
    col = np.arange(GRID_W)
    cs = np.clip(col - NA_COLS // 2, 0, GRID_W - NA_COLS)
    colmask = (col[None, :] >= cs[:, None]) & (col[None, :] < cs[:, None] + NA_COLS)
    dc = np.clip(col[None, :] - col[:, None] + NA_COLS - 1, 0, 2 * NA_COLS - 2)
    onehot = (dc[None] == np.arange(2 * NA_COLS - 1)[:, None, None]) & colmask[None]
    t = jnp.einsum('hrc,cqk->hrqk', rpb.astype(F32) * LOG2E, jnp.asarray(onehot, F32),
                   precision=lax.Precision.HIGHEST)
    t = t + jnp.asarray(np.where(colmask, 0.0, NEG), F32)
    n_tiles = NA_ROWS - 1
    tiles = jnp.stack([jnp.concatenate([t[:, par:par + 2 * n_tiles:2], t[:, par + 1:par + 1 + 2 * n_tiles:2]],
                                       axis=-1) for par in range(2)], axis=1)
    tiles = tiles.reshape(C_HEADS // 2, 2, 2, n_tiles, GRID_W, LANES).transpose(0, 2, 3, 1, 4, 5)
    return tiles.reshape(C_HEADS // 2, 2, n_tiles, 2 * GRID_W, LANES)


def kernel(x, mem, norm_gain, mem_norm_gain, w_in_even, w_out_even, q_norm_a, k_norm_a, sink_b,
           rel_bias, w_in_odd, w_out_odd, rpb_c, w_mem_kv, final_norm_gain):
    b, s, d = x.shape
    mlen = mem.shape[1]
    assert s % GRID_W == 0 and s // GRID_W >= NA_ROWS and s % 512 == 0
    tm = 512
    blk = 128

    x2d = x.reshape(b * s, d)
    mem_tm = math.gcd(b * mlen, 512)
    mkv = _mem_kv(mem.reshape(b * mlen, d), mem_norm_gain.reshape(1, d), w_mem_kv.astype(BF16), mem_tm)
    mkv = mkv.reshape(w_mem_kv.shape[0], b, mlen, 2 * MEM_W)

    cos, sin = _rope_tables(s)
    gq = jnp.tile(q_norm_a[0].astype(F32), 2).reshape(1, LANES)
    gk = jnp.tile(k_norm_a[0].astype(F32), 2).reshape(1, LANES)
    qa, ka, va, qb, kb, vb, ga, gb, ym = _in_even(
        x2d, norm_gain[0].reshape(1, d), w_in_even[0].astype(BF16), gq, gk, cos, sin, mkv, s,
        math.gcd(s, 2 * tm))
    r3 = lambda a: a.reshape(b, s, a.shape[-1])
    ya = _global_attn(qa, r3(ka), va, r3(ga), tq=256, n_qb=4)
    group_b = B_HEADS // B_KV
    sink_cols = jnp.repeat(sink_b[0].astype(F32).reshape(B_KV, group_b) * LOG2E, blk,
                           axis=1).reshape(B_KV, 1, group_b * blk)
    yb = _window_attn(qb, r3(kb), vb, _window_bias(rel_bias, blk), sink_cols, r3(gb), blk,
                      blocks_per_step=8)

    x1, qc, kc, vc, gc, ym1 = _mid(ya.reshape(b * s, -1), yb.reshape(b * s, -1), ym, x2d,
                                   w_out_even[0].astype(BF16), norm_gain[1].reshape(1, d),
                                   w_in_odd[0].astype(BF16), mkv, s, tm)
    yc = _nbr_attn(r3(qc), r3(kc), r3(vc), _nbr_bias(rpb_c[0]), r3(gc), rows_per_step=16, head_splits=2)
    out = _out_final(yc.reshape(b * s, -1), ym1, x1, w_out_odd[0].astype(BF16),
                     final_norm_gain.reshape(1, d), 2 * tm)
    return out.reshape(b, s, d)
```

```python
import functools
import math

import jax
import jax.numpy as jnp
import numpy as np
from jax import lax
from jax.experimental import pallas as pl
from jax.experimental.pallas import tpu as pltpu

GRID_W = 64
HEAD_DIM = 64
A_HEADS = 8
A_KV = 2
B_HEADS = 8
B_KV = 2
WINDOW = 128
C_HEADS = 16
NA_ROWS = 8
NA_COLS = 16
M_HEADS = 4
M_HEAD_DIM = 128
MEM_W = M_HEADS * M_HEAD_DIM
REL_BUCKETS = 32
REL_MAX_DIST = 128
ROPE_THETA = 10000.0
EPS = 1e-6

LANES = 128
NEG = -1e30
LOG2E = math.log2(math.e)
V_ONES_ROWS = 16
A_CHUNK = 512
VMEM_LIMIT_BYTES = 56 * 1024 * 1024

F32 = jnp.float32
BF16 = jnp.bfloat16

_NT = (((1,), (1,)), ((), ()))


def _params(*sem):
    return pltpu.CompilerParams(dimension_semantics=sem, vmem_limit_bytes=VMEM_LIMIT_BYTES)


def _rms_rows(x, gain):
    ms = jnp.mean(x * x, axis=-1, keepdims=True)
    return x * lax.rsqrt(ms + EPS) * gain


def _lane_iota(shape):
    return lax.broadcasted_iota(jnp.int32, shape, len(shape) - 1)


def _silu(x):
    return x * (1.0 / (1.0 + jnp.exp(-x)))


def _half_ones():
    r = lax.broadcasted_iota(jnp.int32, (LANES, LANES), 0) // HEAD_DIM
    c = lax.broadcasted_iota(jnp.int32, (LANES, LANES), 1) // HEAD_DIM
    return jnp.where(r == c, 1.0, 0.0).astype(BF16)


def _head_norm_rope(t, gain, cos, sin_signed, ones_bd):
    ss = jnp.dot((t * t).astype(BF16), ones_bd, preferred_element_type=F32)
    tn = t * lax.rsqrt(ss * (1.0 / HEAD_DIM) + EPS) * gain
    lane = _lane_iota(tn.shape)
    quarter = HEAD_DIM // 4
    partner = jnp.where((lane % (2 * quarter)) < quarter,
                        pltpu.roll(tn, LANES - quarter, 1), pltpu.roll(tn, quarter, 1))
    return tn * cos + partner * sin_signed


def _store_padded_heads(q_tiles, out_ref, n_heads, n_kv, transposed=False):
    group = n_heads // n_kv
    for h in range(n_heads):
        t = q_tiles[h // 2]
        src_half = h % 2
        dst_half = (h // group) % 2
        if src_half != dst_half:
            t = pltpu.roll(t, HEAD_DIM, 1)
        lane = _lane_iota(t.shape)
        keep = (lane >= HEAD_DIM) if dst_half == 1 else (lane < HEAD_DIM)
        padded = jnp.where(keep, t, 0.0)
        if transposed:
            out_ref[h] = padded.T.astype(BF16)
        else:
            out_ref[:, h * LANES:(h + 1) * LANES] = padded.astype(BF16)


def _mem_kv_kernel(mem_ref, g_ref, w_ref, o_ref):
    h = _rms_rows(mem_ref[...], g_ref[...]).astype(BF16)
    o_ref[...] = jnp.dot(h, w_ref[...], preferred_element_type=F32).astype(BF16)


def _mem_kv(mem2d, gain, w_bf16, tm):
    depth, d, n = w_bf16.shape
    rows = mem2d.shape[0]
    return pl.pallas_call(
        _mem_kv_kernel,
        out_shape=jax.ShapeDtypeStruct((depth, rows, n), BF16),
        grid=(depth, rows // tm),
        in_specs=[pl.BlockSpec((tm, d), lambda l, i: (i, 0)),
                  pl.BlockSpec((1, d), lambda l, i: (0, 0)),
                  pl.BlockSpec((None, d, n), lambda l, i: (l, 0, 0))],
        out_specs=pl.BlockSpec((None, tm, n), lambda l, i: (l, i, 0)),
        compiler_params=_params("arbitrary", "arbitrary"),
        name="mem_kv_proj",
    )(mem2d, gain, w_bf16)


def _in_even_kernel(x_ref, g_ref, w_ref, gq_ref, gk_ref, cos_ref, sin_ref, mkv_ref,
                    qa_ref, ka_ref, va_ref, qb_ref, kb_ref, vb_ref, ga_ref, gb_ref, ym_ref,
                    qm_scr, gm_scr):
    h = _rms_rows(x_ref[...], g_ref[...]).astype(BF16)

    def seg(lo, hi):
        return jnp.dot(h, w_ref[:, lo:hi], preferred_element_type=F32)

    ones_bd = _half_ones()
    scale = HEAD_DIM ** -0.5
    qa_w = A_HEADS * HEAD_DIM
    kva_w = A_KV * HEAD_DIM
    qb_w = B_HEADS * HEAD_DIM
    kvb_w = B_KV * HEAD_DIM
    offs = np.cumsum([0, qa_w, kva_w, kva_w, qb_w, kvb_w, kvb_w, MEM_W, qa_w, qb_w, MEM_W])
    o_qa, o_ka, o_va, o_qb, o_kb, o_vb, o_qm, o_ga, o_gb, o_gm = (int(v) for v in offs[:-1])

    def do_qa():
        zq = seg(o_qa, o_qa + qa_w)
        tiles = [_head_norm_rope(zq[:, j * LANES:(j + 1) * LANES], gq_ref[...], cos_ref[...],
                                 sin_ref[...], ones_bd) * (scale * LOG2E) for j in range(qa_w // LANES)]
        _store_padded_heads(tiles, qa_ref, A_HEADS, A_KV, transposed=True)

    def do_ka():
        ka_ref[...] = _head_norm_rope(seg(o_ka, o_ka + kva_w), gk_ref[...], cos_ref[...],
                                      sin_ref[...], ones_bd).astype(BF16)

    def do_va():
        vt = seg(o_va, o_va + kva_w).T.astype(BF16)
        for j in range(va_ref.shape[0]):
            for g in range(A_KV):
                va_ref[j, g, :HEAD_DIM] = vt[g * HEAD_DIM:(g + 1) * HEAD_DIM,
                                             j * A_CHUNK:(j + 1) * A_CHUNK]
                va_ref[j, g, HEAD_DIM:] = jnp.ones((V_ONES_ROWS, A_CHUNK), BF16)

    def do_qb():
        zq = seg(o_qb, o_qb + qb_w) * (scale * LOG2E)
        _store_padded_heads([zq[:, j * LANES:(j + 1) * LANES] for j in range(qb_w // LANES)],
                            qb_ref, B_HEADS, B_KV, transposed=True)

    def do_kb():
        kb_ref[...] = seg(o_kb, o_kb + kvb_w).astype(BF16)

    def do_vb():
        vt = seg(o_vb, o_vb + kvb_w).T.astype(BF16)
        for j in range(vb_ref.shape[0]):
            for g in range(B_KV):
                vb_ref[j, g, :HEAD_DIM] = vt[g * HEAD_DIM:(g + 1) * HEAD_DIM, j * WINDOW:(j + 1) * WINDOW]
                vb_ref[j, g, HEAD_DIM:] = jnp.ones((V_ONES_ROWS, WINDOW), BF16)

    def do_ga():
        ga_ref[...] = _silu(seg(o_ga, o_ga + qa_w)).astype(BF16)

    def do_gb():
        gb_ref[...] = _silu(seg(o_gb, o_gb + qb_w)).astype(BF16)

    qm_scr[...] = (seg(o_qm, o_qm + MEM_W) * (M_HEAD_DIM ** -0.5 * LOG2E)).astype(BF16)
    gm_scr[...] = _silu(seg(o_gm, o_gm + MEM_W)).astype(BF16)
    _mem_attn_items(qm_scr, mkv_ref, gm_scr, ym_ref,
                    between=[do_qa, do_ga, do_qb, do_gb, do_va, do_ka, do_vb, do_kb])


def _in_even(x2d, gain, w_bf16, gq, gk, cos, sin, mkv, seq, tm):
    n_tok, d = x2d.shape
    n_in = w_bf16.shape[1]
    per_seq = seq // tm
    mlen = mkv.shape[2]
    widths = [A_HEADS * LANES, A_KV * HEAD_DIM, A_KV * HEAD_DIM,
              B_HEADS * LANES, B_KV * HEAD_DIM, B_KV * HEAD_DIM,
              A_HEADS * HEAD_DIM, B_HEADS * HEAD_DIM, MEM_W]
    row = lambda i: (i, 0)
    const = lambda i: (0, 0)
    out_shape = [jax.ShapeDtypeStruct((n_tok, w), BF16) for w in widths]
    out_specs = [pl.BlockSpec((tm, w), row) for w in widths]
    vrows = HEAD_DIM + V_ONES_ROWS
    out_shape[0] = jax.ShapeDtypeStruct((n_tok // tm, A_HEADS, LANES, tm), BF16)
    out_specs[0] = pl.BlockSpec((None, A_HEADS, LANES, tm), lambda i: (i, 0, 0, 0))
    out_shape[3] = jax.ShapeDtypeStruct((n_tok // tm, B_HEADS, LANES, tm), BF16)
    out_specs[3] = pl.BlockSpec((None, B_HEADS, LANES, tm), lambda i: (i, 0, 0, 0))
    out_shape[2] = jax.ShapeDtypeStruct((n_tok // A_CHUNK, A_KV, vrows, A_CHUNK), BF16)
    out_specs[2] = pl.BlockSpec((tm // A_CHUNK, A_KV, vrows, A_CHUNK), lambda i: (i, 0, 0, 0))
    out_shape[5] = jax.ShapeDtypeStruct((n_tok // WINDOW, B_KV, vrows, WINDOW), BF16)
    out_specs[5] = pl.BlockSpec((tm // WINDOW, B_KV, vrows, WINDOW), lambda i: (i, 0, 0, 0))
    return pl.pallas_call(
        _in_even_kernel,
        out_shape=out_shape,
        grid=(n_tok // tm,),
        in_specs=[pl.BlockSpec((tm, d), row),
                  pl.BlockSpec((1, d), const),
                  pl.BlockSpec((d, n_in), const, pipeline_mode=pl.Buffered(1)),
                  pl.BlockSpec((1, LANES), const),
                  pl.BlockSpec((1, LANES), const),
                  pl.BlockSpec((tm, LANES), lambda i: (i % per_seq, 0)),
                  pl.BlockSpec((tm, LANES), lambda i: (i % per_seq, 0)),
                  pl.BlockSpec((None, None, mlen, 2 * MEM_W), lambda i: (0, i // per_seq, 0, 0))],
        out_specs=out_specs,
        scratch_shapes=[pltpu.VMEM((tm, MEM_W), BF16), pltpu.VMEM((tm, MEM_W), BF16)],
        compiler_params=_params("arbitrary"),
        name="in_proj_even",
    )(x2d, gain, w_bf16, gq, gk, cos, sin, mkv)


def _mid_kernel(ya_ref, yb_ref, ym_ref, x_ref, wo_ref, g_ref, wi_ref, mkv_ref,
                x1_ref, qc_ref, kc_ref, vc_ref, gc_ref, ym1_ref, qm_scr, gm_scr):
    acc = x_ref[...]
    off = 0
    for y_ref in (ya_ref, yb_ref, ym_ref):
        width = y_ref.shape[1]
        acc = acc + jnp.dot(y_ref[...], wo_ref[off:off + width, :], preferred_element_type=F32)
        off += width
    x1_ref[...] = acc
    h = _rms_rows(acc, g_ref[...]).astype(BF16)

    def seg(lo, hi):
        return jnp.dot(h, wi_ref[:, lo:hi], preferred_element_type=F32)

    cw = C_HEADS * HEAD_DIM
    half = cw // 2
    o_qc, o_kc, o_vc, o_qm, o_gc, o_gm = 0, cw, 2 * cw, 3 * cw, 3 * cw + MEM_W, 4 * cw + MEM_W

    def thunk(ref, base, part, fn):
        def run():
            lo = part * half
            ref[:, lo:lo + half] = fn(seg(base + lo, base + lo + half)).astype(BF16)
        return run

    plain = lambda z: z
    to_q = lambda z: z * (HEAD_DIM ** -0.5 * LOG2E)
    segments = [thunk(ref, base, part, fn)
                for ref, base, fn in ((gc_ref, o_gc, _silu), (qc_ref, o_qc, to_q),
                                      (kc_ref, o_kc, plain), (vc_ref, o_vc, plain))
                for part in range(2)]
    qm_scr[...] = (seg(o_qm, o_qm + MEM_W) * (M_HEAD_DIM ** -0.5 * LOG2E)).astype(BF16)
    gm_scr[...] = _silu(seg(o_gm, o_gm + MEM_W)).astype(BF16)
    _mem_attn_items(qm_scr, mkv_ref, gm_scr, ym1_ref, between=segments)


def _mid(ya, yb, ym, x2d, wo_bf16, gain, wi_bf16, mkv, seq, tm):
    n_tok, d = x2d.shape
    per_seq = seq // tm
    mlen = mkv.shape[2]
    cw = C_HEADS * HEAD_DIM
    row = lambda i: (i, 0)
    const = lambda i: (0, 0)
    once = pl.Buffered(1)
    out_widths = [cw, cw, cw, cw, MEM_W]
    return pl.pallas_call(
        _mid_kernel,
        out_shape=[jax.ShapeDtypeStruct((n_tok, d), F32)]
                  + [jax.ShapeDtypeStruct((n_tok, w), BF16) for w in out_widths],
        grid=(n_tok // tm,),
        in_specs=[pl.BlockSpec((tm, ya.shape[1]), row),
                  pl.BlockSpec((tm, yb.shape[1]), row),
                  pl.BlockSpec((tm, ym.shape[1]), row),
                  pl.BlockSpec((tm, d), row),
                  pl.BlockSpec(wo_bf16.shape, const, pipeline_mode=once),
                  pl.BlockSpec((1, d), const),
                  pl.BlockSpec(wi_bf16.shape, const, pipeline_mode=once),
                  pl.BlockSpec((None, None, mlen, 2 * MEM_W), lambda i: (1, i // per_seq, 0, 0))],
        out_specs=[pl.BlockSpec((tm, d), row)] + [pl.BlockSpec((tm, w), row) for w in out_widths],
        scratch_shapes=[pltpu.VMEM((tm, MEM_W), BF16), pltpu.VMEM((tm, MEM_W), BF16)],
        compiler_params=_params("arbitrary"),
        name="out_proj_in_proj_odd",
    )(ya, yb, ym, x2d, wo_bf16, gain, wi_bf16, mkv)


def _global_attn_kernel(q_ref, k_ref, vt_ref, gate_ref, o_ref, m_scr, acc_scr, s_scr, mc_scr):
    n_qb = m_scr.shape[0]
    slab_w = q_ref.shape[3]
    tq = q_ref.shape[0] * slab_w // n_qb
    n_chunks = vt_ref.shape[0]
    tk = vt_ref.shape[3]
    group = A_HEADS // A_KV

    def q_block(b, h):
        lo = b * tq
        return q_ref[lo // slab_w, h][:, lo % slab_w:lo % slab_w + tq]

    qts = [[jnp.concatenate([q_block(b, g * group + i) for i in range(group)], axis=1)
            for g in range(A_KV)] for b in range(n_qb)]
    m_scr[...] = jnp.full(m_scr.shape, NEG, F32)
    acc_scr[...] = jnp.zeros(acc_scr.shape, F32)

    parts = 2
    tp = tk // parts
    pieces = [(p, g) for p in range(parts) for g in range(A_KV)]

    def scores_piece(b, c, slot, p, g):
        start = pl.multiple_of(c * tk + p * tp, tp)
        st = jnp.dot(k_ref[pl.ds(start, tp), :], qts[b][g], preferred_element_type=F32)
        s_scr[slot, g, p * tp:(p + 1) * tp] = st
        mx = jnp.max(st, axis=0, keepdims=True)
        mc_scr[slot, g] = mx if p == 0 else jnp.maximum(mc_scr[slot, g], mx)

    def accumulate_piece(b, c, slot, p, g):
        m_old = m_scr[b, g]
        m_new = jnp.maximum(m_old, mc_scr[slot, g])
        pt = jnp.exp2(s_scr[slot, g, p * tp:(p + 1) * tp] - m_new).astype(BF16)
        pv = jnp.dot(vt_ref[c, g][:, p * tp:(p + 1) * tp], pt, preferred_element_type=F32)
        if p == 0:
            acc_scr[b, g] = jnp.exp2(m_old - m_new) * acc_scr[b, g] + pv
        else:
            acc_scr[b, g] = acc_scr[b, g] + pv
        if p == parts - 1:
            m_scr[b, g] = m_new

    def scores_and_accumulate(score_of, accumulate_of):
        for p, g in pieces:
            if score_of is not None:
                scores_piece(*score_of, p, g)
            if accumulate_of is not None:
                accumulate_piece(*accumulate_of, p, g)

    def finalize(b):
        rows = slice(b * tq, (b + 1) * tq)
        ot = [acc_scr[b, g, :HEAD_DIM] * (1.0 / acc_scr[b, g, HEAD_DIM:HEAD_DIM + 1])
              for g in range(A_KV)]
        for j in range(A_HEADS // 2):
            g, i0 = (2 * j) // group, (2 * j) % group
            tile_t = jnp.concatenate([ot[g][:, i0 * tq:(i0 + 1) * tq],
                                      ot[g][:, (i0 + 1) * tq:(i0 + 2) * tq]], axis=0)
            gate = gate_ref[rows, j * LANES:(j + 1) * LANES].astype(F32)
            o_ref[rows, j * LANES:(j + 1) * LANES] = (tile_t.T * gate).astype(BF16)

    scores_and_accumulate((0, 0, 0), None)
    for b in range(n_qb):
        def body(c2, carry, b=b):
            c = 2 * c2
            scores_and_accumulate((b, c + 1, 1), (b, c, 0))
            scores_and_accumulate((b, c + 2, 0), (b, c + 1, 1))
            return carry

        lax.fori_loop(0, n_chunks // 2 - 1, body, 0)
        scores_and_accumulate((b, n_chunks - 1, 1), (b, n_chunks - 2, 0))
        following = (b + 1, 0, 0) if b + 1 < n_qb else None
        scores_and_accumulate(following, (b, n_chunks - 1, 1))
        finalize(b)


def _global_attn(qat, ka, vat, gate, tq, n_qb):
    b, s, _ = ka.shape
    _, n_kv, vrows, tk = vat.shape
    tm = qat.shape[-1]
    n_chunks = s // tk
    step_q = n_qb * tq
    assert n_chunks % 2 == 0 and (tm % step_q == 0 or step_q % tm == 0)
    rows = (A_HEADS // A_KV) * tq
    vat = vat.reshape(b, n_chunks, n_kv, vrows, tk)
    qat = qat.reshape(b, s // tm, A_HEADS, LANES, tm)
    if step_q >= tm:
        q_spec = pl.BlockSpec((None, step_q // tm, A_HEADS, LANES, tm), lambda bi, i: (bi, i, 0, 0, 0))
    else:
        per_slab = tm // step_q
        q_spec = pl.BlockSpec((None, 1, A_HEADS, LANES, step_q),
                              lambda bi, i: (bi, i // per_slab, 0, 0, i % per_slab))
    return pl.pallas_call(
        _global_attn_kernel,
        out_shape=jax.ShapeDtypeStruct((b, s, A_HEADS * HEAD_DIM), BF16),
        grid=(b, s // step_q),
        in_specs=[q_spec,
                  pl.BlockSpec((None, s, LANES), lambda bi, i: (bi, 0, 0)),
                  pl.BlockSpec((None, n_chunks, n_kv, vrows, tk), lambda bi, i: (bi, 0, 0, 0, 0)),
                  pl.BlockSpec((None, step_q, A_HEADS * HEAD_DIM), lambda bi, i: (bi, i, 0))],
        out_specs=pl.BlockSpec((None, step_q, A_HEADS * HEAD_DIM), lambda bi, i: (bi, i, 0)),
        scratch_shapes=[pltpu.VMEM((n_qb, A_KV, 1, rows), F32),
                        pltpu.VMEM((n_qb, A_KV, vrows, rows), F32),
                        pltpu.VMEM((2, A_KV, tk, rows), F32), pltpu.VMEM((2, A_KV, 1, rows), F32)],
        compiler_params=_params("arbitrary", "arbitrary"),
        name="global_attn",
    )(qat, ka, vat, gate)


def _window_attn_kernel(q_ref, k_ref, vt_ref, bias_ref, sink_ref, gate_ref, o_ref, *, blk):
    blocks_per_step = q_ref.shape[2] // blk
    nb = vt_ref.shape[0]
    step = pl.program_id(1)
    group = B_HEADS // B_KV

    def neighbours(t):
        i = step * blocks_per_step + t
        return jnp.maximum(i - 1, 0), i, jnp.minimum(i + 1, nb - 1)

    def scores(t, g):
        i = step * blocks_per_step + t
        qt = jnp.concatenate([q_ref[g * group + j][:, t * blk:(t + 1) * blk]
                              for j in range(group)], axis=1)
        k = jnp.concatenate([k_ref[pl.ds(pl.multiple_of(n * blk, blk), blk), :]
                             for n in neighbours(t)], axis=0)
        st = jnp.dot(k, qt, preferred_element_type=F32) + bias_ref[g]
        if t == 0:
            st = jnp.concatenate([st[:blk] + jnp.where(i == 0, NEG, 0.0), st[blk:]], axis=0)
        if t == blocks_per_step - 1:
            st = jnp.concatenate([st[:2 * blk], st[2 * blk:] + jnp.where(i == nb - 1, NEG, 0.0)], axis=0)
        m = jnp.maximum(jnp.max(st, axis=0, keepdims=True), sink_ref[g])
        return st, m

    def probs(st, m):
        return jnp.exp2(st - m).astype(BF16), m

    def output(t, g, pt, m):
        vt = jnp.concatenate([vt_ref[n, g] for n in neighbours(t)], axis=1)
        ot = jnp.dot(vt, pt, preferred_element_type=F32)
        denom = ot[HEAD_DIM:HEAD_DIM + 1] + jnp.exp2(sink_ref[g] - m)
        return ot[:HEAD_DIM] * (1.0 / denom)

    def store(t, ot):
        rows = slice(t * blk, (t + 1) * blk)
        for j in range(B_HEADS // 2):
            g, i0 = (2 * j) // group, (2 * j) % group
            tile_t = jnp.concatenate([ot[g][:, i0 * blk:(i0 + 1) * blk],
                                      ot[g][:, (i0 + 1) * blk:(i0 + 2) * blk]], axis=0)
            gate = gate_ref[rows, j * LANES:(j + 1) * LANES].astype(F32)
            o_ref[rows, j * LANES:(j + 1) * LANES] = (tile_t.T * gate).astype(BF16)

    items = [(t, g) for t in range(blocks_per_step) for g in range(B_KV)]
    sm, pb, outs = {}, {}, {}
    for n in range(len(items) + 2):
        if n < len(items):
            sm[n] = scores(*items[n])
        if 1 <= n <= len(items):
            pb[n - 1] = probs(*sm.pop(n - 1))
        if n >= 2:
            t, g = items[n - 2]
            outs[g] = output(t, g, *pb.pop(n - 2))
            if g == B_KV - 1:
                store(t, outs)


def _window_attn(qbt, kb, vbt, bias_t, sink_cols, gate, blk, blocks_per_step):
    b, s, _ = kb.shape
    nb = s // blk
    rows = (B_HEADS // B_KV) * blk
    tq = blk * blocks_per_step
    tm = qbt.shape[-1]
    assert tm % tq == 0
    per_slab = tm // tq
    vrows = vbt.shape[2]
    vbt = vbt.reshape(b, nb, B_KV, vrows, blk)
    qbt = qbt.reshape(b, s // tm, B_HEADS, LANES, tm)
    return pl.pallas_call(
        functools.partial(_window_attn_kernel, blk=blk),
        out_shape=jax.ShapeDtypeStruct((b, s, B_HEADS * HEAD_DIM), BF16),
        grid=(b, s // tq),
        in_specs=[pl.BlockSpec((None, None, B_HEADS, LANES, tq),
                               lambda bi, i: (bi, i // per_slab, 0, 0, i % per_slab)),
                  pl.BlockSpec((None, s, LANES), lambda bi, i: (bi, 0, 0)),
                  pl.BlockSpec((None, nb, B_KV, vrows, blk), lambda bi, i: (bi, 0, 0, 0, 0)),
                  pl.BlockSpec((B_KV, 3 * blk, rows), lambda bi, i: (0, 0, 0)),
                  pl.BlockSpec((B_KV, 1, rows), lambda bi, i: (0, 0, 0)),
                  pl.BlockSpec((None, tq, B_HEADS * HEAD_DIM), lambda bi, i: (bi, i, 0))],
        out_specs=pl.BlockSpec((None, tq, B_HEADS * HEAD_DIM), lambda bi, i: (bi, i, 0)),
        compiler_params=_params("arbitrary", "arbitrary"),
        name="window_attn",
    )(qbt, kb, vbt, bias_t, sink_cols, gate)


def _mem_attn_items(q_ref, kv_ref, gate_ref, o_ref, between=(), sub=256):
    between = list(between)
    mlen = kv_ref.shape[0]
    ones = jnp.ones((mlen, M_HEAD_DIM), BF16)

    def scores(t, h):
        lo, hi = h * M_HEAD_DIM, (h + 1) * M_HEAD_DIM
        s = lax.dot_general(q_ref[t * sub:(t + 1) * sub, lo:hi], kv_ref[:, lo:hi], _NT,
                            preferred_element_type=F32)
        return s, jnp.max(s, axis=-1, keepdims=True)

    def probs(s, m):
        return jnp.exp2(s - m).astype(BF16)

    def output(t, h, p):
        lo, hi = h * M_HEAD_DIM, (h + 1) * M_HEAD_DIM
        v = jnp.concatenate([kv_ref[:, MEM_W + lo:MEM_W + hi], ones], axis=1)
        o = jnp.dot(p, v, preferred_element_type=F32)
        o = o[:, :M_HEAD_DIM] * (1.0 / o[:, M_HEAD_DIM:])
        rows = slice(t * sub, (t + 1) * sub)
        o_ref[rows, lo:hi] = (o * gate_ref[rows, lo:hi].astype(F32)).astype(BF16)

    items = [(t, h) for t in range(q_ref.shape[0] // sub) for h in range(M_HEADS)]
    sm, pb = {}, {}
    for n in range(len(items) + 2):
        if n < len(items):
            sm[n] = scores(*items[n])
        if 1 <= n <= len(items):
            pb[n - 1] = probs(*sm.pop(n - 1))
        if n >= 2:
            output(*items[n - 2], pb.pop(n - 2))
        if between:
            between.pop(0)()
    for run in between:
        run()


def _nbr_attn_kernel(q_ref, k_ref, v_ref, bias_ref, gate_ref, o_ref, *, grid_rows, rows_per_step):
    rb = pl.program_id(2)
    nkeys = NA_ROWS * GRID_W
    n_pairs = q_ref.shape[1] // LANES
    ones = jnp.ones((nkeys, LANES), BF16)
    starts, first_offsets = [], []
    for t in range(rows_per_step):
        r = rb * rows_per_step + t
        rs = jnp.clip(r - NA_ROWS // 2, 0, grid_rows - NA_ROWS)
        starts.append(pl.multiple_of(rs * GRID_W, GRID_W))
        first_offsets.append(rs - r + NA_ROWS - 1)
    items = [(t, j) for t in range(rows_per_step) for j in range(n_pairs)]

    def scores(t, j):
        lo, hi = j * LANES, (j + 1) * LANES
        qt = q_ref[t * GRID_W:(t + 1) * GRID_W, lo:hi]
        lane = _lane_iota(qt.shape)
        zero = jnp.zeros_like(qt)
        q = jnp.concatenate([jnp.where(lane < HEAD_DIM, qt, zero),
                             jnp.where(lane >= HEAD_DIM, qt, zero)], axis=0)
        k = k_ref[pl.ds(starts[t], nkeys), lo:hi]
        par, m0 = first_offsets[t] % 2, first_offsets[t] // 2
        bias = jnp.concatenate([bias_ref[j, par, m0 + i] for i in range(NA_ROWS // 2)], axis=1)
        s = lax.dot_general(q, k, _NT, preferred_element_type=F32) + bias
        return s, jnp.max(s, axis=-1, keepdims=True)

    def probs(s, m):
        return jnp.exp2(s - m).astype(BF16)

    def output(t, j, p):
        lo, hi = j * LANES, (j + 1) * LANES
        v = jnp.concatenate([v_ref[pl.ds(starts[t], nkeys), lo:hi], ones], axis=1)
        o = jnp.dot(p, v, preferred_element_type=F32)
        o = o[:, :LANES] * (1.0 / o[:, LANES:])
        lane_o = _lane_iota((GRID_W, LANES))
        tile = jnp.where(lane_o < HEAD_DIM, o[:GRID_W, :], o[GRID_W:, :])
        rows = slice(t * GRID_W, (t + 1) * GRID_W)
        o_ref[rows, lo:hi] = (tile * gate_ref[rows, lo:hi].astype(F32)).astype(BF16)

    sm = {}
    pb = {}
    for n in range(len(items) + 2):
        if n < len(items):
            sm[n] = scores(*items[n])
        if 1 <= n <= len(items):
            pb[n - 1] = probs(*sm.pop(n - 1))
        if n >= 2:
            output(*items[n - 2], pb.pop(n - 2))


def _nbr_attn(qc, kc, vc, bias, gate, rows_per_step, head_splits):
    b, s, w = qc.shape
    grid_rows = s // GRID_W
    nkeys = NA_ROWS * GRID_W
    wh = w // head_splits
    tq = rows_per_step * GRID_W
    blk = lambda hh, bi, rb: (bi, rb, hh)
    return pl.pallas_call(
        functools.partial(_nbr_attn_kernel, grid_rows=grid_rows, rows_per_step=rows_per_step),
        out_shape=jax.ShapeDtypeStruct((b, s, w), BF16),
        grid=(head_splits, b, grid_rows // rows_per_step),
        in_specs=[pl.BlockSpec((None, tq, wh), blk),
                  pl.BlockSpec((None, s, wh), lambda hh, bi, rb: (bi, 0, hh)),
                  pl.BlockSpec((None, s, wh), lambda hh, bi, rb: (bi, 0, hh)),
                  pl.BlockSpec((wh // LANES,) + bias.shape[1:], lambda hh, bi, rb: (hh, 0, 0, 0, 0)),
                  pl.BlockSpec((None, tq, wh), blk)],
        out_specs=pl.BlockSpec((None, tq, wh), blk),
        compiler_params=_params("arbitrary", "arbitrary", "arbitrary"),
        name="nbr_attn",
    )(qc, kc, vc, bias, gate)


def _out_final_kernel(yc_ref, ym_ref, x_ref, w_ref, g_ref, o_ref):
    acc = x_ref[...]
    off = 0
    for y_ref in (yc_ref, ym_ref):
        width = y_ref.shape[1]
        acc = acc + jnp.dot(y_ref[...], w_ref[off:off + width, :], preferred_element_type=F32)
        off += width
    o_ref[...] = _rms_rows(acc, g_ref[...])


def _out_final(yc, ym, x2d, w_bf16, final_gain, tm):
    n_tok, d = x2d.shape
    row = lambda i: (i, 0)
    const = lambda i: (0, 0)
    return pl.pallas_call(
        _out_final_kernel,
        out_shape=jax.ShapeDtypeStruct((n_tok, d), F32),
        grid=(n_tok // tm,),
        in_specs=[pl.BlockSpec((tm, yc.shape[1]), row),
                  pl.BlockSpec((tm, ym.shape[1]), row),
                  pl.BlockSpec((tm, d), row),
                  pl.BlockSpec(w_bf16.shape, const),
                  pl.BlockSpec((1, d), const)],
        out_specs=pl.BlockSpec((tm, d), row),
        compiler_params=_params("arbitrary"),
        name="out_proj_final",
    )(yc, ym, x2d, w_bf16, final_gain)


def _rope_tables(seq):
    quarter = HEAD_DIM // 4
    freqs = jnp.power(ROPE_THETA, -jnp.arange(quarter, dtype=F32) / quarter)
    t = jnp.arange(seq)
    ang_r = (t // GRID_W).astype(F32)[:, None] * freqs
    ang_c = (t % GRID_W).astype(F32)[:, None] * freqs
    cos_h = jnp.concatenate([jnp.cos(ang_r), jnp.cos(ang_r), jnp.cos(ang_c), jnp.cos(ang_c)], axis=-1)
    sin_h = jnp.concatenate([-jnp.sin(ang_r), jnp.sin(ang_r), -jnp.sin(ang_c), jnp.sin(ang_c)], axis=-1)
    return jnp.tile(cos_h, (1, 2)), jnp.tile(sin_h, (1, 2))


def _t5_bucket(rel):
    nb = REL_BUCKETS // 2
    max_exact = nb // 2
    ret = jnp.where(rel > 0, nb, 0)
    n = jnp.abs(rel)
    nf = jnp.maximum(n, 1).astype(F32)
    large = max_exact + (jnp.log(nf / max_exact) / math.log(REL_MAX_DIST / max_exact)
                         * (nb - max_exact)).astype(jnp.int32)
    large = jnp.minimum(large, nb - 1)
    return ret + jnp.where(n < max_exact, n, large)


def _window_bias(rel_bias, blk):
    span = blk + 2 * WINDOW
    reach = span - WINDOW - 1
    rel = jnp.arange(-reach, reach + 1)
    per_rel = jnp.where((jnp.abs(rel) <= WINDOW)[:, None],
                        rel_bias.astype(F32)[_t5_bucket(rel)] * LOG2E, NEG).T
    bias = jnp.stack([per_rel[:, reach - WINDOW - q:reach - WINDOW - q + span] for q in range(blk)],
                     axis=-1)
    group = B_HEADS // B_KV
    bias = bias.reshape(B_KV, group, span, blk).transpose(0, 2, 1, 3)
    return bias.reshape(B_KV, span, group * blk)


def _nbr_bias(rpb):
    col = np.arange(GRID_W)
    cs = np.clip(col - NA_COLS // 2, 0, GRID_W - NA_COLS)
    colmask = (col[None, :] >= cs[:, None]) & (col[None, :] < cs[:, None] + NA_COLS)
    dc = np.clip(col[None, :] - col[:, None] + NA_COLS - 1, 0, 2 * NA_COLS - 2)
    onehot = (dc[None] == np.arange(2 * NA_COLS - 1)[:, None, None]) & colmask[None]
    t = jnp.einsum('hrc,cqk->hrqk', rpb.astype(F32) * LOG2E, jnp.asarray(onehot, F32),
                   precision=lax.Precision.HIGHEST)
    t = t + jnp.asarray(np.where(colmask, 0.0, NEG), F32)
    n_tiles = NA_ROWS - 1
    tiles = jnp.stack([jnp.concatenate([t[:, par:par + 2 * n_tiles:2], t[:, par + 1:par + 1 + 2 * n_tiles:2]],
                                       axis=-1) for par in range(2)], axis=1)
    tiles = tiles.reshape(C_HEADS // 2, 2, 2, n_tiles, GRID_W, LANES).transpose(0, 2, 3, 1, 4, 5)
    return tiles.reshape(C_HEADS // 2, 2, n_tiles, 2 * GRID_W, LANES)


def kernel(x, mem, norm_gain, mem_norm_gain, w_in_even, w_out_even, q_norm_a, k_norm_a, sink_b,
           rel_bias, w_in_odd, w_out_odd, rpb_c, w_mem_kv, final_norm_gain):
    b, s, d = x.shape
    mlen = mem.shape[1]
    assert s % GRID_W == 0 and s // GRID_W >= NA_ROWS and s % 512 == 0
    tm = 512
    blk = 128

    x2d = x.reshape(b * s, d)
    mem_tm = math.gcd(b * mlen, 512)
    mkv = _mem_kv(mem.reshape(b * mlen, d), mem_norm_gain.reshape(1, d), w_mem_kv.astype(BF16), mem_tm)
    mkv = mkv.reshape(w_mem_kv.shape[0], b, mlen, 2 * MEM_W)

    cos, sin = _rope_tables(s)
    gq = jnp.tile(q_norm_a[0].astype(F32), 2).reshape(1, LANES)
    gk = jnp.tile(k_norm_a[0].astype(F32), 2).reshape(1, LANES)
    qa, ka, va, qb, kb, vb, ga, gb, ym = _in_even(
        x2d, norm_gain[0].reshape(1, d), w_in_even[0].astype(BF16), gq, gk, cos, sin, mkv, s,
        math.gcd(s, 2 * tm))
    r3 = lambda a: a.reshape(b, s, a.shape[-1])
    ya = _global_attn(qa, r3(ka), va, r3(ga), tq=256, n_qb=4)
    group_b = B_HEADS // B_KV
    sink_cols = jnp.repeat(sink_b[0].astype(F32).reshape(B_KV, group_b) * LOG2E, blk,
                           axis=1).reshape(B_KV, 1, group_b * blk)
    yb = _window_attn(qb, r3(kb), vb, _window_bias(rel_bias, blk), sink_cols, r3(gb), blk,
                      blocks_per_step=8)

    x1, qc, kc, vc, gc, ym1 = _mid(ya.reshape(b * s, -1), yb.reshape(b * s, -1), ym, x2d,
                                   w_out_even[0].astype(BF16), norm_gain[1].reshape(1, d),
                                   w_in_odd[0].astype(BF16), mkv, s, tm)
    yc = _nbr_attn(r3(qc), r3(kc), r3(vc), _nbr_bias(rpb_c[0]), r3(gc), rows_per_step=32, head_splits=2)
    out = _out_final(yc.reshape(b * s, -1), ym1, x1, w_out_odd[0].astype(BF16),
                     final_norm_gain.reshape(1, d), 2 * tm)
    return out.reshape(b, s, d)
```
